```python
import math
import jax, jax.numpy as jnp
from jax import lax
import numpy as np

D_MODEL = 2048
BATCH = 8
SEQ = 2048
DEPTH = 1
DEC_BATCH = 8
DEC_SEQ = 16
PAST_LEN = 4096

CHUNK = 64
HEAD_DIM = 64
N_Q_HEADS = 16
N_KV_HEADS = 4
GROUP = N_Q_HEADS // N_KV_HEADS
ATTN_DIM = N_Q_HEADS * HEAD_DIM
KV_DIM = N_KV_HEADS * HEAD_DIM
WINDOW = 128
WIN_CHUNKS = WINDOW // CHUNK
CONV_DIM = D_MODEL // 2
CONV_WIDTH = 3
N_BUCKETS = 32
MAX_DISTANCE = 128
N_EXPERTS = 32
TOP_K = 4
D_FF = D_MODEL
SWIGLU_LIMIT = 7.0
SWIGLU_ALPHA = 1.702
MOE_BLOCK = 128
NORM_EPS = 1e-5
IN_DIM = ATTN_DIM + 2 * KV_DIM + 3 * CONV_DIM + 2 * D_MODEL
SPLIT_POINTS = (ATTN_DIM,
                ATTN_DIM + KV_DIM,
                ATTN_DIM + 2 * KV_DIM,
                ATTN_DIM + 2 * KV_DIM + CONV_DIM,
                ATTN_DIM + 2 * KV_DIM + 2 * CONV_DIM,
                ATTN_DIM + 2 * KV_DIM + 3 * CONV_DIM,
                ATTN_DIM + 2 * KV_DIM + 3 * CONV_DIM + D_MODEL)

kernel_name = "hybrid_swa_sink_shortconv_moe_stream_step"


def rmsnorm(x, g):
    xf = x.astype(jnp.float32)
    y = xf * lax.rsqrt(jnp.mean(xf * xf, axis=-1, keepdims=True) + NORM_EPS)
    return (y * g.astype(jnp.float32)).astype(x.dtype)


def t5_bucket(rel):
    half = N_BUCKETS // 2
    max_exact = half // 2
    ret = jnp.where(rel > 0, half, 0)
    n = jnp.abs(rel)
    nf = jnp.maximum(n, 1).astype(jnp.float32)
    large = max_exact + (jnp.log(nf / max_exact) / math.log(MAX_DISTANCE / max_exact)
                         * (half - max_exact)).astype(jnp.int32)
    large = jnp.minimum(large, half - 1)
    return ret + jnp.where(n < max_exact, n, large)


def rel_bias(table, n_q, n_k):
    rel = (jnp.arange(n_k, dtype=jnp.int32)[None, :] - WINDOW) - jnp.arange(n_q, dtype=jnp.int32)[:, None]
    b = table.astype(jnp.float32)[t5_bucket(rel)]
    return jnp.transpose(b, (2, 0, 1)).reshape(N_KV_HEADS, GROUP, n_q, n_k)


def sink_attention(q, k, v, bias, valid, sinks):
    s = jnp.einsum('nqkgd,njkd->nkgqj', q, k, preferred_element_type=jnp.float32) * (HEAD_DIM ** -0.5) + bias
    s = jnp.where(valid[:, None, None, None, :], s, -1e30)
    sink = sinks.astype(jnp.float32).reshape(N_KV_HEADS, GROUP)[None, :, :, None]
    m = jnp.maximum(jnp.max(s, axis=-1), sink)
    p = jnp.exp(s - m[..., None])
    denom = jnp.sum(p, axis=-1) + jnp.exp(sink - m)
    o = jnp.einsum('nkgqj,njkd->nqkgd', p.astype(v.dtype), v, preferred_element_type=jnp.float32)
    o = o / jnp.transpose(denom, (0, 3, 1, 2))[..., None]
    return o.astype(q.dtype)


def project(xn, w_in, b_gate):
    z = xn @ w_in
    q, k, v, cb, cc, ch, ga, gc = jnp.split(z, SPLIT_POINTS, axis=-1)
    bg = b_gate.astype(jnp.float32)
    g_a = jax.nn.sigmoid(ga.astype(jnp.float32) + bg[:D_MODEL]).astype(xn.dtype)
    g_c = jax.nn.sigmoid(gc.astype(jnp.float32) + bg[D_MODEL:]).astype(xn.dtype)
    u = cc * ch
    return q, k, v, cb, u, g_a, g_c


def conv_mix(u_ext, conv_w, n):
    y = conv_w[0] * u_ext[:, 0:n]
    for i in range(1, CONV_WIDTH):
        y = y + conv_w[i] * u_ext[:, i:i + n]
    return y


def merge(o_attn, o_conv, g_a, g_c, w_o_attn, w_o_conv, w_out):
    return (g_a * (o_attn @ w_o_attn) + g_c * (o_conv @ w_o_conv)) @ w_out


def band_blocks(k, nc):
    b = k.shape[0]
    kp = jnp.pad(k, ((0, 0), (WINDOW, 0), (0, 0), (0, 0))).reshape(b, nc + WIN_CHUNKS, CHUNK, N_KV_HEADS, HEAD_DIM)
    kb = jnp.concatenate([kp[:, i:i + nc] for i in range(WIN_CHUNKS + 1)], axis=2)
    return kb.reshape(b * nc, (WIN_CHUNKS + 1) * CHUNK, N_KV_HEADS, HEAD_DIM)


def mixer_prompt(xn, w_in, b_gate, sinks, conv_w, w_o_attn, w_o_conv, w_out, rel_table):
    b, s, _ = xn.shape
    nc = s // CHUNK
    q, k, v, cb, u, g_a, g_c = project(xn, w_in, b_gate)
    q = q.reshape(b * nc, CHUNK, N_KV_HEADS, GROUP, HEAD_DIM)
    k = k.reshape(b, s, N_KV_HEADS, HEAD_DIM)
    v = v.reshape(b, s, N_KV_HEADS, HEAD_DIM)
    n_keys = (WIN_CHUNKS + 1) * CHUNK
    key_pos = jnp.arange(nc)[:, None] * CHUNK + jnp.arange(n_keys)[None, :] - WINDOW
    valid = jnp.broadcast_to((key_pos >= 0)[None], (b, nc, n_keys)).reshape(b * nc, n_keys)
    bias = rel_bias(rel_table, CHUNK, n_keys)
    o = sink_attention(q, band_blocks(k, nc), band_blocks(v, nc), bias, valid, sinks).reshape(b, s, ATTN_DIM)
    u_pad = jnp.pad(u, ((0, 0), (CONV_WIDTH - 1, 0), (0, 0)))
    c = cb * conv_mix(u_pad, conv_w, s)
    out = merge(o, c, g_a, g_c, w_o_attn, w_o_conv, w_out)
    return out, k[:, -WINDOW:], v[:, -WINDOW:], u[:, -(CONV_WIDTH - 1):]


def mixer_sample(xn, ck, cv, cu, w_in, b_gate, sinks, conv_w, w_o_attn, w_o_conv, w_out, rel_table):
    b, n, _ = xn.shape
    q, k, v, cb, u, g_a, g_c = project(xn, w_in, b_gate)
    q = q.reshape(b, n, N_KV_HEADS, GROUP, HEAD_DIM)
    k_ext = jnp.concatenate([ck.astype(k.dtype), k.reshape(b, n, N_KV_HEADS, HEAD_DIM)], axis=1)
    v_ext = jnp.concatenate([cv.astype(v.dtype), v.reshape(b, n, N_KV_HEADS, HEAD_DIM)], axis=1)
    valid = jnp.ones((b, WINDOW + n), dtype=bool)
    bias = rel_bias(rel_table, n, WINDOW + n)
    o = sink_attention(q, k_ext, v_ext, bias, valid, sinks).reshape(b, n, ATTN_DIM)
    u_ext = jnp.concatenate([cu.astype(u.dtype), u], axis=1)
    c = cb * conv_mix(u_ext, conv_w, n)
    out = merge(o, c, g_a, g_c, w_o_attn, w_o_conv, w_out)
    return out, k_ext[:, -WINDOW:], v_ext[:, -WINDOW:], u_ext[:, -(CONV_WIDTH - 1):]


def moe_ffn(h, w_router, b_router, w_gate_up, b_gate_up, w_down, b_down):
    shape = h.shape
    x = h.reshape(-1, D_MODEL)
    t = x.shape[0]
    logits = x.astype(jnp.float32) @ w_router.astype(jnp.float32) + b_router.astype(jnp.float32)
    top_v, top_i = lax.top_k(logits, TOP_K)
    top_w = jax.nn.softmax(top_v, axis=-1)
    a = t * TOP_K
    e_flat = top_i.reshape(a).astype(jnp.int32)
    t_flat = jnp.repeat(jnp.arange(t, dtype=jnp.int32), TOP_K)
    w_flat = top_w.reshape(a)
    order = jnp.argsort(e_flat * a + jnp.arange(a, dtype=jnp.int32))
    e_sorted = e_flat[order]
    t_sorted = t_flat[order]
    counts = jnp.bincount(e_flat, length=N_EXPERTS)
    padded = (counts + MOE_BLOCK - 1) // MOE_BLOCK * MOE_BLOCK
    start = jnp.cumsum(counts) - counts
    pstart = jnp.cumsum(padded) - padded
    dest = pstart[e_sorted] + jnp.arange(a, dtype=jnp.int32) - start[e_sorted]
    n_blocks = (a + N_EXPERTS * (MOE_BLOCK - 1) + MOE_BLOCK - 1) // MOE_BLOCK
    xs = jnp.zeros((n_blocks * MOE_BLOCK, D_MODEL), x.dtype).at[dest].set(x[t_sorted])
    block_e = jnp.minimum(jnp.searchsorted(jnp.cumsum(padded), jnp.arange(n_blocks) * MOE_BLOCK, side='right'),
                          N_EXPERTS - 1)

    def expert_block(args):
        xb, e = args
        gu = xb @ w_gate_up[e] + b_gate_up[e]
        gate, up = gu[:, :D_FF], gu[:, D_FF:]
        gate = jnp.minimum(gate, SWIGLU_LIMIT)
        up = jnp.clip(up, -SWIGLU_LIMIT, SWIGLU_LIMIT)
        hh = (up + 1) * (gate * jax.nn.sigmoid(SWIGLU_ALPHA * gate))
        return hh @ w_down[e] + b_down[e]

    ys = lax.map(expert_block, (xs.reshape(n_blocks, MOE_BLOCK, D_MODEL), block_e)).reshape(-1, D_MODEL)
    y_assign = ys[dest] * w_flat[order][:, None].astype(x.dtype)
    y = jnp.zeros((t, D_MODEL), x.dtype).at[t_sorted].add(y_assign)
    return y.reshape(shape)


def setup_inputs(seed: int = 0) -> dict:
    key = jax.random.key(seed)
    ks = jax.random.split(key, 24)
    f32 = jnp.float32
    nrm = lambda k, shp, sc: jax.random.normal(k, shp, f32) * sc
    return {
        "x_prompt": nrm(ks[0], (BATCH, SEQ, D_MODEL), 1.0),
        "x_sample": nrm(ks[1], (DEC_BATCH, DEC_SEQ, D_MODEL), 1.0),
        "cache_attn_k": nrm(ks[2], (DEPTH, DEC_BATCH, WINDOW, N_KV_HEADS, HEAD_DIM), 1.0),
        "cache_attn_v": nrm(ks[3], (DEPTH, DEC_BATCH, WINDOW, N_KV_HEADS, HEAD_DIM), 1.0),
        "state_conv": nrm(ks[4], (DEPTH, DEC_BATCH, CONV_WIDTH - 1, CONV_DIM), 1.0),
        "rel_bias_table": nrm(ks[5], (N_BUCKETS, N_Q_HEADS), 0.5),
        "norm_mix_g": 1.0 + nrm(ks[6], (DEPTH, D_MODEL), 0.02),
        "w_in": nrm(ks[7], (DEPTH, D_MODEL, IN_DIM), D_MODEL ** -0.5),
        "b_gate": nrm(ks[8], (DEPTH, 2 * D_MODEL), 0.02),
        "attn_sinks": nrm(ks[9], (DEPTH, N_Q_HEADS), 1.0),
        "conv_w": nrm(ks[10], (DEPTH, CONV_WIDTH, CONV_DIM), CONV_WIDTH ** -0.5),
        "w_o_attn": nrm(ks[11], (DEPTH, ATTN_DIM, D_MODEL), ATTN_DIM ** -0.5),
        "w_o_conv": nrm(ks[12], (DEPTH, CONV_DIM, D_MODEL), CONV_DIM ** -0.5),
        "w_out": nrm(ks[13], (DEPTH, D_MODEL, D_MODEL), D_MODEL ** -0.5),
        "norm_ffn_g": 1.0 + nrm(ks[14], (DEPTH, D_MODEL), 0.02),
        "w_router": nrm(ks[15], (DEPTH, D_MODEL, N_EXPERTS), D_MODEL ** -0.5),
        "b_router": nrm(ks[16], (DEPTH, N_EXPERTS), 0.01),
        "w_gate_up": nrm(ks[17], (DEPTH, N_EXPERTS, D_MODEL, 2 * D_FF), D_MODEL ** -0.5),
        "b_gate_up": nrm(ks[18], (DEPTH, N_EXPERTS, 2 * D_FF), 0.01),
        "w_down": nrm(ks[19], (DEPTH, N_EXPERTS, D_FF, D_MODEL), D_FF ** -0.5),
        "b_down": nrm(ks[20], (DEPTH, N_EXPERTS, D_MODEL), 0.01),
        "norm_final_g": 1.0 + nrm(ks[21], (D_MODEL,), 0.02),
    }


def reference(x_prompt, x_sample, cache_attn_k, cache_attn_v, state_conv, rel_bias_table,
              norm_mix_g, w_in, b_gate, attn_sinks, conv_w, w_o_attn, w_o_conv, w_out,
              norm_ffn_g, w_router, b_router, w_gate_up, b_gate_up, w_down, b_down, norm_final_g):
    hp, hs = x_prompt, x_sample
    kp_l, vp_l, up_l, ks_l, vs_l, us_l = [], [], [], [], [], []
    for l in range(DEPTH):
        mp, kp, vp, up = mixer_prompt(rmsnorm(hp, norm_mix_g[l]), w_in[l], b_gate[l], attn_sinks[l], conv_w[l],
                                      w_o_attn[l], w_o_conv[l], w_out[l], rel_bias_table)
        ms, ksn, vsn, usn = mixer_sample(rmsnorm(hs, norm_mix_g[l]), cache_attn_k[l], cache_attn_v[l], state_conv[l],
                                         w_in[l], b_gate[l], attn_sinks[l], conv_w[l],
                                         w_o_attn[l], w_o_conv[l], w_out[l], rel_bias_table)
        hp = hp + mp
        hs = hs + ms
        hp = hp + moe_ffn(rmsnorm(hp, norm_ffn_g[l]), w_router[l], b_router[l], w_gate_up[l], b_gate_up[l],
                          w_down[l], b_down[l])
        hs = hs + moe_ffn(rmsnorm(hs, norm_ffn_g[l]), w_router[l], b_router[l], w_gate_up[l], b_gate_up[l],
                          w_down[l], b_down[l])
        kp_l.append(kp); vp_l.append(vp); up_l.append(up)
        ks_l.append(ksn); vs_l.append(vsn); us_l.append(usn)
    y_prompt = rmsnorm(hp, norm_final_g)
    y_sample = rmsnorm(hs, norm_final_g)
    new_k_prompt = jnp.stack(kp_l, 0)
    new_v_prompt = jnp.stack(vp_l, 0)
    new_conv_prompt = jnp.stack(up_l, 0)
    new_k_sample = jnp.stack(ks_l, 0)
    new_v_sample = jnp.stack(vs_l, 0)
    new_conv_sample = jnp.stack(us_l, 0)
    return (y_prompt, y_sample, new_k_prompt, new_v_prompt, new_conv_prompt, new_k_sample, new_v_sample, new_conv_sample)
```

```python
import functools
import math

import numpy as np
import jax
import jax.numpy as jnp
from jax import lax
from jax.experimental import pallas as pl
from jax.experimental.pallas import tpu as pltpu

F32 = jnp.float32
BF16 = jnp.bfloat16
I32 = jnp.int32

D_MODEL = 2048
CHUNK = 64
HEAD_DIM = 64
N_Q_HEADS = 16
N_KV_HEADS = 4
GROUP = N_Q_HEADS // N_KV_HEADS
ATTN_DIM = N_Q_HEADS * HEAD_DIM
KV_DIM = N_KV_HEADS * HEAD_DIM
WINDOW = 128
CONV_DIM = D_MODEL // 2
N_BUCKETS = 32
MAX_DISTANCE = 128
N_EXPERTS = 32
TOP_K = 4
D_FF = D_MODEL
SWIGLU_LIMIT = 7.0
SWIGLU_ALPHA = 1.702
NORM_EPS = 1e-5
IN_DIM = ATTN_DIM + 2 * KV_DIM + 3 * CONV_DIM + 2 * D_MODEL

V7X_VMEM_BYTES = 64 * 1024 * 1024
LANES = 128

ZC_GA = 0
ZC_GC = D_MODEL
ZC_CB = 2 * D_MODEL
ZC_U = ZC_CB + CONV_DIM
ZC_Q = ZC_U + CONV_DIM
ZC_K = ZC_Q + ATTN_DIM
ZC_V = ZC_K + KV_DIM
Z_DIM = ZC_V + KV_DIM
PROJ_TN = 512

ROW_BLK = 256
SUPER_ROWS = 9 * ROW_BLK
FF_TILE = 256
N_FF_TILES = D_FF // FF_TILE


def _vmem_limit(nbytes):
    return int(min(nbytes, V7X_VMEM_BYTES - 6 * 1024 * 1024))


_MODE_PLAIN, _MODE_STASH0, _MODE_STASH1, _MODE_MUL0, _MODE_MUL1, _MODE_GATE = range(6)


def _proj_schedule():
    t = PROJ_TN
    cc0 = (ATTN_DIM + 2 * KV_DIM + CONV_DIM) // t
    ch0 = (ATTN_DIM + 2 * KV_DIM + 2 * CONV_DIM) // t
    cb0 = (ATTN_DIM + 2 * KV_DIM) // t
    ga0 = (ATTN_DIM + 2 * KV_DIM + 3 * CONV_DIM) // t
    steps = [
        (cc0, ZC_U // t, 0, _MODE_STASH0),
        (cc0 + 1, ZC_U // t, 0, _MODE_STASH1),
        (ch0, ZC_U // t, 0, _MODE_MUL0),
        (ch0 + 1, ZC_U // t + 1, 0, _MODE_MUL1),
        (0, ZC_Q // t, 0, _MODE_PLAIN),
        (1, ZC_Q // t + 1, 0, _MODE_PLAIN),
        (2, ZC_K // t, 0, _MODE_PLAIN),
        (cb0, ZC_CB // t, 0, _MODE_PLAIN),
        (cb0 + 1, ZC_CB // t + 1, 0, _MODE_PLAIN),
    ]
    for j in range(2 * D_MODEL // t):
        steps.append((ga0 + j, j, j, _MODE_GATE))
    return np.asarray(steps, dtype=np.int32)


def _in_proj_kernel(wt_ref, ot_ref, bt_ref, md_ref, x_ref, g_ref, w_ref, b_ref, o_ref, xn_ref, stash_ref):
    del wt_ref, ot_ref, bt_ref
    n = pl.program_id(1)

    @pl.when(n == 0)
    def _():
        x = x_ref[...]
        ms = jnp.mean(x * x, axis=-1, keepdims=True)
        xn_ref[...] = ((x * lax.rsqrt(ms + NORM_EPS)) * g_ref[...]).astype(BF16)

    acc = jnp.dot(xn_ref[...], w_ref[...], preferred_element_type=F32)
    md = md_ref[n]

    @pl.when(md == _MODE_PLAIN)
    def _():
        o_ref[...] = acc

    @pl.when(md == _MODE_STASH0)
    def _():
        stash_ref[0] = acc

    @pl.when(md == _MODE_STASH1)
    def _():
        stash_ref[1] = acc

    @pl.when(md == _MODE_MUL0)
    def _():
        o_ref[...] = stash_ref[0] * acc

    @pl.when(md == _MODE_MUL1)
    def _():
        o_ref[...] = stash_ref[1] * acc

    @pl.when(md == _MODE_GATE)
    def _():
        o_ref[...] = jax.nn.sigmoid(acc + b_ref[...])


def _in_proj(x, g_mix, w_in_bf, b_gate, tm):
    t, d = x.shape
    sched = _proj_schedule()
    n_steps = sched.shape[0]
    grid_spec = pltpu.PrefetchScalarGridSpec(
        num_scalar_prefetch=4,
        grid=(t // tm, n_steps),
        in_specs=[
            pl.BlockSpec((tm, d), lambda m, n, wt, ot, bt, md: (m, 0)),
            pl.BlockSpec((1, d), lambda m, n, wt, ot, bt, md: (0, 0)),
            pl.BlockSpec((d, PROJ_TN), lambda m, n, wt, ot, bt, md: (0, wt[n])),
            pl.BlockSpec((1, PROJ_TN), lambda m, n, wt, ot, bt, md: (0, bt[n])),
        ],
        out_specs=pl.BlockSpec((tm, PROJ_TN), lambda m, n, wt, ot, bt, md: (m, ot[n])),
        scratch_shapes=[pltpu.VMEM((tm, d), BF16), pltpu.VMEM((2, tm, PROJ_TN), F32)],
    )
    vmem = 2 * tm * d * 4 + tm * d * 2 + 2 * d * PROJ_TN * 2 + 6 * tm * PROJ_TN * 4 + (8 << 20)
    return pl.pallas_call(
        _in_proj_kernel,
        grid_spec=grid_spec,
        out_shape=jax.ShapeDtypeStruct((t, Z_DIM), F32),
        compiler_params=pltpu.CompilerParams(
            dimension_semantics=("arbitrary", "arbitrary"), vmem_limit_bytes=_vmem_limit(vmem)),
        name="in_proj",
    )(jnp.asarray(sched[:, 0]), jnp.asarray(sched[:, 1]), jnp.asarray(sched[:, 2]), jnp.asarray(sched[:, 3]),
      x, g_mix.reshape(1, d), w_in_bf, b_gate.reshape(1, 2 * D_MODEL))


def _t5_buckets(n_q, n_k):
    half = N_BUCKETS // 2
    max_exact = half // 2
    rel = (np.arange(n_k, dtype=np.int32)[None, :] - WINDOW) - np.arange(n_q, dtype=np.int32)[:, None]
    ret = np.where(rel > 0, half, 0)
    n = np.abs(rel)
    nf = np.maximum(n, 1).astype(np.float32)
    large = max_exact + (np.log(nf / np.float32(max_exact)) / np.float32(math.log(MAX_DISTANCE / max_exact))
                         * np.float32(half - max_exact)).astype(np.int32)
    large = np.minimum(large, half - 1)
    return (ret + np.where(n < max_exact, n, large)).astype(np.int32)


def _build_bias(bias_ref, bucket_ref, table_ref, n_q):
    bucket = bucket_ref[...]
    for kvh in range(N_KV_HEADS):
        for g in range(GROUP):
            h = kvh * GROUP + g
            mat = jnp.zeros(bucket.shape, F32)
            for b in range(N_BUCKETS):
                mat = jnp.where(bucket == b, table_ref[b, h], mat)
            bias_ref[kvh, g * n_q:(g + 1) * n_q, :] = mat


def _attend(q, kwin, vwin, bias_ref, sinks_ref, n_q, invalid):
    outs = []
    grp = lax.broadcasted_iota(I32, (GROUP * n_q, 1), 0) // n_q
    for kvh in range(N_KV_HEADS):
        c0 = kvh * GROUP * HEAD_DIM
        qh = jnp.concatenate([q[:, c0 + g * HEAD_DIM:c0 + (g + 1) * HEAD_DIM] for g in range(GROUP)], axis=0)
        qh = (qh * (HEAD_DIM ** -0.5)).astype(BF16)
        kh = kwin[:, kvh * HEAD_DIM:(kvh + 1) * HEAD_DIM].astype(BF16)
        vh = vwin[:, kvh * HEAD_DIM:(kvh + 1) * HEAD_DIM].astype(BF16)
        s = lax.dot_general(qh, kh, (((1,), (1,)), ((), ())), preferred_element_type=F32) + bias_ref[kvh]
        if invalid is not None:
            s = jnp.where(invalid, -1e30, s)
        sink = jnp.zeros((GROUP * n_q, 1), F32)
        for g in range(GROUP):
            sink = jnp.where(grp == g, sinks_ref[kvh * GROUP + g], sink)
        m = jnp.maximum(jnp.max(s, axis=-1, keepdims=True), sink)
        p = jnp.exp(s - m)
        denom = jnp.sum(p, axis=-1, keepdims=True) + jnp.exp(sink - m)
        o = jnp.dot(p.astype(BF16), vh, preferred_element_type=F32) / denom
        outs.append(o)
    return outs


def _store_heads(o_ref, r0, n_q, outs):
    for kvh, o in enumerate(outs):
        for g in range(GROUP):
            c = (kvh * GROUP + g) * HEAD_DIM
            o_ref[r0:r0 + n_q, c:c + HEAD_DIM] = o[g * n_q:(g + 1) * n_q].astype(o_ref.dtype)


def _attn_prompt_kernel(table_ref, sinks_ref, bucket_ref, q_ref, kp_ref, kc_ref, vp_ref, vc_ref, o_ref, bias_ref):
    first = (pl.program_id(0) == 0) & (pl.program_id(1) == 0)

    @pl.when(first)
    def _():
        _build_bias(bias_ref, bucket_ref, table_ref, CHUNK)

    kwin = jnp.concatenate([kp_ref[...], kc_ref[...]], axis=0)
    vwin = jnp.concatenate([vp_ref[...], vc_ref[...]], axis=0)
    n_k = WINDOW + CHUNK
    seq_start = pl.program_id(1) == 0
    col = lax.broadcasted_iota(I32, (1, n_k), 1)
    for c in range(WINDOW // CHUNK):
        r0 = c * CHUNK
        invalid = seq_start & (col + r0 < WINDOW)
        outs = _attend(q_ref[r0:r0 + CHUNK, :], kwin[r0:r0 + n_k], vwin[r0:r0 + n_k], bias_ref, sinks_ref,
                       CHUNK, invalid)
        _store_heads(o_ref, r0, CHUNK, outs)


def _attn_prompt(zz, batch, seq, table, sinks):
    qb = WINDOW
    nb = seq // qb
    n_k = WINDOW + CHUNK
    bucket = jnp.asarray(_t5_buckets(CHUNK, n_k))
    smem = pl.BlockSpec(memory_space=pltpu.SMEM)
    kblk, vblk = ZC_K // KV_DIM, ZC_V // KV_DIM
    return pl.pallas_call(
        _attn_prompt_kernel,
        grid=(batch, nb),
        in_specs=[
            smem, smem,
            pl.BlockSpec((CHUNK, n_k), lambda b, j: (0, 0)),
            pl.BlockSpec((qb, ATTN_DIM), lambda b, j: (b * nb + j, ZC_Q // ATTN_DIM)),
            pl.BlockSpec((qb, KV_DIM), lambda b, j: (b * nb + jnp.maximum(j - 1, 0), kblk)),
            pl.BlockSpec((qb, KV_DIM), lambda b, j: (b * nb + j, kblk)),
            pl.BlockSpec((qb, KV_DIM), lambda b, j: (b * nb + jnp.maximum(j - 1, 0), vblk)),
            pl.BlockSpec((qb, KV_DIM), lambda b, j: (b * nb + j, vblk)),
        ],
        out_specs=pl.BlockSpec((qb, ATTN_DIM), lambda b, j: (b * nb + j, 0)),
        out_shape=jax.ShapeDtypeStruct((batch * seq, ATTN_DIM), BF16),
        scratch_shapes=[pltpu.VMEM((N_KV_HEADS, GROUP * CHUNK, n_k), F32)],
        compiler_params=pltpu.CompilerParams(dimension_semantics=("arbitrary", "arbitrary")),
        name="attn_prompt",
    )(table, sinks, bucket, zz, zz, zz, zz, zz)


def _attn_sample_kernel(table_ref, sinks_ref, bucket_ref, q_ref, k_ref, v_ref, o_ref, bias_ref, *, n_q):
    @pl.when(pl.program_id(0) == 0)
    def _():
        _build_bias(bias_ref, bucket_ref, table_ref, n_q)

    outs = _attend(q_ref[...], k_ref[0], v_ref[0], bias_ref, sinks_ref, n_q, None)
    _store_heads(o_ref, 0, n_q, outs)


def _attn_sample(zz, k_ext, v_ext, batch, n_q, table, sinks):
    n_k = WINDOW + n_q
    bucket = jnp.asarray(_t5_buckets(n_q, n_k))
    smem = pl.BlockSpec(memory_space=pltpu.SMEM)
    return pl.pallas_call(
        functools.partial(_attn_sample_kernel, n_q=n_q),
        grid=(batch,),
        in_specs=[
            smem, smem,
            pl.BlockSpec((n_q, n_k), lambda b: (0, 0)),
            pl.BlockSpec((n_q, ATTN_DIM), lambda b: (b, ZC_Q // ATTN_DIM)),
            pl.BlockSpec((1, n_k, KV_DIM), lambda b: (b, 0, 0)),
            pl.BlockSpec((1, n_k, KV_DIM), lambda b: (b, 0, 0)),
        ],
        out_specs=pl.BlockSpec((n_q, ATTN_DIM), lambda b: (b, 0)),
        out_shape=jax.ShapeDtypeStruct((batch * n_q, ATTN_DIM), BF16),
        scratch_shapes=[pltpu.VMEM((N_KV_HEADS, GROUP * n_q, n_k), F32)],
        compiler_params=pltpu.CompilerParams(dimension_semantics=("arbitrary",)),
        name="attn_sample",
    )(table, sinks, bucket, zz, k_ext, v_ext)


def _merge_kernel(o_ref, ga_ref, gc_ref, cb_ref, u_ref, h0_ref, h1_ref, x_ref, cw_ref, woa_ref, woc_ref, wout_ref,
                  gf_ref, wr_ref, br_ref, h_ref, hn_ref, idx_ref, wts_ref, *, tm, seq_rows, tiles_per_seq):
    u = u_ref[...]
    row = lax.broadcasted_iota(I32, (tm, 1), 0)
    if seq_rows is None:
        at_start = (pl.program_id(0) % tiles_per_seq) == 0
        hist0 = jnp.where(at_start, 0.0, h0_ref[6:7, :])
        hist1 = jnp.where(at_start, 0.0, h1_ref[7:8, :])
        pos = row
    else:
        hist0 = h0_ref[...]
        hist1 = h1_ref[...]
        pos = row % seq_rows
    r1 = pltpu.roll(u, 1, 0)
    r2 = pltpu.roll(u, 2, 0)
    u1 = jnp.where(pos == 0, hist1, r1)
    u2 = jnp.where(pos == 0, hist0, jnp.where(pos == 1, hist1, r2))
    cw = cw_ref[...]
    y = cw[0:1, :] * u2
    y = y + cw[1:2, :] * u1
    y = y + cw[2:3, :] * u
    c = (cb_ref[...] * y).astype(BF16)
    t = ga_ref[...] * jnp.dot(o_ref[...], woa_ref[...], preferred_element_type=F32)
    t = t + gc_ref[...] * jnp.dot(c, woc_ref[...], preferred_element_type=F32)
    h = x_ref[...] + jnp.dot(t.astype(BF16), wout_ref[...], preferred_element_type=F32)
    h_ref[...] = h
    ms = jnp.mean(h * h, axis=-1, keepdims=True)
    hn = (h * lax.rsqrt(ms + NORM_EPS)) * gf_ref[...]
    hn_ref[...] = hn
    logits = lax.dot_general(wr_ref[...], hn.astype(BF16), (((1,), (1,)), ((), ())),
                             preferred_element_type=F32) + br_ref[...]
    erow = lax.broadcasted_iota(I32, logits.shape, 0).astype(F32)
    vals, idxs = [], []
    for _ in range(TOP_K):
        mx = jnp.max(logits, axis=0, keepdims=True)
        ix = jnp.min(jnp.where(logits == mx, erow, float(N_EXPERTS)), axis=0, keepdims=True)
        vals.append(mx)
        idxs.append(ix)
        logits = jnp.where(erow == ix, -jnp.inf, logits)
    es = [jnp.exp(v - vals[0]) for v in vals]
    tot = es[0]
    for e in es[1:]:
        tot = tot + e
    idx_ref[...] = jnp.concatenate(idxs, axis=0).astype(I32)
    wts_ref[...] = jnp.concatenate([e / tot for e in es], axis=0)


def _merge(o, zz, x, hist0, hist1, conv_w, woa, woc, wout, g_ffn, wr_t, br, tm, seq_rows, seq_len):
    t = x.shape[0]
    d = D_MODEL
    full = lambda shape: pl.BlockSpec(shape, lambda i: (0,) * len(shape))
    if seq_rows is None:
        tiles_per_seq = seq_len // tm
        hist_spec = pl.BlockSpec((8, CONV_DIM), lambda i: (jnp.maximum(i * (tm // 8) - 1, 0), ZC_U // CONV_DIM))
        h0_spec = h1_spec = hist_spec
        h0_arg = h1_arg = zz
    else:
        tiles_per_seq = None
        h0_spec = h1_spec = pl.BlockSpec((tm, CONV_DIM), lambda i: (i, 0))
        h0_arg, h1_arg = hist0, hist1
    kern = functools.partial(_merge_kernel, tm=tm, seq_rows=seq_rows, tiles_per_seq=tiles_per_seq)
    vmem = (2 * tm * (ATTN_DIM * 2 + 5 * d * 4 + 4 * CONV_DIM * 4) + 2 * 2 * (2 * ATTN_DIM * d + d * d)
            + 10 * tm * d * 4 + (8 << 20))
    return pl.pallas_call(
        kern,
        grid=(t // tm,),
        in_specs=[
            pl.BlockSpec((tm, ATTN_DIM), lambda i: (i, 0)),
            pl.BlockSpec((tm, d), lambda i: (i, ZC_GA // d)),
            pl.BlockSpec((tm, d), lambda i: (i, ZC_GC // d)),
            pl.BlockSpec((tm, CONV_DIM), lambda i: (i, ZC_CB // CONV_DIM)),
            pl.BlockSpec((tm, CONV_DIM), lambda i: (i, ZC_U // CONV_DIM)),
            h0_spec, h1_spec,
            pl.BlockSpec((tm, d), lambda i: (i, 0)),
            full((3, CONV_DIM)),
            full((ATTN_DIM, d)), full((CONV_DIM, d)), full((d, d)),
            full((1, d)), full((N_EXPERTS, d)), full((N_EXPERTS, 1)),
        ],
        out_specs=[
            pl.BlockSpec((tm, d), lambda i: (i, 0)),
            pl.BlockSpec((tm, d), lambda i: (i, 0)),
            pl.BlockSpec((TOP_K, tm), lambda i: (0, i)),
            pl.BlockSpec((TOP_K, tm), lambda i: (0, i)),
        ],
        out_shape=[
            jax.ShapeDtypeStruct((t, d), F32),
            jax.ShapeDtypeStruct((t, d), F32),
            jax.ShapeDtypeStruct((TOP_K, t), I32),
            jax.ShapeDtypeStruct((TOP_K, t), F32),
        ],
        compiler_params=pltpu.CompilerParams(dimension_semantics=("arbitrary",), vmem_limit_bytes=_vmem_limit(vmem)),
        name="merge_prompt" if seq_rows is None else "merge_sample",
    )(o, zz, zz, zz, zz, h0_arg, h1_arg, x, conv_w, woa, woc, wout, g_ffn.reshape(1, d), wr_t,
      br.reshape(N_EXPERTS, 1))


ROUTE_TILE = 384


def _route_kernel(idx_ref, dest_ref, cnt_ref, cnt_acc, carry, pst, tri_ref):
    p = pl.program_id(0)
    j = pl.program_id(1)
    tr = idx_ref.shape[1]
    idx = idx_ref[...]
    erow = lax.broadcasted_iota(I32, (N_EXPERTS, tr), 0)
    masks = [(erow == idx[k:k + 1, :]).astype(F32) for k in range(TOP_K)]
    mtot = masks[0]
    for m in masks[1:]:
        mtot = mtot + m
    tile_cnt = jnp.sum(mtot, axis=1, keepdims=True)

    @pl.when((p == 0) & (j == 0))
    def _():
        cnt_acc[...] = jnp.zeros_like(cnt_acc)

    @pl.when(p == 0)
    def _():
        cnt_acc[...] += jnp.broadcast_to(tile_cnt, cnt_acc.shape)

    @pl.when((p == 1) & (j == 0))
    def _():
        cnt = cnt_acc[:, 0:1]
        padded = jnp.floor((cnt + float(ROW_BLK - 1)) / float(ROW_BLK)) * float(ROW_BLK)
        r = lax.broadcasted_iota(I32, (N_EXPERTS, N_EXPERTS), 0)
        c = lax.broadcasted_iota(I32, (N_EXPERTS, N_EXPERTS), 1)
        rowv = jnp.sum(jnp.where(r == c, padded, 0.0), axis=0, keepdims=True)
        start = jnp.sum(jnp.where(c < r, rowv, 0.0), axis=1, keepdims=True)
        pst[...] = jnp.broadcast_to(start, pst.shape)
        carry[...] = jnp.zeros_like(carry)
        a = lax.broadcasted_iota(I32, (tr, tr), 0)
        b = lax.broadcasted_iota(I32, (tr, tr), 1)
        tri_ref[...] = (a < b).astype(BF16)

    @pl.when(p == 1)
    def _():
        excl = jnp.dot(mtot.astype(BF16), tri_ref[...], preferred_element_type=F32)
        val = pst[:, 0:1] + carry[:, 0:1] + excl
        dest = jnp.concatenate([jnp.sum(m * val, axis=0, keepdims=True) for m in masks], axis=0)
        dest_ref[...] = dest.astype(I32)
        carry[...] += jnp.broadcast_to(tile_cnt, carry.shape)
        cnt_ref[...] = cnt_acc[...].astype(I32)


def _route(idx_t):
    t = idx_t.shape[1]
    tr = ROUTE_TILE
    return pl.pallas_call(
        _route_kernel,
        grid=(2, t // tr),
        in_specs=[pl.BlockSpec((TOP_K, tr), lambda p, j: (0, j))],
        out_specs=[
            pl.BlockSpec((TOP_K, tr), lambda p, j: (0, j * p)),
            pl.BlockSpec((N_EXPERTS, LANES), lambda p, j: (0, 0)),
        ],
        out_shape=[jax.ShapeDtypeStruct((TOP_K, t), I32), jax.ShapeDtypeStruct((N_EXPERTS, LANES), I32)],
        scratch_shapes=[pltpu.VMEM((N_EXPERTS, LANES), F32), pltpu.VMEM((N_EXPERTS, LANES), F32),
                        pltpu.VMEM((N_EXPERTS, LANES), F32), pltpu.VMEM((tr, tr), BF16)],
        compiler_params=pltpu.CompilerParams(dimension_semantics=("arbitrary", "arbitrary")),
        name="route",
    )(idx_t)


DISPATCH_TILE = 128


def _dispatch_kernel(dest_ref, hn_hbm, xs_hbm, sem):
    tt = dest_ref.shape[1]
    t0 = pl.program_id(0) * tt

    def issue(t, c):
        for k in range(TOP_K):
            pltpu.make_async_copy(hn_hbm.at[pl.ds(t0 + t, 1)], xs_hbm.at[pl.ds(dest_ref[k, t], 1)], sem).start()
        return c

    lax.fori_loop(0, tt, issue, 0)

    def drain(t, c):
        for k in range(TOP_K):
            pltpu.make_async_copy(hn_hbm.at[pl.ds(0, 1)], xs_hbm.at[pl.ds(0, 1)], sem).wait()
        return c

    lax.fori_loop(0, tt, drain, 0)


def _dispatch(dest, hn, n_slots):
    t, d = hn.shape
    tt = DISPATCH_TILE
    return pl.pallas_call(
        _dispatch_kernel,
        grid=(t // tt,),
        in_specs=[pl.BlockSpec((TOP_K, tt), lambda i: (0, i), memory_space=pltpu.SMEM),
                  pl.BlockSpec(memory_space=pl.ANY)],
        out_specs=pl.BlockSpec(memory_space=pl.ANY),
        out_shape=jax.ShapeDtypeStruct((n_slots, d), hn.dtype),
        scratch_shapes=[pltpu.SemaphoreType.DMA(())],
        compiler_params=pltpu.CompilerParams(dimension_semantics=("arbitrary",)),
        name="dispatch",
    )(dest, hn)


def _expert_kernel(we_ref, ws_ref, wnb_ref, wvalid_ref, wact_ref,
                   xs_hbm, wg_ref, wu_ref, wd_ref, bg_ref, bu_ref, bd_ref, ys_hbm,
                   x_sb, acc, wg_bf, wu_bf, wd_bf, stage, sem_in, sem_out):
    del we_ref, wact_ref
    w = pl.program_id(0)
    f = pl.program_id(1)
    nb = wnb_ref[w]
    start = ws_ref[w]
    valid = wvalid_ref[w]

    @pl.when((f == 0) & (nb > 0))
    def _load():
        row = lax.broadcasted_iota(I32, (ROW_BLK, 1), 0)

        def body(j, c):
            r0 = pl.multiple_of(j * ROW_BLK, ROW_BLK)
            cp = pltpu.make_async_copy(xs_hbm.at[pl.ds(pl.multiple_of(start + r0, ROW_BLK), ROW_BLK)], stage, sem_in)
            cp.start()
            cp.wait()
            x_sb[pl.ds(r0, ROW_BLK), :] = jnp.where(row + r0 < valid, stage[...], 0.0).astype(BF16)
            return c

        lax.fori_loop(0, nb, body, 0)

    @pl.when(nb > 0)
    def _compute():
        wg_bf[...] = wg_ref[0].astype(BF16)
        wu_bf[...] = wu_ref[0].astype(BF16)
        wd_bf[...] = wd_ref[0].astype(BF16)
        bg = bg_ref[0]
        bu = bu_ref[0]
        bd = bd_ref[0]

        def contribution(r0):
            x = x_sb[pl.ds(r0, ROW_BLK), :]
            g = jnp.dot(x, wg_bf[...], preferred_element_type=F32) + bg
            u = jnp.dot(x, wu_bf[...], preferred_element_type=F32) + bu
            g = jnp.minimum(g, SWIGLU_LIMIT)
            u = jnp.clip(u, -SWIGLU_LIMIT, SWIGLU_LIMIT)
            hh = (u + 1.0) * (g * jax.nn.sigmoid(SWIGLU_ALPHA * g))
            return jnp.dot(hh.astype(BF16), wd_bf[...], preferred_element_type=F32)

        @pl.when(f == 0)
        def _():
            def body(m, c):
                r0 = pl.multiple_of(m * ROW_BLK, ROW_BLK)
                acc[pl.ds(r0, ROW_BLK), :] = contribution(r0) + bd
                return c

            lax.fori_loop(0, nb, body, 0)

        @pl.when(f > 0)
        def _():
            def body(m, c):
                r0 = pl.multiple_of(m * ROW_BLK, ROW_BLK)
                acc[pl.ds(r0, ROW_BLK), :] += contribution(r0)
                return c

            lax.fori_loop(0, nb, body, 0)

    @pl.when((f == N_FF_TILES - 1) & (nb > 0))
    def _store():
        def copy(j):
            r0 = pl.multiple_of(j * ROW_BLK, ROW_BLK)
            return pltpu.make_async_copy(acc.at[pl.ds(r0, ROW_BLK)],
                                         ys_hbm.at[pl.ds(pl.multiple_of(start + r0, ROW_BLK), ROW_BLK)], sem_out)

        def issue(j, c):
            copy(j).start()
            return c

        def drain(j, c):
            copy(j).wait()
            return c

        lax.fori_loop(0, nb, issue, 0)
        lax.fori_loop(0, nb, drain, 0)


def _experts(xs, tables, w_gate_up, b_gate_up, w_down, b_down):
    n_slots, d = xs.shape
    n_items = tables[0].shape[0]
    tf = FF_TILE
    nf = N_FF_TILES

    def ff(f, act, w):
        return f * act[w] + (nf - 1) * (1 - act[w])

    grid_spec = pltpu.PrefetchScalarGridSpec(
        num_scalar_prefetch=5,
        grid=(n_items, nf),
        in_specs=[
            pl.BlockSpec(memory_space=pl.ANY),
            pl.BlockSpec((1, d, tf), lambda w, f, we, ws, wnb, wv, act: (we[w], 0, ff(f, act, w))),
            pl.BlockSpec((1, d, tf), lambda w, f, we, ws, wnb, wv, act: (we[w], 0, nf + ff(f, act, w))),
            pl.BlockSpec((1, tf, d), lambda w, f, we, ws, wnb, wv, act: (we[w], ff(f, act, w), 0)),
            pl.BlockSpec((1, 1, tf), lambda w, f, we, ws, wnb, wv, act: (we[w], 0, ff(f, act, w))),
            pl.BlockSpec((1, 1, tf), lambda w, f, we, ws, wnb, wv, act: (we[w], 0, nf + ff(f, act, w))),
            pl.BlockSpec((1, 1, d), lambda w, f, we, ws, wnb, wv, act: (we[w], 0, 0)),
        ],
        out_specs=pl.BlockSpec(memory_space=pl.ANY),
        scratch_shapes=[
            pltpu.VMEM((SUPER_ROWS, d), BF16),
            pltpu.VMEM((SUPER_ROWS, d), F32),
            pltpu.VMEM((d, tf), BF16), pltpu.VMEM((d, tf), BF16), pltpu.VMEM((tf, d), BF16),
            pltpu.VMEM((ROW_BLK, d), F32),
            pltpu.SemaphoreType.DMA(()), pltpu.SemaphoreType.DMA(()),
        ],
    )
    vmem = (SUPER_ROWS * d * 6 + 2 * 3 * d * tf * 4 + 3 * d * tf * 2 + ROW_BLK * d * 4
            + 6 * ROW_BLK * d * 4 + (6 << 20))
    return pl.pallas_call(
        _expert_kernel,
        grid_spec=grid_spec,
        out_shape=jax.ShapeDtypeStruct((n_slots, d), F32),
        compiler_params=pltpu.CompilerParams(
            dimension_semantics=("arbitrary", "arbitrary"), vmem_limit_bytes=_vmem_limit(vmem)),
        name="experts",
    )(*tables, xs, w_gate_up, w_gate_up, w_down, b_gate_up.reshape(N_EXPERTS, 1, 2 * D_FF),
      b_gate_up.reshape(N_EXPERTS, 1, 2 * D_FF), b_down.reshape(N_EXPERTS, 1, d))


def _work_tables(counts, n_items):
    padded = (counts + ROW_BLK - 1) // ROW_BLK * ROW_BLK
    pstart = jnp.cumsum(padded) - padded
    items_e = (padded + SUPER_ROWS - 1) // SUPER_ROWS
    item_end = jnp.cumsum(items_e)
    item_start = item_end - items_e
    w = jnp.arange(n_items, dtype=I32)
    total = item_end[-1]
    active = (w < total).astype(I32)
    wl = jnp.minimum(w, total - 1)
    e = jnp.minimum(jnp.searchsorted(item_end, wl, side='right'), N_EXPERTS - 1).astype(I32)
    local = wl - item_start[e]
    start = pstart[e] + local * SUPER_ROWS
    rows = jnp.minimum(SUPER_ROWS, padded[e] - local * SUPER_ROWS)
    valid = jnp.clip(counts[e] - local * SUPER_ROWS, 0, rows)
    nb = (rows // ROW_BLK) * active
    return (e.astype(I32), start.astype(I32), nb.astype(I32), valid.astype(I32), active)


COMBINE_TILE = 128


def _combine_kernel(dest_ref, wts_ref, h_ref, g_ref, ys_hbm, out_ref, buf, sem):
    tt = dest_ref.shape[1]

    def issue(t, c):
        for k in range(TOP_K):
            pltpu.make_async_copy(ys_hbm.at[pl.ds(dest_ref[k, t], 1)], buf.at[k, pl.ds(t, 1)], sem).start()
        return c

    lax.fori_loop(0, tt, issue, 0)

    def drain(t, c):
        for k in range(TOP_K):
            pltpu.make_async_copy(ys_hbm.at[pl.ds(0, 1)], buf.at[k, pl.ds(0, 1)], sem).wait()
        return c

    lax.fori_loop(0, tt, drain, 0)
    wts = wts_ref[...]
    moe = wts[:, 0:1] * buf[0]
    for k in range(1, TOP_K):
        moe = moe + wts[:, k:k + 1] * buf[k]
    y = h_ref[...] + moe
    ms = jnp.mean(y * y, axis=-1, keepdims=True)
    out_ref[...] = (y * lax.rsqrt(ms + NORM_EPS)) * g_ref[...]


def _combine(dest, wts, h, g_final, ys):
    t, d = h.shape
    tt = COMBINE_TILE
    return pl.pallas_call(
        _combine_kernel,
        grid=(t // tt,),
        in_specs=[
            pl.BlockSpec((TOP_K, tt), lambda i: (0, i), memory_space=pltpu.SMEM),
            pl.BlockSpec((tt, TOP_K), lambda i: (i, 0)),
            pl.BlockSpec((tt, d), lambda i: (i, 0)),
            pl.BlockSpec((1, d), lambda i: (0, 0)),
            pl.BlockSpec(memory_space=pl.ANY),
        ],
        out_specs=pl.BlockSpec((tt, d), lambda i: (i, 0)),
        out_shape=jax.ShapeDtypeStruct((t, d), F32),
        scratch_shapes=[pltpu.VMEM((TOP_K, tt, d), F32), pltpu.SemaphoreType.DMA(())],
        compiler_params=pltpu.CompilerParams(dimension_semantics=("arbitrary",)),
        name="combine",
    )(dest, wts, h, g_final.reshape(1, d), ys)


def kernel(x_prompt, x_sample, cache_attn_k, cache_attn_v, state_conv, rel_bias_table, norm_mix_g, w_in, b_gate,
           attn_sinks, conv_w, w_o_attn, w_o_conv, w_out, norm_ffn_g, w_router, b_router, w_gate_up, b_gate_up,
           w_down, b_down, norm_final_g):
    batch, seq, d = x_prompt.shape
    dbatch, dseq, _ = x_sample.shape
    assert norm_mix_g.shape[0] == 1 and d == D_MODEL
    t_p, t_s = batch * seq, dbatch * dseq
    xp = x_prompt.reshape(t_p, d)
    xs = x_sample.reshape(t_s, d)

    w_in_bf = w_in[0].astype(BF16)
    woa = w_o_attn[0].astype(BF16)
    woc = w_o_conv[0].astype(BF16)
    wout = w_out[0].astype(BF16)
    wr_t = w_router[0].T.astype(BF16)

    zz_p = _in_proj(xp, norm_mix_g[0], w_in_bf, b_gate[0], tm=1024)
    zz_s = _in_proj(xs, norm_mix_g[0], w_in_bf, b_gate[0], tm=t_s)

    o_p = _attn_prompt(zz_p, batch, seq, rel_bias_table, attn_sinks[0])
    k_new = zz_s[:, ZC_K:ZC_K + KV_DIM].reshape(dbatch, dseq, KV_DIM)
    v_new = zz_s[:, ZC_V:ZC_V + KV_DIM].reshape(dbatch, dseq, KV_DIM)
    k_ext = jnp.concatenate([cache_attn_k[0].reshape(dbatch, WINDOW, KV_DIM), k_new], axis=1)
    v_ext = jnp.concatenate([cache_attn_v[0].reshape(dbatch, WINDOW, KV_DIM), v_new], axis=1)
    o_s = _attn_sample(zz_s, k_ext, v_ext, dbatch, dseq, rel_bias_table, attn_sinks[0])

    hist0 = jnp.repeat(state_conv[0][:, 0, :], dseq, axis=0)
    hist1 = jnp.repeat(state_conv[0][:, 1, :], dseq, axis=0)
    merge_args = (conv_w[0], woa, woc, wout, norm_ffn_g[0], wr_t, b_router[0])
    h_p, hn_p, idx_p, wts_p = _merge(o_p, zz_p, xp, None, None, *merge_args, tm=256, seq_rows=None, seq_len=seq)
    h_s, hn_s, idx_s, wts_s = _merge(o_s, zz_s, xs, hist0, hist1, *merge_args, tm=t_s, seq_rows=dseq, seq_len=dseq)

    hn = jnp.concatenate([hn_p, hn_s], axis=0)
    idx_t = jnp.concatenate([idx_p, idx_s], axis=1)
    wts = jnp.concatenate([wts_p, wts_s], axis=1).T
    t_all = t_p + t_s
    n_assign = t_all * TOP_K
    n_slots = (n_assign + N_EXPERTS * (ROW_BLK - 1) + ROW_BLK - 1) // ROW_BLK * ROW_BLK
    n_items = N_EXPERTS + n_slots // SUPER_ROWS

    dest, cnt = _route(idx_t)
    tables = _work_tables(cnt[:, 0], n_items)
    xs_sorted = _dispatch(dest, hn, n_slots)
    ys = _experts(xs_sorted, tables, w_gate_up[0], b_gate_up[0], w_down[0], b_down[0])
    y_p = _combine(dest[:, :t_p], wts[:t_p], h_p, norm_final_g, ys)
    y_s = _combine(dest[:, t_p:], wts[t_p:], h_s, norm_final_g, ys)

    def kv_tail(z, c0, b, s):
        return z[:, c0:c0 + KV_DIM].reshape(b, s, N_KV_HEADS, HEAD_DIM)[:, -WINDOW:][None]

    u_p = zz_p[:, ZC_U:ZC_U + CONV_DIM].reshape(batch, seq, CONV_DIM)
    u_s = zz_s[:, ZC_U:ZC_U + CONV_DIM].reshape(dbatch, dseq, CONV_DIM)
    conv_hist = state_conv.shape[2]
    new_conv_p = u_p[:, -conv_hist:][None]
    new_conv_s = jnp.concatenate([state_conv[0], u_s], axis=1)[:, -conv_hist:][None]
    new_k_s = k_ext[:, -WINDOW:].reshape(dbatch, WINDOW, N_KV_HEADS, HEAD_DIM)[None]
    new_v_s = v_ext[:, -WINDOW:].reshape(dbatch, WINDOW, N_KV_HEADS, HEAD_DIM)[None]
    return (y_p.reshape(batch, seq, d), y_s.reshape(dbatch, dseq, d),
            kv_tail(zz_p, ZC_K, batch, seq), kv_tail(zz_p, ZC_V, batch, seq), new_conv_p,
            new_k_s, new_v_s, new_conv_s)
```

```python
import functools
import math

import numpy as np
import jax
import jax.numpy as jnp
from jax import lax
from jax.experimental import pallas as pl
from jax.experimental.pallas import tpu as pltpu

F32 = jnp.float32
BF16 = jnp.bfloat16
I32 = jnp.int32

D_MODEL = 2048
CHUNK = 64
HEAD_DIM = 64
N_Q_HEADS = 16
N_KV_HEADS = 4
GROUP = N_Q_HEADS // N_KV_HEADS
ATTN_DIM = N_Q_HEADS * HEAD_DIM
KV_DIM = N_KV_HEADS * HEAD_DIM
WINDOW = 128
CONV_DIM = D_MODEL // 2
N_BUCKETS = 32
MAX_DISTANCE = 128
N_EXPERTS = 32
TOP_K = 4
D_FF = D_MODEL
SWIGLU_LIMIT = 7.0
SWIGLU_ALPHA = 1.702
NORM_EPS = 1e-5
IN_DIM = ATTN_DIM + 2 * KV_DIM + 3 * CONV_DIM + 2 * D_MODEL

V7X_VMEM_BYTES = 64 * 1024 * 1024
LANES = 128

ZC_GA = 0
ZC_GC = D_MODEL
ZC_CB = 2 * D_MODEL
ZC_U = ZC_CB + CONV_DIM
ZC_Q = ZC_U + CONV_DIM
ZC_K = ZC_Q + ATTN_DIM
ZC_V = ZC_K + KV_DIM
Z_DIM = ZC_V + KV_DIM
PROJ_TN = 512

ROW_BLK = 256
SUPER_BLKS = 5
SUPER_ROWS = SUPER_BLKS * ROW_BLK
FF_TILE = 256
N_FF_TILES = D_FF // FF_TILE
assert SUPER_BLKS <= N_FF_TILES


def _vmem_limit(nbytes):
    return int(min(nbytes, V7X_VMEM_BYTES - 6 * 1024 * 1024))


_MODE_PLAIN, _MODE_STASH0, _MODE_STASH1, _MODE_MUL0, _MODE_MUL1, _MODE_GATE = range(6)


def _proj_schedule():
    t = PROJ_TN
    cc0 = (ATTN_DIM + 2 * KV_DIM + CONV_DIM) // t
    ch0 = (ATTN_DIM + 2 * KV_DIM + 2 * CONV_DIM) // t
    cb0 = (ATTN_DIM + 2 * KV_DIM) // t
    ga0 = (ATTN_DIM + 2 * KV_DIM + 3 * CONV_DIM) // t
    steps = [
        (cc0, ZC_U // t, 0, _MODE_STASH0),
        (cc0 + 1, ZC_U // t, 0, _MODE_STASH1),
        (ch0, ZC_U // t, 0, _MODE_MUL0),
        (ch0 + 1, ZC_U // t + 1, 0, _MODE_MUL1),
        (0, ZC_Q // t, 0, _MODE_PLAIN),
        (1, ZC_Q // t + 1, 0, _MODE_PLAIN),
        (2, ZC_K // t, 0, _MODE_PLAIN),
        (cb0, ZC_CB // t, 0, _MODE_PLAIN),
        (cb0 + 1, ZC_CB // t + 1, 0, _MODE_PLAIN),
    ]
    for j in range(2 * D_MODEL // t):
        steps.append((ga0 + j, j, j, _MODE_GATE))
    return np.asarray(steps, dtype=np.int32)


def _in_proj_kernel(wt_ref, ot_ref, bt_ref, md_ref, x_ref, g_ref, w_ref, b_ref, o_ref, xn_ref, stash_ref):
    del wt_ref, ot_ref, bt_ref
    n = pl.program_id(1)

    @pl.when(n == 0)
    def _():
        x = x_ref[...]
        ms = jnp.mean(x * x, axis=-1, keepdims=True)
        xn_ref[...] = ((x * lax.rsqrt(ms + NORM_EPS)) * g_ref[...]).astype(BF16)

    acc = jnp.dot(xn_ref[...], w_ref[...], preferred_element_type=F32)
    md = md_ref[n]

    @pl.when(md == _MODE_PLAIN)
    def _():
        o_ref[...] = acc

    @pl.when(md == _MODE_STASH0)
    def _():
        stash_ref[0] = acc

    @pl.when(md == _MODE_STASH1)
    def _():
        stash_ref[1] = acc

    @pl.when(md == _MODE_MUL0)
    def _():
        o_ref[...] = stash_ref[0] * acc

    @pl.when(md == _MODE_MUL1)
    def _():
        o_ref[...] = stash_ref[1] * acc

    @pl.when(md == _MODE_GATE)
    def _():
        o_ref[...] = jax.nn.sigmoid(acc + b_ref[...])


def _in_proj(x, g_mix, w_in_bf, b_gate, tm):
    t, d = x.shape
    sched = _proj_schedule()
    n_steps = sched.shape[0]
    grid_spec = pltpu.PrefetchScalarGridSpec(
        num_scalar_prefetch=4,
        grid=(t // tm, n_steps),
        in_specs=[
            pl.BlockSpec((tm, d), lambda m, n, wt, ot, bt, md: (m, 0)),
            pl.BlockSpec((1, d), lambda m, n, wt, ot, bt, md: (0, 0)),
            pl.BlockSpec((d, PROJ_TN), lambda m, n, wt, ot, bt, md: (0, wt[n])),
            pl.BlockSpec((1, PROJ_TN), lambda m, n, wt, ot, bt, md: (0, bt[n])),
        ],
        out_specs=pl.BlockSpec((tm, PROJ_TN), lambda m, n, wt, ot, bt, md: (m, ot[n])),
        scratch_shapes=[pltpu.VMEM((tm, d), BF16), pltpu.VMEM((2, tm, PROJ_TN), F32)],
    )
    vmem = 2 * tm * d * 4 + tm * d * 2 + 2 * d * PROJ_TN * 2 + 6 * tm * PROJ_TN * 4 + (8 << 20)
    return pl.pallas_call(
        _in_proj_kernel,
        grid_spec=grid_spec,
        out_shape=jax.ShapeDtypeStruct((t, Z_DIM), F32),
        compiler_params=pltpu.CompilerParams(
            dimension_semantics=("arbitrary", "arbitrary"), vmem_limit_bytes=_vmem_limit(vmem)),
        name="in_proj",
    )(jnp.asarray(sched[:, 0]), jnp.asarray(sched[:, 1]), jnp.asarray(sched[:, 2]), jnp.asarray(sched[:, 3]),
      x, g_mix.reshape(1, d), w_in_bf, b_gate.reshape(1, 2 * D_MODEL))


def _t5_buckets(n_q, n_k):
    half = N_BUCKETS // 2
    max_exact = half // 2
    rel = (np.arange(n_k, dtype=np.int32)[None, :] - WINDOW) - np.arange(n_q, dtype=np.int32)[:, None]
    ret = np.where(rel > 0, half, 0)
    n = np.abs(rel)
    nf = np.maximum(n, 1).astype(np.float32)
    large = max_exact + (np.log(nf / np.float32(max_exact)) / np.float32(math.log(MAX_DISTANCE / max_exact))
                         * np.float32(half - max_exact)).astype(np.int32)
    large = np.minimum(large, half - 1)
    return (ret + np.where(n < max_exact, n, large)).astype(np.int32)


def _build_bias(bias_ref, bucket_ref, table_ref, n_q):
    bucket = bucket_ref[...]
    for kvh in range(N_KV_HEADS):
        for g in range(GROUP):
            h = kvh * GROUP + g
            mat = jnp.zeros(bucket.shape, F32)
            for b in range(N_BUCKETS):
                mat = jnp.where(bucket == b, table_ref[b, h], mat)
            bias_ref[kvh, g * n_q:(g + 1) * n_q, :] = mat


def _attend(q, kwin, vwin, bias_ref, sinks_ref, n_q, invalid):
    outs = []
    grp = lax.broadcasted_iota(I32, (GROUP * n_q, 1), 0) // n_q
    for kvh in range(N_KV_HEADS):
        c0 = kvh * GROUP * HEAD_DIM
        qh = jnp.concatenate([q[:, c0 + g * HEAD_DIM:c0 + (g + 1) * HEAD_DIM] for g in range(GROUP)], axis=0)
        qh = (qh * (HEAD_DIM ** -0.5)).astype(BF16)
        kh = kwin[:, kvh * HEAD_DIM:(kvh + 1) * HEAD_DIM].astype(BF16)
        vh = vwin[:, kvh * HEAD_DIM:(kvh + 1) * HEAD_DIM].astype(BF16)
        s = lax.dot_general(qh, kh, (((1,), (1,)), ((), ())), preferred_element_type=F32) + bias_ref[kvh]
        if invalid is not None:
            s = jnp.where(invalid, -1e30, s)
        sink = jnp.zeros((GROUP * n_q, 1), F32)
        for g in range(GROUP):
            sink = jnp.where(grp == g, sinks_ref[kvh * GROUP + g], sink)
        m = jnp.maximum(jnp.max(s, axis=-1, keepdims=True), sink)
        p = jnp.exp(s - m)
        denom = jnp.sum(p, axis=-1, keepdims=True) + jnp.exp(sink - m)
        o = jnp.dot(p.astype(BF16), vh, preferred_element_type=F32) / denom
        outs.append(o)
    return outs


def _store_heads(o_ref, r0, n_q, outs):
    for kvh, o in enumerate(outs):
        for g in range(GROUP):
            c = (kvh * GROUP + g) * HEAD_DIM
            o_ref[r0:r0 + n_q, c:c + HEAD_DIM] = o[g * n_q:(g + 1) * n_q].astype(o_ref.dtype)


def _attn_prompt_kernel(table_ref, sinks_ref, bucket_ref, q_ref, kp_ref, kc_ref, vp_ref, vc_ref, o_ref, bias_ref):
    first = (pl.program_id(0) == 0) & (pl.program_id(1) == 0)

    @pl.when(first)
    def _():
        _build_bias(bias_ref, bucket_ref, table_ref, CHUNK)

    kwin = jnp.concatenate([kp_ref[...], kc_ref[...]], axis=0)
    vwin = jnp.concatenate([vp_ref[...], vc_ref[...]], axis=0)
    n_k = WINDOW + CHUNK
    seq_start = pl.program_id(1) == 0
    col = lax.broadcasted_iota(I32, (1, n_k), 1)
    for c in range(WINDOW // CHUNK):
        r0 = c * CHUNK
        invalid = seq_start & (col + r0 < WINDOW)
        outs = _attend(q_ref[r0:r0 + CHUNK, :], kwin[r0:r0 + n_k], vwin[r0:r0 + n_k], bias_ref, sinks_ref,
                       CHUNK, invalid)
        _store_heads(o_ref, r0, CHUNK, outs)


def _attn_prompt(zz, batch, seq, table, sinks):
    qb = WINDOW
    nb = seq // qb
    n_k = WINDOW + CHUNK
    bucket = jnp.asarray(_t5_buckets(CHUNK, n_k))
    smem = pl.BlockSpec(memory_space=pltpu.SMEM)
    kblk, vblk = ZC_K // KV_DIM, ZC_V // KV_DIM
    return pl.pallas_call(
        _attn_prompt_kernel,
        grid=(batch, nb),
        in_specs=[
            smem, smem,
            pl.BlockSpec((CHUNK, n_k), lambda b, j: (0, 0)),
            pl.BlockSpec((qb, ATTN_DIM), lambda b, j: (b * nb + j, ZC_Q // ATTN_DIM)),
            pl.BlockSpec((qb, KV_DIM), lambda b, j: (b * nb + jnp.maximum(j - 1, 0), kblk)),
            pl.BlockSpec((qb, KV_DIM), lambda b, j: (b * nb + j, kblk)),
            pl.BlockSpec((qb, KV_DIM), lambda b, j: (b * nb + jnp.maximum(j - 1, 0), vblk)),
            pl.BlockSpec((qb, KV_DIM), lambda b, j: (b * nb + j, vblk)),
        ],
        out_specs=pl.BlockSpec((qb, ATTN_DIM), lambda b, j: (b * nb + j, 0)),
        out_shape=jax.ShapeDtypeStruct((batch * seq, ATTN_DIM), BF16),
        scratch_shapes=[pltpu.VMEM((N_KV_HEADS, GROUP * CHUNK, n_k), F32)],
        compiler_params=pltpu.CompilerParams(dimension_semantics=("arbitrary", "arbitrary")),
        name="attn_prompt",
    )(table, sinks, bucket, zz, zz, zz, zz, zz)


def _attn_sample_kernel(table_ref, sinks_ref, bucket_ref, q_ref, k_ref, v_ref, o_ref, bias_ref, *, n_q):
    @pl.when(pl.program_id(0) == 0)
    def _():
        _build_bias(bias_ref, bucket_ref, table_ref, n_q)

    outs = _attend(q_ref[...], k_ref[0], v_ref[0], bias_ref, sinks_ref, n_q, None)
    _store_heads(o_ref, 0, n_q, outs)


def _attn_sample(zz, k_ext, v_ext, batch, n_q, table, sinks):
    n_k = WINDOW + n_q
    bucket = jnp.asarray(_t5_buckets(n_q, n_k))
    smem = pl.BlockSpec(memory_space=pltpu.SMEM)
    return pl.pallas_call(
        functools.partial(_attn_sample_kernel, n_q=n_q),
        grid=(batch,),
        in_specs=[
            smem, smem,
            pl.BlockSpec((n_q, n_k), lambda b: (0, 0)),
            pl.BlockSpec((n_q, ATTN_DIM), lambda b: (b, ZC_Q // ATTN_DIM)),
            pl.BlockSpec((1, n_k, KV_DIM), lambda b: (b, 0, 0)),
            pl.BlockSpec((1, n_k, KV_DIM), lambda b: (b, 0, 0)),
        ],
        out_specs=pl.BlockSpec((n_q, ATTN_DIM), lambda b: (b, 0)),
        out_shape=jax.ShapeDtypeStruct((batch * n_q, ATTN_DIM), BF16),
        scratch_shapes=[pltpu.VMEM((N_KV_HEADS, GROUP * n_q, n_k), F32)],
        compiler_params=pltpu.CompilerParams(dimension_semantics=("arbitrary",)),
        name="attn_sample",
    )(table, sinks, bucket, zz, k_ext, v_ext)


def _merge_kernel(o_ref, ga_ref, gc_ref, cb_ref, u_ref, h0_ref, h1_ref, x_ref, cw_ref, woa_ref, woc_ref, wout_ref,
                  gf_ref, wr_ref, br_ref, h_ref, hn_ref, idx_ref, wts_ref, *, tm, seq_rows, tiles_per_seq):
    u = u_ref[...]
    row = lax.broadcasted_iota(I32, (tm, 1), 0)
    if seq_rows is None:
        at_start = (pl.program_id(0) % tiles_per_seq) == 0
        hist0 = jnp.where(at_start, 0.0, h0_ref[6:7, :])
        hist1 = jnp.where(at_start, 0.0, h1_ref[7:8, :])
        pos = row
    else:
        hist0 = h0_ref[...]
        hist1 = h1_ref[...]
        pos = row % seq_rows
    r1 = pltpu.roll(u, 1, 0)
    r2 = pltpu.roll(u, 2, 0)
    u1 = jnp.where(pos == 0, hist1, r1)
    u2 = jnp.where(pos == 0, hist0, jnp.where(pos == 1, hist1, r2))
    cw = cw_ref[...]
    y = cw[0:1, :] * u2
    y = y + cw[1:2, :] * u1
    y = y + cw[2:3, :] * u
    c = (cb_ref[...] * y).astype(BF16)
    t = ga_ref[...] * jnp.dot(o_ref[...], woa_ref[...], preferred_element_type=F32)
    t = t + gc_ref[...] * jnp.dot(c, woc_ref[...], preferred_element_type=F32)
    h = x_ref[...] + jnp.dot(t.astype(BF16), wout_ref[...], preferred_element_type=F32)
    h_ref[...] = h
    ms = jnp.mean(h * h, axis=-1, keepdims=True)
    hn = (h * lax.rsqrt(ms + NORM_EPS)) * gf_ref[...]
    hn_ref[...] = hn
    logits = lax.dot_general(wr_ref[...], hn.astype(BF16), (((1,), (1,)), ((), ())),
                             preferred_element_type=F32) + br_ref[...]
    erow = lax.broadcasted_iota(I32, logits.shape, 0).astype(F32)
    vals, idxs = [], []
    for _ in range(TOP_K):
        mx = jnp.max(logits, axis=0, keepdims=True)
        ix = jnp.min(jnp.where(logits == mx, erow, float(N_EXPERTS)), axis=0, keepdims=True)
        vals.append(mx)
        idxs.append(ix)
        logits = jnp.where(erow == ix, -jnp.inf, logits)
    es = [jnp.exp(v - vals[0]) for v in vals]
    tot = es[0]
    for e in es[1:]:
        tot = tot + e
    idx_ref[...] = jnp.concatenate(idxs, axis=0).astype(I32)
    wts_ref[...] = jnp.concatenate([e / tot for e in es], axis=0)


def _merge(o, zz, x, hist0, hist1, conv_w, woa, woc, wout, g_ffn, wr_t, br, tm, seq_rows, seq_len):
    t = x.shape[0]
    d = D_MODEL
    full = lambda shape: pl.BlockSpec(shape, lambda i: (0,) * len(shape))
    if seq_rows is None:
        tiles_per_seq = seq_len // tm
        hist_spec = pl.BlockSpec((8, CONV_DIM), lambda i: (jnp.maximum(i * (tm // 8) - 1, 0), ZC_U // CONV_DIM))
        h0_spec = h1_spec = hist_spec
        h0_arg = h1_arg = zz
    else:
        tiles_per_seq = None
        h0_spec = h1_spec = pl.BlockSpec((tm, CONV_DIM), lambda i: (i, 0))
        h0_arg, h1_arg = hist0, hist1
    kern = functools.partial(_merge_kernel, tm=tm, seq_rows=seq_rows, tiles_per_seq=tiles_per_seq)
    vmem = (2 * tm * (ATTN_DIM * 2 + 5 * d * 4 + 4 * CONV_DIM * 4) + 2 * 2 * (2 * ATTN_DIM * d + d * d)
            + 10 * tm * d * 4 + (8 << 20))
    return pl.pallas_call(
        kern,
        grid=(t // tm,),
        in_specs=[
            pl.BlockSpec((tm, ATTN_DIM), lambda i: (i, 0)),
            pl.BlockSpec((tm, d), lambda i: (i, ZC_GA // d)),
            pl.BlockSpec((tm, d), lambda i: (i, ZC_GC // d)),
            pl.BlockSpec((tm, CONV_DIM), lambda i: (i, ZC_CB // CONV_DIM)),
            pl.BlockSpec((tm, CONV_DIM), lambda i: (i, ZC_U // CONV_DIM)),
            h0_spec, h1_spec,
            pl.BlockSpec((tm, d), lambda i: (i, 0)),
            full((3, CONV_DIM)),
            full((ATTN_DIM, d)), full((CONV_DIM, d)), full((d, d)),
            full((1, d)), full((N_EXPERTS, d)), full((N_EXPERTS, 1)),
        ],
        out_specs=[
            pl.BlockSpec((tm, d), lambda i: (i, 0)),
            pl.BlockSpec((tm, d), lambda i: (i, 0)),
            pl.BlockSpec((TOP_K, tm), lambda i: (0, i)),
            pl.BlockSpec((TOP_K, tm), lambda i: (0, i)),
        ],
        out_shape=[
            jax.ShapeDtypeStruct((t, d), F32),
            jax.ShapeDtypeStruct((t, d), F32),
            jax.ShapeDtypeStruct((TOP_K, t), I32),
            jax.ShapeDtypeStruct((TOP_K, t), F32),
        ],
        compiler_params=pltpu.CompilerParams(dimension_semantics=("arbitrary",), vmem_limit_bytes=_vmem_limit(vmem)),
        name="merge_prompt" if seq_rows is None else "merge_sample",
    )(o, zz, zz, zz, zz, h0_arg, h1_arg, x, conv_w, woa, woc, wout, g_ffn.reshape(1, d), wr_t,
      br.reshape(N_EXPERTS, 1))


ROUTE_TILE = 384


def _route_kernel(idx_ref, dest_ref, cnt_ref, cnt_acc, carry, pst, tri_ref):
    p = pl.program_id(0)
    j = pl.program_id(1)
    tr = idx_ref.shape[1]
    idx = idx_ref[...]
    erow = lax.broadcasted_iota(I32, (N_EXPERTS, tr), 0)
    masks = [(erow == idx[k:k + 1, :]).astype(F32) for k in range(TOP_K)]
    mtot = masks[0]
    for m in masks[1:]:
        mtot = mtot + m
    tile_cnt = jnp.sum(mtot, axis=1, keepdims=True)

    @pl.when((p == 0) & (j == 0))
    def _():
        cnt_acc[...] = jnp.zeros_like(cnt_acc)

    @pl.when(p == 0)
    def _():
        cnt_acc[...] += jnp.broadcast_to(tile_cnt, cnt_acc.shape)

    @pl.when((p == 1) & (j == 0))
    def _():
        cnt = cnt_acc[:, 0:1]
        padded = jnp.floor((cnt + float(ROW_BLK - 1)) / float(ROW_BLK)) * float(ROW_BLK)
        r = lax.broadcasted_iota(I32, (N_EXPERTS, N_EXPERTS), 0)
        c = lax.broadcasted_iota(I32, (N_EXPERTS, N_EXPERTS), 1)
        rowv = jnp.sum(jnp.where(r == c, padded, 0.0), axis=0, keepdims=True)
        start = jnp.sum(jnp.where(c < r, rowv, 0.0), axis=1, keepdims=True)
        pst[...] = jnp.broadcast_to(start, pst.shape)
        carry[...] = jnp.zeros_like(carry)
        a = lax.broadcasted_iota(I32, (tr, tr), 0)
        b = lax.broadcasted_iota(I32, (tr, tr), 1)
        tri_ref[...] = (a < b).astype(BF16)

    @pl.when(p == 1)
    def _():
        excl = jnp.dot(mtot.astype(BF16), tri_ref[...], preferred_element_type=F32)
        val = pst[:, 0:1] + carry[:, 0:1] + excl
        dest = jnp.concatenate([jnp.sum(m * val, axis=0, keepdims=True) for m in masks], axis=0)
        dest_ref[...] = dest.astype(I32)
        carry[...] += jnp.broadcast_to(tile_cnt, carry.shape)
        cnt_ref[...] = cnt_acc[...].astype(I32)


def _route(idx_t):
    t = idx_t.shape[1]
    tr = ROUTE_TILE
    return pl.pallas_call(
        _route_kernel,
        grid=(2, t // tr),
        in_specs=[pl.BlockSpec((TOP_K, tr), lambda p, j: (0, j))],
        out_specs=[
            pl.BlockSpec((TOP_K, tr), lambda p, j: (0, j * p)),
            pl.BlockSpec((N_EXPERTS, LANES), lambda p, j: (0, 0)),
        ],
        out_shape=[jax.ShapeDtypeStruct((TOP_K, t), I32), jax.ShapeDtypeStruct((N_EXPERTS, LANES), I32)],
        scratch_shapes=[pltpu.VMEM((N_EXPERTS, LANES), F32), pltpu.VMEM((N_EXPERTS, LANES), F32),
                        pltpu.VMEM((N_EXPERTS, LANES), F32), pltpu.VMEM((tr, tr), BF16)],
        compiler_params=pltpu.CompilerParams(dimension_semantics=("arbitrary", "arbitrary")),
        name="route",
    )(idx_t)


DISPATCH_TILE = 384
DMA_UNROLL = 8


def _dispatch_kernel(dest_ref, hn_ref, xs_hbm, sem):
    tt = dest_ref.shape[1]

    def issue(t, c):
        for k in range(TOP_K):
            pltpu.make_async_copy(hn_ref.at[pl.ds(t, 1)], xs_hbm.at[pl.ds(dest_ref[k, t], 1)], sem).start()
        return c

    lax.fori_loop(0, tt, issue, 0, unroll=DMA_UNROLL)

    def drain(t, c):
        for k in range(TOP_K):
            pltpu.make_async_copy(hn_ref.at[pl.ds(0, 1)], xs_hbm.at[pl.ds(0, 1)], sem).wait()
        return c

    lax.fori_loop(0, tt, drain, 0, unroll=DMA_UNROLL)


def _dispatch(dest, hn, n_slots):
    t, d = hn.shape
    tt = DISPATCH_TILE
    return pl.pallas_call(
        _dispatch_kernel,
        grid=(t // tt,),
        in_specs=[pl.BlockSpec((TOP_K, tt), lambda i: (0, i), memory_space=pltpu.SMEM),
                  pl.BlockSpec((tt, d), lambda i: (i, 0))],
        out_specs=pl.BlockSpec(memory_space=pl.ANY),
        out_shape=jax.ShapeDtypeStruct((n_slots, d), hn.dtype),
        scratch_shapes=[pltpu.SemaphoreType.DMA(())],
        compiler_params=pltpu.CompilerParams(dimension_semantics=("arbitrary",)),
        name="dispatch",
    )(dest, hn)


def _expert_kernel(we_ref, ws_ref, wnb_ref, wvalid_ref, wact_ref,
                   xs_hbm, wg_ref, wu_ref, wd_ref, bg_ref, bu_ref, bd_ref, ys_hbm,
                   x_sb, acc, wgu_bf, wd_bf, stage, sem_in, sem_out):
    del we_ref, wact_ref
    w = pl.program_id(0)
    f = pl.program_id(1)
    n_items = pl.num_programs(0)
    last_f = N_FF_TILES - 1
    slot = w % 2
    nb = wnb_ref[w]
    row = lax.broadcasted_iota(I32, (ROW_BLK, 1), 0)

    def x_copy(item, j):
        src = xs_hbm.at[pl.ds(pl.multiple_of(ws_ref[item] + j * ROW_BLK, ROW_BLK), ROW_BLK)]
        return pltpu.make_async_copy(src, stage, sem_in)

    def x_cast(item, j, sl):
        r0 = pl.multiple_of(j * ROW_BLK, ROW_BLK)
        x_sb[sl, pl.ds(r0, ROW_BLK), :] = jnp.where(row + r0 < wvalid_ref[item], stage[...], 0.0).astype(BF16)

    def y_copy(item, j, sl):
        r0 = pl.multiple_of(j * ROW_BLK, ROW_BLK)
        dst = ys_hbm.at[pl.ds(pl.multiple_of(ws_ref[item] + r0, ROW_BLK), ROW_BLK)]
        return pltpu.make_async_copy(acc.at[sl, pl.ds(r0, ROW_BLK)], dst, sem_out.at[sl])

    def y_wait_all(item, sl):
        def body(j, c):
            y_copy(item, j, sl).wait()
            return c

        lax.fori_loop(0, wnb_ref[item], body, 0)

    @pl.when((w == 0) & (f == 0))
    def _first_rows():
        def body(j, c):
            cp = x_copy(0, j)
            cp.start()
            cp.wait()
            x_cast(0, j, 0)
            return c

        lax.fori_loop(0, wnb_ref[0], body, 0)

    nxt = jnp.minimum(w + 1, n_items - 1)
    prefetch = (w + 1 < n_items) & (f < wnb_ref[nxt])

    @pl.when(prefetch)
    def _():
        x_copy(nxt, f).start()

    @pl.when((f == 0) & (w >= 2))
    def _reclaim_acc():
        y_wait_all(jnp.maximum(w - 2, 0), slot)

    @pl.when(nb > 0)
    def _compute():
        wgu_bf[:, :FF_TILE] = wg_ref[0].astype(BF16)
        wgu_bf[:, FF_TILE:] = wu_ref[0].astype(BF16)
        wd_bf[...] = wd_ref[0].astype(BF16)
        bg = bg_ref[0]
        bu = bu_ref[0]
        bd = bd_ref[0]

        def contribution(r0):
            x = x_sb[slot, pl.ds(r0, ROW_BLK), :]
            gu = jnp.dot(x, wgu_bf[...], preferred_element_type=F32)
            g = jnp.minimum(gu[:, :FF_TILE] + bg, SWIGLU_LIMIT)
            u = jnp.clip(gu[:, FF_TILE:] + bu, -SWIGLU_LIMIT, SWIGLU_LIMIT)
            hh = (u + 1.0) * (g * jax.nn.sigmoid(SWIGLU_ALPHA * g))
            return jnp.dot(hh.astype(BF16), wd_bf[...], preferred_element_type=F32)

        @pl.when(f == 0)
        def _():
            def body(m, c):
                r0 = pl.multiple_of(m * ROW_BLK, ROW_BLK)
                acc[slot, pl.ds(r0, ROW_BLK), :] = contribution(r0) + bd
                return c

            lax.fori_loop(0, nb, body, 0)

        @pl.when(f > 0)
        def _():
            def body(m, c):
                r0 = pl.multiple_of(m * ROW_BLK, ROW_BLK)
                acc[slot, pl.ds(r0, ROW_BLK), :] += contribution(r0)
                return c

            lax.fori_loop(0, nb, body, 0)

    @pl.when((f == last_f) & (nb > 0))
    def _store():
        def body(j, c):
            y_copy(w, j, slot).start()
            return c

        lax.fori_loop(0, nb, body, 0)

    @pl.when(prefetch)
    def _():
        x_copy(nxt, f).wait()
        x_cast(nxt, f, 1 - slot)

    @pl.when((w == n_items - 1) & (f == last_f))
    def _drain():
        y_wait_all(jnp.maximum(w - 1, 0), 1 - slot)
        y_wait_all(w, slot)


def _experts(xs, tables, w_gate_up, b_gate_up, w_down, b_down):
    n_slots, d = xs.shape
    n_items = tables[0].shape[0]
    tf = FF_TILE
    nf = N_FF_TILES

    def ff(f, act, w):
        return f * act[w] + (nf - 1) * (1 - act[w])

    grid_spec = pltpu.PrefetchScalarGridSpec(
        num_scalar_prefetch=5,
        grid=(n_items, nf),
        in_specs=[
            pl.BlockSpec(memory_space=pl.ANY),
            pl.BlockSpec((1, d, tf), lambda w, f, we, ws, wnb, wv, act: (we[w], 0, ff(f, act, w))),
            pl.BlockSpec((1, d, tf), lambda w, f, we, ws, wnb, wv, act: (we[w], 0, nf + ff(f, act, w))),
            pl.BlockSpec((1, tf, d), lambda w, f, we, ws, wnb, wv, act: (we[w], ff(f, act, w), 0)),
            pl.BlockSpec((1, 1, tf), lambda w, f, we, ws, wnb, wv, act: (we[w], 0, ff(f, act, w))),
            pl.BlockSpec((1, 1, tf), lambda w, f, we, ws, wnb, wv, act: (we[w], 0, nf + ff(f, act, w))),
            pl.BlockSpec((1, 1, d), lambda w, f, we, ws, wnb, wv, act: (we[w], 0, 0)),
        ],
        out_specs=pl.BlockSpec(memory_space=pl.ANY),
        scratch_shapes=[
            pltpu.VMEM((2, SUPER_ROWS, d), BF16),
            pltpu.VMEM((2, SUPER_ROWS, d), F32),
            pltpu.VMEM((d, 2 * tf), BF16), pltpu.VMEM((tf, d), BF16),
            pltpu.VMEM((ROW_BLK, d), F32),
            pltpu.SemaphoreType.DMA(()), pltpu.SemaphoreType.DMA((2,)),
        ],
    )
    vmem = (2 * SUPER_ROWS * d * 6 + 2 * 3 * d * tf * 4 + 3 * d * tf * 2 + ROW_BLK * d * 4
            + 6 * ROW_BLK * d * 4 + (6 << 20))
    return pl.pallas_call(
        _expert_kernel,
        grid_spec=grid_spec,
        out_shape=jax.ShapeDtypeStruct((n_slots, d), F32),
        compiler_params=pltpu.CompilerParams(
            dimension_semantics=("arbitrary", "arbitrary"), vmem_limit_bytes=_vmem_limit(vmem)),
        name="experts",
    )(*tables, xs, w_gate_up, w_gate_up, w_down, b_gate_up.reshape(N_EXPERTS, 1, 2 * D_FF),
      b_gate_up.reshape(N_EXPERTS, 1, 2 * D_FF), b_down.reshape(N_EXPERTS, 1, d))


def _work_tables(counts, n_items):
    padded = (counts + ROW_BLK - 1) // ROW_BLK * ROW_BLK
    pstart = jnp.cumsum(padded) - padded
    items_e = (padded + SUPER_ROWS - 1) // SUPER_ROWS
    item_end = jnp.cumsum(items_e)
    item_start = item_end - items_e
    w = jnp.arange(n_items, dtype=I32)
    total = item_end[-1]
    active = (w < total).astype(I32)
    wl = jnp.minimum(w, total - 1)
    e = jnp.minimum(jnp.searchsorted(item_end, wl, side='right'), N_EXPERTS - 1).astype(I32)
    local = wl - item_start[e]
    start = pstart[e] + local * SUPER_ROWS
    rows = jnp.minimum(SUPER_ROWS, padded[e] - local * SUPER_ROWS)
    valid = jnp.clip(counts[e] - local * SUPER_ROWS, 0, rows)
    nb = (rows // ROW_BLK) * active
    return (e.astype(I32), start.astype(I32), nb.astype(I32), valid.astype(I32), active)


COMBINE_TILE = 128


def _combine_kernel(dest_ref, dest_next_ref, wts_ref, h_ref, g_ref, ys_hbm, out_ref, buf, sem):
    i = pl.program_id(0)
    n = pl.num_programs(0)
    slot = i % 2
    tt = dest_ref.shape[1]

    def issue(dref, sl):
        def body(t, c):
            for k in range(TOP_K):
                pltpu.make_async_copy(ys_hbm.at[pl.ds(dref[k, t], 1)], buf.at[sl, k, pl.ds(t, 1)], sem.at[sl]).start()
            return c

        lax.fori_loop(0, tt, body, 0, unroll=DMA_UNROLL)

    @pl.when(i == 0)
    def _():
        issue(dest_ref, 0)

    @pl.when(i + 1 < n)
    def _():
        issue(dest_next_ref, 1 - slot)

    def drain(t, c):
        for k in range(TOP_K):
            pltpu.make_async_copy(ys_hbm.at[pl.ds(0, 1)], buf.at[slot, k, pl.ds(0, 1)], sem.at[slot]).wait()
        return c

    lax.fori_loop(0, tt, drain, 0, unroll=DMA_UNROLL)
    wts = wts_ref[...]
    moe = wts[:, 0:1] * buf[slot, 0]
    for k in range(1, TOP_K):
        moe = moe + wts[:, k:k + 1] * buf[slot, k]
    y = h_ref[...] + moe
    ms = jnp.mean(y * y, axis=-1, keepdims=True)
    out_ref[...] = (y * lax.rsqrt(ms + NORM_EPS)) * g_ref[...]


def _combine(dest, wts, h, g_final, ys):
    t, d = h.shape
    tt = COMBINE_TILE
    n = t // tt
    return pl.pallas_call(
        _combine_kernel,
        grid=(n,),
        in_specs=[
            pl.BlockSpec((TOP_K, tt), lambda i: (0, i), memory_space=pltpu.SMEM),
            pl.BlockSpec((TOP_K, tt), lambda i: (0, jnp.minimum(i + 1, n - 1)), memory_space=pltpu.SMEM),
            pl.BlockSpec((tt, TOP_K), lambda i: (i, 0)),
            pl.BlockSpec((tt, d), lambda i: (i, 0)),
            pl.BlockSpec((1, d), lambda i: (0, 0)),
            pl.BlockSpec(memory_space=pl.ANY),
        ],
        out_specs=pl.BlockSpec((tt, d), lambda i: (i, 0)),
        out_shape=jax.ShapeDtypeStruct((t, d), F32),
        scratch_shapes=[pltpu.VMEM((2, TOP_K, tt, d), F32), pltpu.SemaphoreType.DMA((2,))],
        compiler_params=pltpu.CompilerParams(dimension_semantics=("arbitrary",)),
        name="combine",
    )(dest, dest, wts, h, g_final.reshape(1, d), ys)


def kernel(x_prompt, x_sample, cache_attn_k, cache_attn_v, state_conv, rel_bias_table, norm_mix_g, w_in, b_gate,
           attn_sinks, conv_w, w_o_attn, w_o_conv, w_out, norm_ffn_g, w_router, b_router, w_gate_up, b_gate_up,
           w_down, b_down, norm_final_g):
    batch, seq, d = x_prompt.shape
    dbatch, dseq, _ = x_sample.shape
    assert norm_mix_g.shape[0] == 1 and d == D_MODEL
    t_p, t_s = batch * seq, dbatch * dseq
    xp = x_prompt.reshape(t_p, d)
    xs = x_sample.reshape(t_s, d)

    w_in_bf = w_in[0].astype(BF16)
    woa = w_o_attn[0].astype(BF16)
    woc = w_o_conv[0].astype(BF16)
    wout = w_out[0].astype(BF16)
    wr_t = w_router[0].T.astype(BF16)

    zz_p = _in_proj(xp, norm_mix_g[0], w_in_bf, b_gate[0], tm=1024)
    zz_s = _in_proj(xs, norm_mix_g[0], w_in_bf, b_gate[0], tm=t_s)

    o_p = _attn_prompt(zz_p, batch, seq, rel_bias_table, attn_sinks[0])
    k_new = zz_s[:, ZC_K:ZC_K + KV_DIM].reshape(dbatch, dseq, KV_DIM)
    v_new = zz_s[:, ZC_V:ZC_V + KV_DIM].reshape(dbatch, dseq, KV_DIM)
    k_ext = jnp.concatenate([cache_attn_k[0].reshape(dbatch, WINDOW, KV_DIM), k_new], axis=1)
    v_ext = jnp.concatenate([cache_attn_v[0].reshape(dbatch, WINDOW, KV_DIM), v_new], axis=1)
    o_s = _attn_sample(zz_s, k_ext, v_ext, dbatch, dseq, rel_bias_table, attn_sinks[0])

    hist0 = jnp.repeat(state_conv[0][:, 0, :], dseq, axis=0)
    hist1 = jnp.repeat(state_conv[0][:, 1, :], dseq, axis=0)
    merge_args = (conv_w[0], woa, woc, wout, norm_ffn_g[0], wr_t, b_router[0])
    h_p, hn_p, idx_p, wts_p = _merge(o_p, zz_p, xp, None, None, *merge_args, tm=256, seq_rows=None, seq_len=seq)
    h_s, hn_s, idx_s, wts_s = _merge(o_s, zz_s, xs, hist0, hist1, *merge_args, tm=t_s, seq_rows=dseq, seq_len=dseq)

    hn = jnp.concatenate([hn_p, hn_s], axis=0)
    idx_t = jnp.concatenate([idx_p, idx_s], axis=1)
    wts = jnp.concatenate([wts_p, wts_s], axis=1).T
    t_all = t_p + t_s
    n_assign = t_all * TOP_K
    n_slots = (n_assign + N_EXPERTS * (ROW_BLK - 1) + ROW_BLK - 1) // ROW_BLK * ROW_BLK
    n_items = N_EXPERTS + n_slots // SUPER_ROWS

    dest, cnt = _route(idx_t)
    tables = _work_tables(cnt[:, 0], n_items)
    xs_sorted = _dispatch(dest, hn, n_slots)
    ys = _experts(xs_sorted, tables, w_gate_up[0], b_gate_up[0], w_down[0], b_down[0])
    y_p = _combine(dest[:, :t_p], wts[:t_p], h_p, norm_final_g, ys)
    y_s = _combine(dest[:, t_p:], wts[t_p:], h_s, norm_final_g, ys)

    def kv_tail(z, c0, b, s):
        return z[:, c0:c0 + KV_DIM].reshape(b, s, N_KV_HEADS, HEAD_DIM)[:, -WINDOW:][None]

    u_p = zz_p[:, ZC_U:ZC_U + CONV_DIM].reshape(batch, seq, CONV_DIM)
    u_s = zz_s[:, ZC_U:ZC_U + CONV_DIM].reshape(dbatch, dseq, CONV_DIM)
    conv_hist = state_conv.shape[2]
    new_conv_p = u_p[:, -conv_hist:][None]
    new_conv_s = jnp.concatenate([state_conv[0], u_s], axis=1)[:, -conv_hist:][None]
    new_k_s = k_ext[:, -WINDOW:].reshape(dbatch, WINDOW, N_KV_HEADS, HEAD_DIM)[None]
    new_v_s = v_ext[:, -WINDOW:].reshape(dbatch, WINDOW, N_KV_HEADS, HEAD_DIM)[None]
    return (y_p.reshape(batch, seq, d), y_s.reshape(dbatch, dseq, d),
            kv_tail(zz_p, ZC_K, batch, seq), kv_tail(zz_p, ZC_V, batch, seq), new_conv_p,
            new_k_s, new_v_s, new_conv_s)
```

```python
import functools
import math

import numpy as np
import jax
import jax.numpy as jnp
from jax import lax
from jax.experimental import pallas as pl
from jax.experimental.pallas import tpu as pltpu

F32 = jnp.float32
BF16 = jnp.bfloat16
I32 = jnp.int32

D_MODEL = 2048
CHUNK = 64
HEAD_DIM = 64
N_Q_HEADS = 16
N_KV_HEADS = 4
GROUP = N_Q_HEADS // N_KV_HEADS
ATTN_DIM = N_Q_HEADS * HEAD_DIM
KV_DIM = N_KV_HEADS * HEAD_DIM
WINDOW = 128
CONV_DIM = D_MODEL // 2
N_BUCKETS = 32
MAX_DISTANCE = 128
N_EXPERTS = 32
TOP_K = 4
D_FF = D_MODEL
SWIGLU_LIMIT = 7.0
SWIGLU_ALPHA = 1.702
NORM_EPS = 1e-5
IN_DIM = ATTN_DIM + 2 * KV_DIM + 3 * CONV_DIM + 2 * D_MODEL

V7X_VMEM_BYTES = 64 * 1024 * 1024
LANES = 128

ZC_GA = 0
ZC_GC = D_MODEL
ZC_CB = 2 * D_MODEL
ZC_U = ZC_CB + CONV_DIM
ZC_Q = ZC_U + CONV_DIM
ZC_K = ZC_Q + ATTN_DIM
ZC_V = ZC_K + KV_DIM
Z_DIM = ZC_V + KV_DIM
PROJ_TN = 512

ROW_BLK = 256
SUPER_BLKS = 5
SUPER_ROWS = SUPER_BLKS * ROW_BLK
FF_TILE = 256
N_FF_TILES = D_FF // FF_TILE
assert SUPER_BLKS <= N_FF_TILES


def _vmem_limit(nbytes):
    return int(min(nbytes, V7X_VMEM_BYTES - 6 * 1024 * 1024))


_MODE_PLAIN, _MODE_STASH0, _MODE_STASH1, _MODE_MUL0, _MODE_MUL1, _MODE_GATE = range(6)


def _proj_schedule():
    t = PROJ_TN
    cc0 = (ATTN_DIM + 2 * KV_DIM + CONV_DIM) // t
    ch0 = (ATTN_DIM + 2 * KV_DIM + 2 * CONV_DIM) // t
    cb0 = (ATTN_DIM + 2 * KV_DIM) // t
    ga0 = (ATTN_DIM + 2 * KV_DIM + 3 * CONV_DIM) // t
    steps = [
        (cc0, ZC_U // t, 0, _MODE_STASH0),
        (cc0 + 1, ZC_U // t, 0, _MODE_STASH1),
        (ch0, ZC_U // t, 0, _MODE_MUL0),
        (ch0 + 1, ZC_U // t + 1, 0, _MODE_MUL1),
        (0, ZC_Q // t, 0, _MODE_PLAIN),
        (1, ZC_Q // t + 1, 0, _MODE_PLAIN),
        (2, ZC_K // t, 0, _MODE_PLAIN),
        (cb0, ZC_CB // t, 0, _MODE_PLAIN),
        (cb0 + 1, ZC_CB // t + 1, 0, _MODE_PLAIN),
    ]
    for j in range(2 * D_MODEL // t):
        steps.append((ga0 + j, j, j, _MODE_GATE))
    return np.asarray(steps, dtype=np.int32)


def _in_proj_kernel(wt_ref, ot_ref, bt_ref, md_ref, x_ref, g_ref, w_ref, b_ref, o_ref, xn_ref, stash_ref):
    del wt_ref, ot_ref, bt_ref
    n = pl.program_id(1)

    @pl.when(n == 0)
    def _():
        x = x_ref[...]
        ms = jnp.mean(x * x, axis=-1, keepdims=True)
        xn_ref[...] = ((x * lax.rsqrt(ms + NORM_EPS)) * g_ref[...]).astype(BF16)

    def product():
        return jnp.dot(xn_ref[...], w_ref[...], preferred_element_type=F32)

    md = md_ref[n]

    @pl.when(md == _MODE_PLAIN)
    def _():
        o_ref[...] = product()

    @pl.when(md == _MODE_STASH0)
    def _():
        stash_ref[0] = product()

    @pl.when(md == _MODE_STASH1)
    def _():
        stash_ref[1] = product()

    @pl.when(md == _MODE_MUL0)
    def _():
        o_ref[...] = stash_ref[0] * product()

    @pl.when(md == _MODE_MUL1)
    def _():
        o_ref[...] = stash_ref[1] * product()

    @pl.when(md == _MODE_GATE)
    def _():
        o_ref[...] = jax.nn.sigmoid(product() + b_ref[...])


def _in_proj(x, g_mix, w_in_bf, b_gate, tm):
    t, d = x.shape
    sched = _proj_schedule()
    n_steps = sched.shape[0]
    grid_spec = pltpu.PrefetchScalarGridSpec(
        num_scalar_prefetch=4,
        grid=(t // tm, n_steps),
        in_specs=[
            pl.BlockSpec((tm, d), lambda m, n, wt, ot, bt, md: (m, 0)),
            pl.BlockSpec((1, d), lambda m, n, wt, ot, bt, md: (0, 0)),
            pl.BlockSpec((d, PROJ_TN), lambda m, n, wt, ot, bt, md: (0, wt[n])),
            pl.BlockSpec((1, PROJ_TN), lambda m, n, wt, ot, bt, md: (0, bt[n])),
        ],
        out_specs=pl.BlockSpec((tm, PROJ_TN), lambda m, n, wt, ot, bt, md: (m, ot[n])),
        scratch_shapes=[pltpu.VMEM((tm, d), BF16), pltpu.VMEM((2, tm, PROJ_TN), F32)],
    )
    vmem = 2 * tm * d * 4 + tm * d * 2 + 2 * d * PROJ_TN * 2 + 6 * tm * PROJ_TN * 4 + (8 << 20)
    return pl.pallas_call(
        _in_proj_kernel,
        grid_spec=grid_spec,
        out_shape=jax.ShapeDtypeStruct((t, Z_DIM), F32),
        compiler_params=pltpu.CompilerParams(
            dimension_semantics=("arbitrary", "arbitrary"), vmem_limit_bytes=_vmem_limit(vmem)),
        name="in_proj",
    )(jnp.asarray(sched[:, 0]), jnp.asarray(sched[:, 1]), jnp.asarray(sched[:, 2]), jnp.asarray(sched[:, 3]),
      x, g_mix.reshape(1, d), w_in_bf, b_gate.reshape(1, 2 * D_MODEL))


def _t5_buckets(n_q, n_k):
    half = N_BUCKETS // 2
    max_exact = half // 2
    rel = (np.arange(n_k, dtype=np.int32)[None, :] - WINDOW) - np.arange(n_q, dtype=np.int32)[:, None]
    ret = np.where(rel > 0, half, 0)
    n = np.abs(rel)
    nf = np.maximum(n, 1).astype(np.float32)
    large = max_exact + (np.log(nf / np.float32(max_exact)) / np.float32(math.log(MAX_DISTANCE / max_exact))
                         * np.float32(half - max_exact)).astype(np.int32)
    large = np.minimum(large, half - 1)
    return (ret + np.where(n < max_exact, n, large)).astype(np.int32)


def _build_bias(bias_ref, bucket_ref, table_ref, n_q):
    bucket = bucket_ref[...]
    for kvh in range(N_KV_HEADS):
        for g in range(GROUP):
            h = kvh * GROUP + g
            mat = jnp.zeros(bucket.shape, F32)
            for b in range(N_BUCKETS):
                mat = jnp.where(bucket == b, table_ref[b, h], mat)
            bias_ref[kvh, g * n_q:(g + 1) * n_q, :] = mat


def _attend(q, kwin, vwin, bias_ref, sinks_ref, n_q, invalid):
    outs = []
    grp = lax.broadcasted_iota(I32, (GROUP * n_q, 1), 0) // n_q
    for kvh in range(N_KV_HEADS):
        c0 = kvh * GROUP * HEAD_DIM
        qh = jnp.concatenate([q[:, c0 + g * HEAD_DIM:c0 + (g + 1) * HEAD_DIM] for g in range(GROUP)], axis=0)
        qh = (qh * (HEAD_DIM ** -0.5)).astype(BF16)
        kh = kwin[:, kvh * HEAD_DIM:(kvh + 1) * HEAD_DIM].astype(BF16)
        vh = vwin[:, kvh * HEAD_DIM:(kvh + 1) * HEAD_DIM].astype(BF16)
        s = lax.dot_general(qh, kh, (((1,), (1,)), ((), ())), preferred_element_type=F32) + bias_ref[kvh]
        if invalid is not None:
            s = jnp.where(invalid, -1e30, s)
        sink = jnp.zeros((GROUP * n_q, 1), F32)
        for g in range(GROUP):
            sink = jnp.where(grp == g, sinks_ref[kvh * GROUP + g], sink)
        m = jnp.maximum(jnp.max(s, axis=-1, keepdims=True), sink)
        p = jnp.exp(s - m)
        denom = jnp.sum(p, axis=-1, keepdims=True) + jnp.exp(sink - m)
        o = jnp.dot(p.astype(BF16), vh, preferred_element_type=F32) / denom
        outs.append(o)
    return outs


def _store_heads(o_ref, r0, n_q, outs):
    for kvh, o in enumerate(outs):
        for g in range(GROUP):
            c = (kvh * GROUP + g) * HEAD_DIM
            o_ref[r0:r0 + n_q, c:c + HEAD_DIM] = o[g * n_q:(g + 1) * n_q].astype(o_ref.dtype)


def _attn_prompt_kernel(table_ref, sinks_ref, bucket_ref, q_ref, kp_ref, kc_ref, vp_ref, vc_ref, o_ref, bias_ref):
    first = (pl.program_id(0) == 0) & (pl.program_id(1) == 0)

    @pl.when(first)
    def _():
        _build_bias(bias_ref, bucket_ref, table_ref, CHUNK)

    kwin = jnp.concatenate([kp_ref[...], kc_ref[...]], axis=0)
    vwin = jnp.concatenate([vp_ref[...], vc_ref[...]], axis=0)
    n_k = WINDOW + CHUNK
    seq_start = pl.program_id(1) == 0
    col = lax.broadcasted_iota(I32, (1, n_k), 1)
    for c in range(WINDOW // CHUNK):
        r0 = c * CHUNK
        invalid = seq_start & (col + r0 < WINDOW)
        outs = _attend(q_ref[r0:r0 + CHUNK, :], kwin[r0:r0 + n_k], vwin[r0:r0 + n_k], bias_ref, sinks_ref,
                       CHUNK, invalid)
        _store_heads(o_ref, r0, CHUNK, outs)


def _attn_prompt(zz, batch, seq, table, sinks):
    qb = WINDOW
    nb = seq // qb
    n_k = WINDOW + CHUNK
    bucket = jnp.asarray(_t5_buckets(CHUNK, n_k))
    smem = pl.BlockSpec(memory_space=pltpu.SMEM)
    kblk, vblk = ZC_K // KV_DIM, ZC_V // KV_DIM
    return pl.pallas_call(
        _attn_prompt_kernel,
        grid=(batch, nb),
        in_specs=[
            smem, smem,
            pl.BlockSpec((CHUNK, n_k), lambda b, j: (0, 0)),
            pl.BlockSpec((qb, ATTN_DIM), lambda b, j: (b * nb + j, ZC_Q // ATTN_DIM)),
            pl.BlockSpec((qb, KV_DIM), lambda b, j: (b * nb + jnp.maximum(j - 1, 0), kblk)),
            pl.BlockSpec((qb, KV_DIM), lambda b, j: (b * nb + j, kblk)),
            pl.BlockSpec((qb, KV_DIM), lambda b, j: (b * nb + jnp.maximum(j - 1, 0), vblk)),
            pl.BlockSpec((qb, KV_DIM), lambda b, j: (b * nb + j, vblk)),
        ],
        out_specs=pl.BlockSpec((qb, ATTN_DIM), lambda b, j: (b * nb + j, 0)),
        out_shape=jax.ShapeDtypeStruct((batch * seq, ATTN_DIM), BF16),
        scratch_shapes=[pltpu.VMEM((N_KV_HEADS, GROUP * CHUNK, n_k), F32)],
        compiler_params=pltpu.CompilerParams(dimension_semantics=("arbitrary", "arbitrary")),
        name="attn_prompt",
    )(table, sinks, bucket, zz, zz, zz, zz, zz)


def _attn_sample_kernel(table_ref, sinks_ref, bucket_ref, q_ref, k_ref, v_ref, o_ref, bias_ref, *, n_q):
    @pl.when(pl.program_id(0) == 0)
    def _():
        _build_bias(bias_ref, bucket_ref, table_ref, n_q)

    outs = _attend(q_ref[...], k_ref[0], v_ref[0], bias_ref, sinks_ref, n_q, None)
    _store_heads(o_ref, 0, n_q, outs)


def _attn_sample(zz, k_ext, v_ext, batch, n_q, table, sinks):
    n_k = WINDOW + n_q
    bucket = jnp.asarray(_t5_buckets(n_q, n_k))
    smem = pl.BlockSpec(memory_space=pltpu.SMEM)
    return pl.pallas_call(
        functools.partial(_attn_sample_kernel, n_q=n_q),
        grid=(batch,),
        in_specs=[
            smem, smem,
            pl.BlockSpec((n_q, n_k), lambda b: (0, 0)),
            pl.BlockSpec((n_q, ATTN_DIM), lambda b: (b, ZC_Q // ATTN_DIM)),
            pl.BlockSpec((1, n_k, KV_DIM), lambda b: (b, 0, 0)),
            pl.BlockSpec((1, n_k, KV_DIM), lambda b: (b, 0, 0)),
        ],
        out_specs=pl.BlockSpec((n_q, ATTN_DIM), lambda b: (b, 0)),
        out_shape=jax.ShapeDtypeStruct((batch * n_q, ATTN_DIM), BF16),
        scratch_shapes=[pltpu.VMEM((N_KV_HEADS, GROUP * n_q, n_k), F32)],
        compiler_params=pltpu.CompilerParams(dimension_semantics=("arbitrary",)),
        name="attn_sample",
    )(table, sinks, bucket, zz, k_ext, v_ext)


_MERGE_INPUTS = 15


def _merge_kernel(*refs, tm, seq_rows, tiles_per_seq):
    (o_ref, ga_ref, gc_ref, cb_ref, u_ref, h0_ref, h1_ref, x_ref, cw_ref, woa_ref, woc_ref, wout_ref,
     gf_ref, wr_ref, br_ref) = refs[:_MERGE_INPUTS]
    h_ref, hn_ref, idx_ref, wts_ref = refs[-4:]
    u = u_ref[...]
    row = lax.broadcasted_iota(I32, (tm, 1), 0)
    if seq_rows is None:
        at_start = (pl.program_id(0) % tiles_per_seq) == 0
        hist0 = jnp.where(at_start, 0.0, h0_ref[6:7, :])
        hist1 = jnp.where(at_start, 0.0, h1_ref[7:8, :])
        pos = row
    else:
        hist0 = h0_ref[...]
        hist1 = h1_ref[...]
        pos = row % seq_rows
    r1 = pltpu.roll(u, 1, 0)
    r2 = pltpu.roll(u, 2, 0)
    u1 = jnp.where(pos == 0, hist1, r1)
    u2 = jnp.where(pos == 0, hist0, jnp.where(pos == 1, hist1, r2))
    cw = cw_ref[...]
    y = cw[0:1, :] * u2
    y = y + cw[1:2, :] * u1
    y = y + cw[2:3, :] * u
    c = (cb_ref[...] * y).astype(BF16)
    t = ga_ref[...] * jnp.dot(o_ref[...], woa_ref[...], preferred_element_type=F32)
    t = t + gc_ref[...] * jnp.dot(c, woc_ref[...], preferred_element_type=F32)
    h = x_ref[...] + jnp.dot(t.astype(BF16), wout_ref[...], preferred_element_type=F32)
    h_ref[...] = h
    ms = jnp.mean(h * h, axis=-1, keepdims=True)
    hn = (h * lax.rsqrt(ms + NORM_EPS)) * gf_ref[...]
    hn_ref[...] = hn
    logits = lax.dot_general(wr_ref[...], hn.astype(BF16), (((1,), (1,)), ((), ())),
                             preferred_element_type=F32) + br_ref[...]
    erow = lax.broadcasted_iota(I32, logits.shape, 0).astype(F32)
    vals, idxs = [], []
    for _ in range(TOP_K):
        mx = jnp.max(logits, axis=0, keepdims=True)
        ix = jnp.min(jnp.where(logits == mx, erow, float(N_EXPERTS)), axis=0, keepdims=True)
        vals.append(mx)
        idxs.append(ix)
        logits = jnp.where(erow == ix, -jnp.inf, logits)
    es = [jnp.exp(v - vals[0]) for v in vals]
    tot = es[0]
    for e in es[1:]:
        tot = tot + e
    idx_ref[...] = jnp.concatenate(idxs, axis=0).astype(I32)
    wts_ref[...] = jnp.concatenate([e / tot for e in es], axis=0)


def _merge(o, zz, x, hist0, hist1, conv_w, woa, woc, wout, g_ffn, wr_t, br, tm, seq_rows, seq_len,
           hn_rows, hn_row0, hn_prev):
    t = x.shape[0]
    d = D_MODEL
    assert hn_row0 % tm == 0
    extra_specs, extra_args, aliases = [], [], {}
    if hn_prev is not None:
        extra_specs, extra_args, aliases = [pl.BlockSpec(memory_space=pl.ANY)], [hn_prev], {_MERGE_INPUTS: 1}
    full = lambda shape: pl.BlockSpec(shape, lambda i: (0,) * len(shape))
    if seq_rows is None:
        tiles_per_seq = seq_len // tm
        hist_spec = pl.BlockSpec((8, CONV_DIM), lambda i: (jnp.maximum(i * (tm // 8) - 1, 0), ZC_U // CONV_DIM))
        h0_spec = h1_spec = hist_spec
        h0_arg = h1_arg = zz
    else:
        tiles_per_seq = None
        h0_spec = h1_spec = pl.BlockSpec((tm, CONV_DIM), lambda i: (i, 0))
        h0_arg, h1_arg = hist0, hist1
    kern = functools.partial(_merge_kernel, tm=tm, seq_rows=seq_rows, tiles_per_seq=tiles_per_seq)
    vmem = (2 * tm * (ATTN_DIM * 2 + 5 * d * 4 + 4 * CONV_DIM * 4) + 2 * 2 * (2 * ATTN_DIM * d + d * d)
            + 10 * tm * d * 4 + (8 << 20))
    return pl.pallas_call(
        kern,
        grid=(t // tm,),
        in_specs=[
            pl.BlockSpec((tm, ATTN_DIM), lambda i: (i, 0)),
            pl.BlockSpec((tm, d), lambda i: (i, ZC_GA // d)),
            pl.BlockSpec((tm, d), lambda i: (i, ZC_GC // d)),
            pl.BlockSpec((tm, CONV_DIM), lambda i: (i, ZC_CB // CONV_DIM)),
            pl.BlockSpec((tm, CONV_DIM), lambda i: (i, ZC_U // CONV_DIM)),
            h0_spec, h1_spec,
            pl.BlockSpec((tm, d), lambda i: (i, 0)),
            full((3, CONV_DIM)),
            full((ATTN_DIM, d)), full((CONV_DIM, d)), full((d, d)),
            full((1, d)), full((N_EXPERTS, d)), full((N_EXPERTS, 1)),
        ] + extra_specs,
        out_specs=[
            pl.BlockSpec((tm, d), lambda i: (i, 0)),
            pl.BlockSpec((tm, d), lambda i: (hn_row0 // tm + i, 0)),
            pl.BlockSpec((TOP_K, tm), lambda i: (0, i)),
            pl.BlockSpec((TOP_K, tm), lambda i: (0, i)),
        ],
        out_shape=[
            jax.ShapeDtypeStruct((t, d), F32),
            jax.ShapeDtypeStruct((hn_rows, d), F32),
            jax.ShapeDtypeStruct((TOP_K, t), I32),
            jax.ShapeDtypeStruct((TOP_K, t), F32),
        ],
        input_output_aliases=aliases,
        compiler_params=pltpu.CompilerParams(dimension_semantics=("arbitrary",), vmem_limit_bytes=_vmem_limit(vmem)),
        name="merge_prompt" if seq_rows is None else "merge_sample",
    )(o, zz, zz, zz, zz, h0_arg, h1_arg, x, conv_w, woa, woc, wout, g_ffn.reshape(1, d), wr_t,
      br.reshape(N_EXPERTS, 1), *extra_args)


ROUTE_TILE = 384


def _route_kernel(idx_ref, dest_ref, cnt_ref, cnt_acc, carry, pst, tri_ref):
    p = pl.program_id(0)
    j = pl.program_id(1)
    tr = idx_ref.shape[1]
    idx = idx_ref[...]
    erow = lax.broadcasted_iota(I32, (N_EXPERTS, tr), 0)
    masks = [(erow == idx[k:k + 1, :]).astype(F32) for k in range(TOP_K)]
    mtot = masks[0]
    for m in masks[1:]:
        mtot = mtot + m
    tile_cnt = jnp.sum(mtot, axis=1, keepdims=True)

    @pl.when((p == 0) & (j == 0))
    def _():
        cnt_acc[...] = jnp.zeros_like(cnt_acc)

    @pl.when(p == 0)
    def _():
        cnt_acc[...] += jnp.broadcast_to(tile_cnt, cnt_acc.shape)

    @pl.when((p == 1) & (j == 0))
    def _():
        cnt = cnt_acc[:, 0:1]
        padded = jnp.floor((cnt + float(ROW_BLK - 1)) / float(ROW_BLK)) * float(ROW_BLK)
        r = lax.broadcasted_iota(I32, (N_EXPERTS, N_EXPERTS), 0)
        c = lax.broadcasted_iota(I32, (N_EXPERTS, N_EXPERTS), 1)
        rowv = jnp.sum(jnp.where(r == c, padded, 0.0), axis=0, keepdims=True)
        start = jnp.sum(jnp.where(c < r, rowv, 0.0), axis=1, keepdims=True)
        pst[...] = jnp.broadcast_to(start, pst.shape)
        carry[...] = jnp.zeros_like(carry)
        a = lax.broadcasted_iota(I32, (tr, tr), 0)
        b = lax.broadcasted_iota(I32, (tr, tr), 1)
        tri_ref[...] = (a < b).astype(BF16)

    @pl.when(p == 1)
    def _():
        excl = jnp.dot(mtot.astype(BF16), tri_ref[...], preferred_element_type=F32)
        val = pst[:, 0:1] + carry[:, 0:1] + excl
        dest = jnp.concatenate([jnp.sum(m * val, axis=0, keepdims=True) for m in masks], axis=0)
        dest_ref[...] = dest.astype(I32)
        carry[...] += jnp.broadcast_to(tile_cnt, carry.shape)
        cnt_ref[...] = cnt_acc[...].astype(I32)


def _route(idx_t):
    t = idx_t.shape[1]
    tr = ROUTE_TILE
    return pl.pallas_call(
        _route_kernel,
        grid=(2, t // tr),
        in_specs=[pl.BlockSpec((TOP_K, tr), lambda p, j: (0, j))],
        out_specs=[
            pl.BlockSpec((TOP_K, tr), lambda p, j: (0, j * p)),
            pl.BlockSpec((N_EXPERTS, LANES), lambda p, j: (0, 0)),
        ],
        out_shape=[jax.ShapeDtypeStruct((TOP_K, t), I32), jax.ShapeDtypeStruct((N_EXPERTS, LANES), I32)],
        scratch_shapes=[pltpu.VMEM((N_EXPERTS, LANES), F32), pltpu.VMEM((N_EXPERTS, LANES), F32),
                        pltpu.VMEM((N_EXPERTS, LANES), F32), pltpu.VMEM((tr, tr), BF16)],
        compiler_params=pltpu.CompilerParams(dimension_semantics=("arbitrary", "arbitrary")),
        name="route",
    )(idx_t)


DISPATCH_TILE = 384
DMA_UNROLL = 8


def _dispatch_kernel(dest_ref, hn_ref, xs_hbm, sem):
    tt = dest_ref.shape[1]

    def issue(t, c):
        for k in range(TOP_K):
            pltpu.make_async_copy(hn_ref.at[pl.ds(t, 1)], xs_hbm.at[pl.ds(dest_ref[k, t], 1)], sem).start()
        return c

    lax.fori_loop(0, tt, issue, 0, unroll=DMA_UNROLL)

    def drain(t, c):
        for k in range(TOP_K):
            pltpu.make_async_copy(hn_ref.at[pl.ds(0, 1)], xs_hbm.at[pl.ds(0, 1)], sem).wait()
        return c

    lax.fori_loop(0, tt, drain, 0, unroll=DMA_UNROLL)


def _dispatch(dest, hn, n_slots):
    t, d = hn.shape
    tt = DISPATCH_TILE
    return pl.pallas_call(
        _dispatch_kernel,
        grid=(t // tt,),
        in_specs=[pl.BlockSpec((TOP_K, tt), lambda i: (0, i), memory_space=pltpu.SMEM),
                  pl.BlockSpec((tt, d), lambda i: (i, 0))],
        out_specs=pl.BlockSpec(memory_space=pl.ANY),
        out_shape=jax.ShapeDtypeStruct((n_slots, d), hn.dtype),
        scratch_shapes=[pltpu.SemaphoreType.DMA(())],
        compiler_params=pltpu.CompilerParams(dimension_semantics=("arbitrary",)),
        name="dispatch",
    )(dest, hn)


def _expert_kernel(we_ref, ws_ref, wnb_ref, wvalid_ref, wact_ref,
                   xs_hbm, wg_ref, wu_ref, wd_ref, bg_ref, bu_ref, bd_ref, ys_hbm,
                   x_sb, acc, wgu_bf, wd_bf, stage, sem_in, sem_out):
    del we_ref, wact_ref
    w = pl.program_id(0)
    f = pl.program_id(1)
    n_items = pl.num_programs(0)
    last_f = N_FF_TILES - 1
    slot = w % 2
    nb = wnb_ref[w]
    row = lax.broadcasted_iota(I32, (ROW_BLK, 1), 0)

    def x_copy(item, j):
        src = xs_hbm.at[pl.ds(pl.multiple_of(ws_ref[item] + j * ROW_BLK, ROW_BLK), ROW_BLK)]
        return pltpu.make_async_copy(src, stage, sem_in)

    def x_cast(item, j, sl):
        r0 = pl.multiple_of(j * ROW_BLK, ROW_BLK)
        x_sb[sl, pl.ds(r0, ROW_BLK), :] = jnp.where(row + r0 < wvalid_ref[item], stage[...], 0.0).astype(BF16)

    def y_copy(item, j, sl):
        r0 = pl.multiple_of(j * ROW_BLK, ROW_BLK)
        dst = ys_hbm.at[pl.ds(pl.multiple_of(ws_ref[item] + r0, ROW_BLK), ROW_BLK)]
        return pltpu.make_async_copy(acc.at[sl, pl.ds(r0, ROW_BLK)], dst, sem_out.at[sl])

    def y_wait_all(item, sl):
        def body(j, c):
            y_copy(item, j, sl).wait()
            return c

        lax.fori_loop(0, wnb_ref[item], body, 0)

    @pl.when((w == 0) & (f == 0))
    def _first_rows():
        def body(j, c):
            cp = x_copy(0, j)
            cp.start()
            cp.wait()
            x_cast(0, j, 0)
            return c

        lax.fori_loop(0, wnb_ref[0], body, 0)

    nxt = jnp.minimum(w + 1, n_items - 1)
    prefetch = (w + 1 < n_items) & (f < wnb_ref[nxt])

    @pl.when(prefetch)
    def _():
        x_copy(nxt, f).start()

    @pl.when((f == 0) & (w >= 2))
    def _reclaim_acc():
        y_wait_all(jnp.maximum(w - 2, 0), slot)

    @pl.when(nb > 0)
    def _compute():
        wgu_bf[:, :FF_TILE] = wg_ref[0].astype(BF16)
        wgu_bf[:, FF_TILE:] = wu_ref[0].astype(BF16)
        wd_bf[...] = wd_ref[0].astype(BF16)
        bg = bg_ref[0]
        bu = bu_ref[0]
        bd = bd_ref[0]

        def update(r0, rows, first):
            x = x_sb[slot, pl.ds(r0, rows), :]
            gu = jnp.dot(x, wgu_bf[...], preferred_element_type=F32)
            g = jnp.minimum(gu[:, :FF_TILE] + bg, SWIGLU_LIMIT)
            u = jnp.clip(gu[:, FF_TILE:] + bu, -SWIGLU_LIMIT, SWIGLU_LIMIT)
            hh = (u + 1.0) * (g * jax.nn.sigmoid(SWIGLU_ALPHA * g))
            part = jnp.dot(hh.astype(BF16), wd_bf[...], preferred_element_type=F32)
            if first:
                acc[slot, pl.ds(r0, rows), :] = part + bd
            else:
                acc[slot, pl.ds(r0, rows), :] += part

        def sweep(first):
            def body(p, c):
                update(pl.multiple_of(p * (2 * ROW_BLK), 2 * ROW_BLK), 2 * ROW_BLK, first)
                return c

            lax.fori_loop(0, nb // 2, body, 0)

            @pl.when(nb % 2 == 1)
            def _():
                update(pl.multiple_of((nb - 1) * ROW_BLK, ROW_BLK), ROW_BLK, first)

        @pl.when(f == 0)
        def _():
            sweep(True)

        @pl.when(f > 0)
        def _():
            sweep(False)

    @pl.when((f == last_f) & (nb > 0))
    def _store():
        def body(j, c):
            y_copy(w, j, slot).start()
            return c

        lax.fori_loop(0, nb, body, 0)

    @pl.when(prefetch)
    def _():
        x_copy(nxt, f).wait()
        x_cast(nxt, f, 1 - slot)

    @pl.when((w == n_items - 1) & (f == last_f))
    def _drain():
        y_wait_all(jnp.maximum(w - 1, 0), 1 - slot)
        y_wait_all(w, slot)


def _experts(xs, tables, w_gate_up, b_gate_up, w_down, b_down):
    n_slots, d = xs.shape
    n_items = tables[0].shape[0]
    tf = FF_TILE
    nf = N_FF_TILES

    def ff(f, act, w):
        return f * act[w] + (nf - 1) * (1 - act[w])

    grid_spec = pltpu.PrefetchScalarGridSpec(
        num_scalar_prefetch=5,
        grid=(n_items, nf),
        in_specs=[
            pl.BlockSpec(memory_space=pl.ANY),
            pl.BlockSpec((1, d, tf), lambda w, f, we, ws, wnb, wv, act: (we[w], 0, ff(f, act, w))),
            pl.BlockSpec((1, d, tf), lambda w, f, we, ws, wnb, wv, act: (we[w], 0, nf + ff(f, act, w))),
            pl.BlockSpec((1, tf, d), lambda w, f, we, ws, wnb, wv, act: (we[w], ff(f, act, w), 0)),
            pl.BlockSpec((1, 1, tf), lambda w, f, we, ws, wnb, wv, act: (we[w], 0, ff(f, act, w))),
            pl.BlockSpec((1, 1, tf), lambda w, f, we, ws, wnb, wv, act: (we[w], 0, nf + ff(f, act, w))),
            pl.BlockSpec((1, 1, d), lambda w, f, we, ws, wnb, wv, act: (we[w], 0, 0)),
        ],
        out_specs=pl.BlockSpec(memory_space=pl.ANY),
        scratch_shapes=[
            pltpu.VMEM((2, SUPER_ROWS, d), BF16),
            pltpu.VMEM((2, SUPER_ROWS, d), F32),
            pltpu.VMEM((d, 2 * tf), BF16), pltpu.VMEM((tf, d), BF16),
            pltpu.VMEM((ROW_BLK, d), F32),
            pltpu.SemaphoreType.DMA(()), pltpu.SemaphoreType.DMA((2,)),
        ],
    )
    vmem = (2 * SUPER_ROWS * d * 6 + 2 * 3 * d * tf * 4 + 3 * d * tf * 2 + ROW_BLK * d * 4
            + 6 * ROW_BLK * d * 4 + (6 << 20))
    return pl.pallas_call(
        _expert_kernel,
        grid_spec=grid_spec,
        out_shape=jax.ShapeDtypeStruct((n_slots, d), F32),
        compiler_params=pltpu.CompilerParams(
            dimension_semantics=("arbitrary", "arbitrary"), vmem_limit_bytes=_vmem_limit(vmem)),
        name="experts",
    )(*tables, xs, w_gate_up, w_gate_up, w_down, b_gate_up.reshape(N_EXPERTS, 1, 2 * D_FF),
      b_gate_up.reshape(N_EXPERTS, 1, 2 * D_FF), b_down.reshape(N_EXPERTS, 1, d))


def _work_tables(counts, n_items):
    padded = (counts + ROW_BLK - 1) // ROW_BLK * ROW_BLK
    pstart = jnp.cumsum(padded) - padded
    items_e = (padded + SUPER_ROWS - 1) // SUPER_ROWS
    item_end = jnp.cumsum(items_e)
    item_start = item_end - items_e
    w = jnp.arange(n_items, dtype=I32)
    total = item_end[-1]
    active = (w < total).astype(I32)
    wl = jnp.minimum(w, total - 1)
    e = jnp.minimum(jnp.searchsorted(item_end, wl, side='right'), N_EXPERTS - 1).astype(I32)
    local = wl - item_start[e]
    start = pstart[e] + local * SUPER_ROWS
    rows = jnp.minimum(SUPER_ROWS, padded[e] - local * SUPER_ROWS)
    valid = jnp.clip(counts[e] - local * SUPER_ROWS, 0, rows)
    nb = (rows // ROW_BLK) * active
    return (e.astype(I32), start.astype(I32), nb.astype(I32), valid.astype(I32), active)


COMBINE_TILE = 128


def _combine_kernel(dest_ref, dest_next_ref, wts_ref, h_ref, g_ref, ys_hbm, out_ref, buf, sem):
    i = pl.program_id(0)
    n = pl.num_programs(0)
    slot = i % 2
    tt = dest_ref.shape[1]

    def issue(dref, sl):
        def body(t, c):
            for k in range(TOP_K):
                pltpu.make_async_copy(ys_hbm.at[pl.ds(dref[k, t], 1)], buf.at[sl, k, pl.ds(t, 1)], sem.at[sl]).start()
            return c

        lax.fori_loop(0, tt, body, 0, unroll=DMA_UNROLL)

    @pl.when(i == 0)
    def _():
        issue(dest_ref, 0)

    @pl.when(i + 1 < n)
    def _():
        issue(dest_next_ref, 1 - slot)

    def drain(t, c):
        for k in range(TOP_K):
            pltpu.make_async_copy(ys_hbm.at[pl.ds(0, 1)], buf.at[slot, k, pl.ds(0, 1)], sem.at[slot]).wait()
        return c

    lax.fori_loop(0, tt, drain, 0, unroll=DMA_UNROLL)
    wts = wts_ref[...]
    moe = wts[:, 0:1] * buf[slot, 0]
    for k in range(1, TOP_K):
        moe = moe + wts[:, k:k + 1] * buf[slot, k]
    y = h_ref[...] + moe
    ms = jnp.mean(y * y, axis=-1, keepdims=True)
    out_ref[...] = (y * lax.rsqrt(ms + NORM_EPS)) * g_ref[...]


def _combine(dest, wts, h, g_final, ys):
    t, d = h.shape
    tt = COMBINE_TILE
    n = t // tt
    return pl.pallas_call(
        _combine_kernel,
        grid=(n,),
        in_specs=[
            pl.BlockSpec((TOP_K, tt), lambda i: (0, i), memory_space=pltpu.SMEM),
            pl.BlockSpec((TOP_K, tt), lambda i: (0, jnp.minimum(i + 1, n - 1)), memory_space=pltpu.SMEM),
            pl.BlockSpec((tt, TOP_K), lambda i: (i, 0)),
            pl.BlockSpec((tt, d), lambda i: (i, 0)),
            pl.BlockSpec((1, d), lambda i: (0, 0)),
            pl.BlockSpec(memory_space=pl.ANY),
        ],
        out_specs=pl.BlockSpec((tt, d), lambda i: (i, 0)),
        out_shape=jax.ShapeDtypeStruct((t, d), F32),
        scratch_shapes=[pltpu.VMEM((2, TOP_K, tt, d), F32), pltpu.SemaphoreType.DMA((2,))],
        compiler_params=pltpu.CompilerParams(dimension_semantics=("arbitrary",)),
        name="combine",
    )(dest, dest, wts, h, g_final.reshape(1, d), ys)


def kernel(x_prompt, x_sample, cache_attn_k, cache_attn_v, state_conv, rel_bias_table, norm_mix_g, w_in, b_gate,
           attn_sinks, conv_w, w_o_attn, w_o_conv, w_out, norm_ffn_g, w_router, b_router, w_gate_up, b_gate_up,
           w_down, b_down, norm_final_g):
    batch, seq, d = x_prompt.shape
    dbatch, dseq, _ = x_sample.shape
    assert norm_mix_g.shape[0] == 1 and d == D_MODEL
    t_p, t_s = batch * seq, dbatch * dseq
    xp = x_prompt.reshape(t_p, d)
    xs = x_sample.reshape(t_s, d)

    w_in_bf = w_in[0].astype(BF16)
    woa = w_o_attn[0].astype(BF16)
    woc = w_o_conv[0].astype(BF16)
    wout = w_out[0].astype(BF16)
    wr_t = w_router[0].T.astype(BF16)

    zz_p = _in_proj(xp, norm_mix_g[0], w_in_bf, b_gate[0], tm=1024)
    zz_s = _in_proj(xs, norm_mix_g[0], w_in_bf, b_gate[0], tm=t_s)

    o_p = _attn_prompt(zz_p, batch, seq, rel_bias_table, attn_sinks[0])
    k_new = zz_s[:, ZC_K:ZC_K + KV_DIM].reshape(dbatch, dseq, KV_DIM)
    v_new = zz_s[:, ZC_V:ZC_V + KV_DIM].reshape(dbatch, dseq, KV_DIM)
    k_ext = jnp.concatenate([cache_attn_k[0].reshape(dbatch, WINDOW, KV_DIM), k_new], axis=1)
    v_ext = jnp.concatenate([cache_attn_v[0].reshape(dbatch, WINDOW, KV_DIM), v_new], axis=1)
    o_s = _attn_sample(zz_s, k_ext, v_ext, dbatch, dseq, rel_bias_table, attn_sinks[0])

    hist0 = jnp.repeat(state_conv[0][:, 0, :], dseq, axis=0)
    hist1 = jnp.repeat(state_conv[0][:, 1, :], dseq, axis=0)
    merge_args = (conv_w[0], woa, woc, wout, norm_ffn_g[0], wr_t, b_router[0])
    t_all = t_p + t_s
    h_p, hn, idx_p, wts_p = _merge(o_p, zz_p, xp, None, None, *merge_args, tm=256, seq_rows=None, seq_len=seq,
                                   hn_rows=t_all, hn_row0=0, hn_prev=None)
    h_s, hn, idx_s, wts_s = _merge(o_s, zz_s, xs, hist0, hist1, *merge_args, tm=t_s, seq_rows=dseq, seq_len=dseq,
                                   hn_rows=t_all, hn_row0=t_p, hn_prev=hn)

    idx_t = jnp.concatenate([idx_p, idx_s], axis=1)
    wts = jnp.concatenate([wts_p, wts_s], axis=1).T
    n_assign = t_all * TOP_K
    n_slots = (n_assign + N_EXPERTS * (ROW_BLK - 1) + ROW_BLK - 1) // ROW_BLK * ROW_BLK
    n_items = N_EXPERTS + n_slots // SUPER_ROWS

    dest, cnt = _route(idx_t)
    tables = _work_tables(cnt[:, 0], n_items)
    xs_sorted = _dispatch(dest, hn, n_slots)
    ys = _experts(xs_sorted, tables, w_gate_up[0], b_gate_up[0], w_down[0], b_down[0])
    y_p = _combine(dest[:, :t_p], wts[:t_p], h_p, norm_final_g, ys)
    y_s = _combine(dest[:, t_p:], wts[t_p:], h_s, norm_final_g, ys)

    zz_p3 = zz_p.reshape(batch, seq, Z_DIM)

    def kv_tail(c0):
        return zz_p3[:, seq - WINDOW:, c0:c0 + KV_DIM].reshape(batch, WINDOW, N_KV_HEADS, HEAD_DIM)[None]

    u_s = zz_s[:, ZC_U:ZC_U + CONV_DIM].reshape(dbatch, dseq, CONV_DIM)
    conv_hist = state_conv.shape[2]
    new_conv_p = zz_p3[:, seq - conv_hist:, ZC_U:ZC_U + CONV_DIM][None]
    new_conv_s = jnp.concatenate([state_conv[0], u_s], axis=1)[:, -conv_hist:][None]
    new_k_s = k_ext[:, -WINDOW:].reshape(dbatch, WINDOW, N_KV_HEADS, HEAD_DIM)[None]
    new_v_s = v_ext[:, -WINDOW:].reshape(dbatch, WINDOW, N_KV_HEADS, HEAD_DIM)[None]
    return (y_p.reshape(batch, seq, d), y_s.reshape(dbatch, dseq, d),
            kv_tail(ZC_K), kv_tail(ZC_V), new_conv_p, new_k_s, new_v_s, new_conv_s)
```

```python
import functools
import math

import numpy as np
import jax
import jax.numpy as jnp
from jax import lax
from jax.experimental import pallas as pl
from jax.experimental.pallas import tpu as pltpu

F32 = jnp.float32
BF16 = jnp.bfloat16
I32 = jnp.int32

D_MODEL = 2048
CHUNK = 64
HEAD_DIM = 64
N_Q_HEADS = 16
N_KV_HEADS = 4
GROUP = N_Q_HEADS // N_KV_HEADS
ATTN_DIM = N_Q_HEADS * HEAD_DIM
KV_DIM = N_KV_HEADS * HEAD_DIM
WINDOW = 128
CONV_DIM = D_MODEL // 2
N_BUCKETS = 32
MAX_DISTANCE = 128
N_EXPERTS = 32
TOP_K = 4
D_FF = D_MODEL
SWIGLU_LIMIT = 7.0
SWIGLU_ALPHA = 1.702
NORM_EPS = 1e-5
IN_DIM = ATTN_DIM + 2 * KV_DIM + 3 * CONV_DIM + 2 * D_MODEL

V7X_VMEM_BYTES = 64 * 1024 * 1024
LANES = 128

ZC_GA = 0
ZC_GC = D_MODEL
ZC_CB = 2 * D_MODEL
ZC_U = ZC_CB + CONV_DIM
ZC_Q = ZC_U + CONV_DIM
ZC_K = ZC_Q + ATTN_DIM
ZC_V = ZC_K + KV_DIM
Z_DIM = ZC_V + KV_DIM
PROJ_TN = 512

ROW_BLK = 256
SUPER_BLKS = 5
SUPER_ROWS = SUPER_BLKS * ROW_BLK
FF_TILE = 256
N_FF_TILES = D_FF // FF_TILE
assert SUPER_BLKS <= N_FF_TILES


def _vmem_limit(nbytes):
    return int(min(nbytes, V7X_VMEM_BYTES - 6 * 1024 * 1024))


_MODE_PLAIN, _MODE_STASH0, _MODE_STASH1, _MODE_MUL0, _MODE_MUL1, _MODE_GATE = range(6)


def _proj_schedule():
    t = PROJ_TN
    cc0 = (ATTN_DIM + 2 * KV_DIM + CONV_DIM) // t
    ch0 = (ATTN_DIM + 2 * KV_DIM + 2 * CONV_DIM) // t
    cb0 = (ATTN_DIM + 2 * KV_DIM) // t
    ga0 = (ATTN_DIM + 2 * KV_DIM + 3 * CONV_DIM) // t
    steps = [
        (cc0, ZC_U // t, 0, _MODE_STASH0),
        (cc0 + 1, ZC_U // t, 0, _MODE_STASH1),
        (ch0, ZC_U // t, 0, _MODE_MUL0),
        (ch0 + 1, ZC_U // t + 1, 0, _MODE_MUL1),
        (0, ZC_Q // t, 0, _MODE_PLAIN),
        (1, ZC_Q // t + 1, 0, _MODE_PLAIN),
        (2, ZC_K // t, 0, _MODE_PLAIN),
        (cb0, ZC_CB // t, 0, _MODE_PLAIN),
        (cb0 + 1, ZC_CB // t + 1, 0, _MODE_PLAIN),
    ]
    for j in range(2 * D_MODEL // t):
        steps.append((ga0 + j, j, j, _MODE_GATE))
    return np.asarray(steps, dtype=np.int32)


def _in_proj_kernel(wt_ref, ot_ref, bt_ref, md_ref, x_ref, g_ref, w_ref, b_ref, o_ref, xn_ref, stash_ref):
    del wt_ref, ot_ref, bt_ref
    n = pl.program_id(1)

    @pl.when(n == 0)
    def _():
        x = x_ref[...]
        ms = jnp.mean(x * x, axis=-1, keepdims=True)
        xn_ref[...] = ((x * lax.rsqrt(ms + NORM_EPS)) * g_ref[...]).astype(BF16)

    def product():
        return jnp.dot(xn_ref[...], w_ref[...], preferred_element_type=F32)

    md = md_ref[n]

    @pl.when(md == _MODE_PLAIN)
    def _():
        o_ref[...] = product()

    @pl.when(md == _MODE_STASH0)
    def _():
        stash_ref[0] = product()

    @pl.when(md == _MODE_STASH1)
    def _():
        stash_ref[1] = product()

    @pl.when(md == _MODE_MUL0)
    def _():
        o_ref[...] = stash_ref[0] * product()

    @pl.when(md == _MODE_MUL1)
    def _():
        o_ref[...] = stash_ref[1] * product()

    @pl.when(md == _MODE_GATE)
    def _():
        o_ref[...] = jax.nn.sigmoid(product() + b_ref[...])


def _in_proj(x, g_mix, w_in_bf, b_gate, tm):
    t, d = x.shape
    sched = _proj_schedule()
    n_steps = sched.shape[0]
    grid_spec = pltpu.PrefetchScalarGridSpec(
        num_scalar_prefetch=4,
        grid=(t // tm, n_steps),
        in_specs=[
            pl.BlockSpec((tm, d), lambda m, n, wt, ot, bt, md: (m, 0)),
            pl.BlockSpec((1, d), lambda m, n, wt, ot, bt, md: (0, 0)),
            pl.BlockSpec((d, PROJ_TN), lambda m, n, wt, ot, bt, md: (0, wt[n])),
            pl.BlockSpec((1, PROJ_TN), lambda m, n, wt, ot, bt, md: (0, bt[n])),
        ],
        out_specs=pl.BlockSpec((tm, PROJ_TN), lambda m, n, wt, ot, bt, md: (m, ot[n])),
        scratch_shapes=[pltpu.VMEM((tm, d), BF16), pltpu.VMEM((2, tm, PROJ_TN), F32)],
    )
    vmem = 2 * tm * d * 4 + tm * d * 2 + 2 * d * PROJ_TN * 2 + 6 * tm * PROJ_TN * 4 + (8 << 20)
    return pl.pallas_call(
        _in_proj_kernel,
        grid_spec=grid_spec,
        out_shape=jax.ShapeDtypeStruct((t, Z_DIM), F32),
        compiler_params=pltpu.CompilerParams(
            dimension_semantics=("arbitrary", "arbitrary"), vmem_limit_bytes=_vmem_limit(vmem)),
        name="in_proj",
    )(jnp.asarray(sched[:, 0]), jnp.asarray(sched[:, 1]), jnp.asarray(sched[:, 2]), jnp.asarray(sched[:, 3]),
      x, g_mix.reshape(1, d), w_in_bf, b_gate.reshape(1, 2 * D_MODEL))


def _t5_buckets(n_q, n_k):
    half = N_BUCKETS // 2
    max_exact = half // 2
    rel = (np.arange(n_k, dtype=np.int32)[None, :] - WINDOW) - np.arange(n_q, dtype=np.int32)[:, None]
    ret = np.where(rel > 0, half, 0)
    n = np.abs(rel)
    nf = np.maximum(n, 1).astype(np.float32)
    large = max_exact + (np.log(nf / np.float32(max_exact)) / np.float32(math.log(MAX_DISTANCE / max_exact))
                         * np.float32(half - max_exact)).astype(np.int32)
    large = np.minimum(large, half - 1)
    return (ret + np.where(n < max_exact, n, large)).astype(np.int32)


def _build_bias(bias_ref, bucket_ref, table_ref, n_q):
    bucket = bucket_ref[...]
    for kvh in range(N_KV_HEADS):
        for g in range(GROUP):
            h = kvh * GROUP + g
            mat = jnp.zeros(bucket.shape, F32)
            for b in range(N_BUCKETS):
                mat = jnp.where(bucket == b, table_ref[b, h], mat)
            bias_ref[kvh, g * n_q:(g + 1) * n_q, :] = mat


def _attend(q, kwin, vwin, bias_ref, sinks_ref, n_q, invalid):
    outs = []
    grp = lax.broadcasted_iota(I32, (GROUP * n_q, 1), 0) // n_q
    for kvh in range(N_KV_HEADS):
        c0 = kvh * GROUP * HEAD_DIM
        qh = jnp.concatenate([q[:, c0 + g * HEAD_DIM:c0 + (g + 1) * HEAD_DIM] for g in range(GROUP)], axis=0)
        qh = (qh * (HEAD_DIM ** -0.5)).astype(BF16)
        kh = kwin[:, kvh * HEAD_DIM:(kvh + 1) * HEAD_DIM].astype(BF16)
        vh = vwin[:, kvh * HEAD_DIM:(kvh + 1) * HEAD_DIM].astype(BF16)
        s = lax.dot_general(qh, kh, (((1,), (1,)), ((), ())), preferred_element_type=F32) + bias_ref[kvh]
        if invalid is not None:
            s = jnp.where(invalid, -1e30, s)
        sink = jnp.zeros((GROUP * n_q, 1), F32)
        for g in range(GROUP):
            sink = jnp.where(grp == g, sinks_ref[kvh * GROUP + g], sink)
        m = jnp.maximum(jnp.max(s, axis=-1, keepdims=True), sink)
        p = jnp.exp(s - m)
        denom = jnp.sum(p, axis=-1, keepdims=True) + jnp.exp(sink - m)
        o = jnp.dot(p.astype(BF16), vh, preferred_element_type=F32) / denom
        outs.append(o)
    return outs


def _store_heads(o_ref, r0, n_q, outs):
    for kvh, o in enumerate(outs):
        for g in range(GROUP):
            c = (kvh * GROUP + g) * HEAD_DIM
            o_ref[r0:r0 + n_q, c:c + HEAD_DIM] = o[g * n_q:(g + 1) * n_q].astype(o_ref.dtype)


def _attn_prompt_kernel(table_ref, sinks_ref, bucket_ref, q_ref, kp_ref, kc_ref, vp_ref, vc_ref, o_ref, bias_ref):
    first = (pl.program_id(0) == 0) & (pl.program_id(1) == 0)

    @pl.when(first)
    def _():
        _build_bias(bias_ref, bucket_ref, table_ref, CHUNK)

    kwin = jnp.concatenate([kp_ref[...], kc_ref[...]], axis=0)
    vwin = jnp.concatenate([vp_ref[...], vc_ref[...]], axis=0)
    n_k = WINDOW + CHUNK
    seq_start = pl.program_id(1) == 0
    col = lax.broadcasted_iota(I32, (1, n_k), 1)
    for c in range(WINDOW // CHUNK):
        r0 = c * CHUNK
        invalid = seq_start & (col + r0 < WINDOW)
        outs = _attend(q_ref[r0:r0 + CHUNK, :], kwin[r0:r0 + n_k], vwin[r0:r0 + n_k], bias_ref, sinks_ref,
                       CHUNK, invalid)
        _store_heads(o_ref, r0, CHUNK, outs)


def _attn_prompt(zz, batch, seq, table, sinks):
    qb = WINDOW
    nb = seq // qb
    n_k = WINDOW + CHUNK
    bucket = jnp.asarray(_t5_buckets(CHUNK, n_k))
    smem = pl.BlockSpec(memory_space=pltpu.SMEM)
    kblk, vblk = ZC_K // KV_DIM, ZC_V // KV_DIM
    return pl.pallas_call(
        _attn_prompt_kernel,
        grid=(batch, nb),
        in_specs=[
            smem, smem,
            pl.BlockSpec((CHUNK, n_k), lambda b, j: (0, 0)),
            pl.BlockSpec((qb, ATTN_DIM), lambda b, j: (b * nb + j, ZC_Q // ATTN_DIM)),
            pl.BlockSpec((qb, KV_DIM), lambda b, j: (b * nb + jnp.maximum(j - 1, 0), kblk)),
            pl.BlockSpec((qb, KV_DIM), lambda b, j: (b * nb + j, kblk)),
            pl.BlockSpec((qb, KV_DIM), lambda b, j: (b * nb + jnp.maximum(j - 1, 0), vblk)),
            pl.BlockSpec((qb, KV_DIM), lambda b, j: (b * nb + j, vblk)),
        ],
        out_specs=pl.BlockSpec((qb, ATTN_DIM), lambda b, j: (b * nb + j, 0)),
        out_shape=jax.ShapeDtypeStruct((batch * seq, ATTN_DIM), BF16),
        scratch_shapes=[pltpu.VMEM((N_KV_HEADS, GROUP * CHUNK, n_k), F32)],
        compiler_params=pltpu.CompilerParams(dimension_semantics=("arbitrary", "arbitrary")),
        name="attn_prompt",
    )(table, sinks, bucket, zz, zz, zz, zz, zz)


def _attn_sample_kernel(table_ref, sinks_ref, bucket_ref, q_ref, k_ref, v_ref, o_ref, bias_ref, *, n_q):
    @pl.when(pl.program_id(0) == 0)
    def _():
        _build_bias(bias_ref, bucket_ref, table_ref, n_q)

    outs = _attend(q_ref[...], k_ref[0], v_ref[0], bias_ref, sinks_ref, n_q, None)
    _store_heads(o_ref, 0, n_q, outs)


def _attn_sample(zz, k_ext, v_ext, batch, n_q, table, sinks):
    n_k = WINDOW + n_q
    bucket = jnp.asarray(_t5_buckets(n_q, n_k))
    smem = pl.BlockSpec(memory_space=pltpu.SMEM)
    return pl.pallas_call(
        functools.partial(_attn_sample_kernel, n_q=n_q),
        grid=(batch,),
        in_specs=[
            smem, smem,
            pl.BlockSpec((n_q, n_k), lambda b: (0, 0)),
            pl.BlockSpec((n_q, ATTN_DIM), lambda b: (b, ZC_Q // ATTN_DIM)),
            pl.BlockSpec((1, n_k, KV_DIM), lambda b: (b, 0, 0)),
            pl.BlockSpec((1, n_k, KV_DIM), lambda b: (b, 0, 0)),
        ],
        out_specs=pl.BlockSpec((n_q, ATTN_DIM), lambda b: (b, 0)),
        out_shape=jax.ShapeDtypeStruct((batch * n_q, ATTN_DIM), BF16),
        scratch_shapes=[pltpu.VMEM((N_KV_HEADS, GROUP * n_q, n_k), F32)],
        compiler_params=pltpu.CompilerParams(dimension_semantics=("arbitrary",)),
        name="attn_sample",
    )(table, sinks, bucket, zz, k_ext, v_ext)


_MERGE_INPUTS = 15


def _merge_kernel(*refs, tm, seq_rows, tiles_per_seq):
    (o_ref, ga_ref, gc_ref, cb_ref, u_ref, h0_ref, h1_ref, x_ref, cw_ref, woa_ref, woc_ref, wout_ref,
     gf_ref, wr_ref, br_ref) = refs[:_MERGE_INPUTS]
    h_ref, hn_ref, idx_ref, wts_ref = refs[-4:]
    u = u_ref[...]
    row = lax.broadcasted_iota(I32, (tm, 1), 0)
    if seq_rows is None:
        at_start = (pl.program_id(0) % tiles_per_seq) == 0
        hist0 = jnp.where(at_start, 0.0, h0_ref[6:7, :])
        hist1 = jnp.where(at_start, 0.0, h1_ref[7:8, :])
        pos = row
    else:
        hist0 = h0_ref[...]
        hist1 = h1_ref[...]
        pos = row % seq_rows
    r1 = pltpu.roll(u, 1, 0)
    r2 = pltpu.roll(u, 2, 0)
    u1 = jnp.where(pos == 0, hist1, r1)
    u2 = jnp.where(pos == 0, hist0, jnp.where(pos == 1, hist1, r2))
    cw = cw_ref[...]
    y = cw[0:1, :] * u2
    y = y + cw[1:2, :] * u1
    y = y + cw[2:3, :] * u
    c = (cb_ref[...] * y).astype(BF16)
    t = ga_ref[...] * jnp.dot(o_ref[...], woa_ref[...], preferred_element_type=F32)
    t = t + gc_ref[...] * jnp.dot(c, woc_ref[...], preferred_element_type=F32)
    h = x_ref[...] + jnp.dot(t.astype(BF16), wout_ref[...], preferred_element_type=F32)
    h_ref[...] = h
    ms = jnp.mean(h * h, axis=-1, keepdims=True)
    hn = (h * lax.rsqrt(ms + NORM_EPS)) * gf_ref[...]
    hn_ref[...] = hn
    logits = lax.dot_general(wr_ref[...], hn.astype(BF16), (((1,), (1,)), ((), ())),
                             preferred_element_type=F32) + br_ref[...]
    erow = lax.broadcasted_iota(I32, logits.shape, 0).astype(F32)
    vals, idxs = [], []
    for _ in range(TOP_K):
        mx = jnp.max(logits, axis=0, keepdims=True)
        ix = jnp.min(jnp.where(logits == mx, erow, float(N_EXPERTS)), axis=0, keepdims=True)
        vals.append(mx)
        idxs.append(ix)
        logits = jnp.where(erow == ix, -jnp.inf, logits)
    es = [jnp.exp(v - vals[0]) for v in vals]
    tot = es[0]
    for e in es[1:]:
        tot = tot + e
    idx_ref[...] = jnp.concatenate(idxs, axis=0).astype(I32)
    wts_ref[...] = jnp.concatenate([e / tot for e in es], axis=0)


def _merge(o, zz, x, hist0, hist1, conv_w, woa, woc, wout, g_ffn, wr_t, br, tm, seq_rows, seq_len,
           hn_rows, hn_row0, hn_prev):
    t = x.shape[0]
    d = D_MODEL
    assert hn_row0 % tm == 0
    extra_specs, extra_args, aliases = [], [], {}
    if hn_prev is not None:
        extra_specs, extra_args, aliases = [pl.BlockSpec(memory_space=pl.ANY)], [hn_prev], {_MERGE_INPUTS: 1}
    full = lambda shape: pl.BlockSpec(shape, lambda i: (0,) * len(shape))
    if seq_rows is None:
        tiles_per_seq = seq_len // tm
        hist_spec = pl.BlockSpec((8, CONV_DIM), lambda i: (jnp.maximum(i * (tm // 8) - 1, 0), ZC_U // CONV_DIM))
        h0_spec = h1_spec = hist_spec
        h0_arg = h1_arg = zz
    else:
        tiles_per_seq = None
        h0_spec = h1_spec = pl.BlockSpec((tm, CONV_DIM), lambda i: (i, 0))
        h0_arg, h1_arg = hist0, hist1
    kern = functools.partial(_merge_kernel, tm=tm, seq_rows=seq_rows, tiles_per_seq=tiles_per_seq)
    vmem = (2 * tm * (ATTN_DIM * 2 + 5 * d * 4 + 4 * CONV_DIM * 4) + 2 * 2 * (2 * ATTN_DIM * d + d * d)
            + 10 * tm * d * 4 + (8 << 20))
    return pl.pallas_call(
        kern,
        grid=(t // tm,),
        in_specs=[
            pl.BlockSpec((tm, ATTN_DIM), lambda i: (i, 0)),
            pl.BlockSpec((tm, d), lambda i: (i, ZC_GA // d)),
            pl.BlockSpec((tm, d), lambda i: (i, ZC_GC // d)),
            pl.BlockSpec((tm, CONV_DIM), lambda i: (i, ZC_CB // CONV_DIM)),
            pl.BlockSpec((tm, CONV_DIM), lambda i: (i, ZC_U // CONV_DIM)),
            h0_spec, h1_spec,
            pl.BlockSpec((tm, d), lambda i: (i, 0)),
            full((3, CONV_DIM)),
            full((ATTN_DIM, d)), full((CONV_DIM, d)), full((d, d)),
            full((1, d)), full((N_EXPERTS, d)), full((N_EXPERTS, 1)),
        ] + extra_specs,
        out_specs=[
            pl.BlockSpec((tm, d), lambda i: (i, 0)),
            pl.BlockSpec((tm, d), lambda i: (hn_row0 // tm + i, 0)),
            pl.BlockSpec((TOP_K, tm), lambda i: (0, i)),
            pl.BlockSpec((TOP_K, tm), lambda i: (0, i)),
        ],
        out_shape=[
            jax.ShapeDtypeStruct((t, d), F32),
            jax.ShapeDtypeStruct((hn_rows, d), F32),
            jax.ShapeDtypeStruct((TOP_K, t), I32),
            jax.ShapeDtypeStruct((TOP_K, t), F32),
        ],
        input_output_aliases=aliases,
        compiler_params=pltpu.CompilerParams(dimension_semantics=("arbitrary",), vmem_limit_bytes=_vmem_limit(vmem)),
        name="merge_prompt" if seq_rows is None else "merge_sample",
    )(o, zz, zz, zz, zz, h0_arg, h1_arg, x, conv_w, woa, woc, wout, g_ffn.reshape(1, d), wr_t,
      br.reshape(N_EXPERTS, 1), *extra_args)


ROUTE_TILE = 384


def _route_kernel(idx_ref, dest_ref, cnt_ref, cnt_acc, carry, pst, tri_ref):
    p = pl.program_id(0)
    j = pl.program_id(1)
    tr = idx_ref.shape[1]
    idx = idx_ref[...]
    erow = lax.broadcasted_iota(I32, (N_EXPERTS, tr), 0)
    masks = [(erow == idx[k:k + 1, :]).astype(F32) for k in range(TOP_K)]
    mtot = masks[0]
    for m in masks[1:]:
        mtot = mtot + m
    tile_cnt = jnp.sum(mtot, axis=1, keepdims=True)

    @pl.when((p == 0) & (j == 0))
    def _():
        cnt_acc[...] = jnp.zeros_like(cnt_acc)

    @pl.when(p == 0)
    def _():
        cnt_acc[...] += jnp.broadcast_to(tile_cnt, cnt_acc.shape)

    @pl.when((p == 1) & (j == 0))
    def _():
        cnt = cnt_acc[:, 0:1]
        padded = jnp.floor((cnt + float(ROW_BLK - 1)) / float(ROW_BLK)) * float(ROW_BLK)
        r = lax.broadcasted_iota(I32, (N_EXPERTS, N_EXPERTS), 0)
        c = lax.broadcasted_iota(I32, (N_EXPERTS, N_EXPERTS), 1)
        rowv = jnp.sum(jnp.where(r == c, padded, 0.0), axis=0, keepdims=True)
        start = jnp.sum(jnp.where(c < r, rowv, 0.0), axis=1, keepdims=True)
        pst[...] = jnp.broadcast_to(start, pst.shape)
        carry[...] = jnp.zeros_like(carry)
        a = lax.broadcasted_iota(I32, (tr, tr), 0)
        b = lax.broadcasted_iota(I32, (tr, tr), 1)
        tri_ref[...] = (a < b).astype(BF16)

    @pl.when(p == 1)
    def _():
        excl = jnp.dot(mtot.astype(BF16), tri_ref[...], preferred_element_type=F32)
        val = pst[:, 0:1] + carry[:, 0:1] + excl
        dest = jnp.concatenate([jnp.sum(m * val, axis=0, keepdims=True) for m in masks], axis=0)
        dest_ref[...] = dest.astype(I32)
        carry[...] += jnp.broadcast_to(tile_cnt, carry.shape)
        cnt_ref[...] = cnt_acc[...].astype(I32)


def _route(idx_t):
    t = idx_t.shape[1]
    tr = ROUTE_TILE
    return pl.pallas_call(
        _route_kernel,
        grid=(2, t // tr),
        in_specs=[pl.BlockSpec((TOP_K, tr), lambda p, j: (0, j))],
        out_specs=[
            pl.BlockSpec((TOP_K, tr), lambda p, j: (0, j * p)),
            pl.BlockSpec((N_EXPERTS, LANES), lambda p, j: (0, 0)),
        ],
        out_shape=[jax.ShapeDtypeStruct((TOP_K, t), I32), jax.ShapeDtypeStruct((N_EXPERTS, LANES), I32)],
        scratch_shapes=[pltpu.VMEM((N_EXPERTS, LANES), F32), pltpu.VMEM((N_EXPERTS, LANES), F32),
                        pltpu.VMEM((N_EXPERTS, LANES), F32), pltpu.VMEM((tr, tr), BF16)],
        compiler_params=pltpu.CompilerParams(dimension_semantics=("arbitrary", "arbitrary")),
        name="route",
    )(idx_t)


DISPATCH_TILE = 384
DMA_UNROLL = 8


def _dispatch_kernel(dest_ref, hn_ref, xs_hbm, sem):
    tt = hn_ref.shape[0]

    def issue(t, c):
        for k in range(TOP_K):
            pltpu.make_async_copy(hn_ref.at[pl.ds(t, 1)], xs_hbm.at[pl.ds(dest_ref[0, TOP_K * t + k], 1)], sem).start()
        return c

    lax.fori_loop(0, tt, issue, 0, unroll=DMA_UNROLL)

    def drain(t, c):
        for k in range(TOP_K):
            pltpu.make_async_copy(hn_ref.at[pl.ds(0, 1)], xs_hbm.at[pl.ds(0, 1)], sem).wait()
        return c

    lax.fori_loop(0, tt, drain, 0, unroll=DMA_UNROLL)


def _dispatch(dest, hn, n_slots):
    t, d = hn.shape
    tt = DISPATCH_TILE
    return pl.pallas_call(
        _dispatch_kernel,
        grid=(t // tt,),
        in_specs=[pl.BlockSpec((1, TOP_K * tt), lambda i: (0, i), memory_space=pltpu.SMEM),
                  pl.BlockSpec((tt, d), lambda i: (i, 0))],
        out_specs=pl.BlockSpec(memory_space=pl.ANY),
        out_shape=jax.ShapeDtypeStruct((n_slots, d), hn.dtype),
        scratch_shapes=[pltpu.SemaphoreType.DMA(())],
        compiler_params=pltpu.CompilerParams(dimension_semantics=("arbitrary",)),
        name="dispatch",
    )(dest, hn)


def _expert_kernel(we_ref, ws_ref, wnb_ref, wvalid_ref, wact_ref,
                   xs_hbm, wg_ref, wu_ref, wd_ref, bg_ref, bu_ref, bd_ref, ys_hbm,
                   x_sb, acc, wgu_bf, wd_bf, stage, sem_in, sem_out):
    del we_ref, wact_ref
    w = pl.program_id(0)
    f = pl.program_id(1)
    n_items = wnb_ref.shape[0]
    last_f = N_FF_TILES - 1
    slot = w % 2
    nb = wnb_ref[w]
    row = lax.broadcasted_iota(I32, (ROW_BLK, 1), 0)

    def x_copy(item, j):
        src = xs_hbm.at[pl.ds(pl.multiple_of(ws_ref[item] + j * ROW_BLK, ROW_BLK), ROW_BLK)]
        return pltpu.make_async_copy(src, stage, sem_in)

    def x_cast(item, j, sl):
        r0 = pl.multiple_of(j * ROW_BLK, ROW_BLK)
        x_sb[sl, pl.ds(r0, ROW_BLK), :] = jnp.where(row + r0 < wvalid_ref[item], stage[...], 0.0).astype(BF16)

    def y_copy(item, j, sl):
        r0 = pl.multiple_of(j * ROW_BLK, ROW_BLK)
        dst = ys_hbm.at[pl.ds(pl.multiple_of(ws_ref[item] + r0, ROW_BLK), ROW_BLK)]
        return pltpu.make_async_copy(acc.at[sl, pl.ds(r0, ROW_BLK)], dst, sem_out.at[sl])

    def y_wait_all(item, sl):
        def body(j, c):
            y_copy(item, j, sl).wait()
            return c

        lax.fori_loop(0, wnb_ref[item], body, 0)

    @pl.when((w == 0) & (f == 0))
    def _first_rows():
        def body(j, c):
            cp = x_copy(0, j)
            cp.start()
            cp.wait()
            x_cast(0, j, 0)
            return c

        lax.fori_loop(0, wnb_ref[0], body, 0)

    nxt = jnp.minimum(w + 1, n_items - 1)
    prefetch = (w + 1 < n_items) & (f < wnb_ref[nxt])

    @pl.when(prefetch)
    def _():
        x_copy(nxt, f).start()

    @pl.when((f == 0) & (w >= 2))
    def _reclaim_acc():
        y_wait_all(jnp.maximum(w - 2, 0), slot)

    @pl.when(nb > 0)
    def _compute():
        bg = bg_ref[0]
        bu = bu_ref[0]
        bd = bd_ref[0]

        def cast_weights():
            wgu_bf[:, :FF_TILE] = wg_ref[0].astype(BF16)
            wgu_bf[:, FF_TILE:] = wu_ref[0].astype(BF16)
            wd_bf[...] = wd_ref[0].astype(BF16)

        def update(r0, rows, first):
            x = x_sb[slot, pl.ds(r0, rows), :]
            gu = jnp.dot(x, wgu_bf[...], preferred_element_type=F32)
            g = jnp.minimum(gu[:, :FF_TILE] + bg, SWIGLU_LIMIT)
            u = jnp.clip(gu[:, FF_TILE:] + bu, -SWIGLU_LIMIT, SWIGLU_LIMIT)
            hh = (u + 1.0) * (g * jax.nn.sigmoid(SWIGLU_ALPHA * g))
            part = jnp.dot(hh.astype(BF16), wd_bf[...], preferred_element_type=F32)
            if first:
                acc[slot, pl.ds(r0, rows), :] = part + bd
            else:
                acc[slot, pl.ds(r0, rows), :] += part

        def sweep(first):
            def body(p, c):
                update(pl.multiple_of(p * (2 * ROW_BLK), 2 * ROW_BLK), 2 * ROW_BLK, first)
                return c

            @pl.when(nb >= 2)
            def _():
                cast_weights()
                update(0, 2 * ROW_BLK, first)
                lax.fori_loop(1, nb // 2, body, 0)

                @pl.when(nb % 2 == 1)
                def _():
                    update(pl.multiple_of((nb - 1) * ROW_BLK, ROW_BLK), ROW_BLK, first)

            @pl.when(nb == 1)
            def _():
                cast_weights()
                update(0, ROW_BLK, first)

        @pl.when(f == 0)
        def _():
            sweep(True)

        @pl.when(f > 0)
        def _():
            sweep(False)

    @pl.when((f == last_f) & (nb > 0))
    def _store():
        def body(j, c):
            y_copy(w, j, slot).start()
            return c

        lax.fori_loop(0, nb, body, 0)

    @pl.when(prefetch)
    def _():
        x_copy(nxt, f).wait()
        x_cast(nxt, f, 1 - slot)

    @pl.when((w == n_items - 1) & (f == last_f))
    def _drain():
        y_wait_all(jnp.maximum(w - 1, 0), 1 - slot)
        y_wait_all(w, slot)


def _experts(xs, tables, w_gate_up, b_gate_up, w_down, b_down):
    n_slots, d = xs.shape
    n_items = tables[0].shape[0]
    tf = FF_TILE
    nf = N_FF_TILES

    def ff(f, act, w):
        return f * act[w] + (nf - 1) * (1 - act[w])

    grid_spec = pltpu.PrefetchScalarGridSpec(
        num_scalar_prefetch=5,
        grid=(n_items, nf),
        in_specs=[
            pl.BlockSpec(memory_space=pl.ANY),
            pl.BlockSpec((1, d, tf), lambda w, f, we, ws, wnb, wv, act: (we[w], 0, ff(f, act, w))),
            pl.BlockSpec((1, d, tf), lambda w, f, we, ws, wnb, wv, act: (we[w], 0, nf + ff(f, act, w))),
            pl.BlockSpec((1, tf, d), lambda w, f, we, ws, wnb, wv, act: (we[w], ff(f, act, w), 0)),
            pl.BlockSpec((1, 1, tf), lambda w, f, we, ws, wnb, wv, act: (we[w], 0, ff(f, act, w))),
            pl.BlockSpec((1, 1, tf), lambda w, f, we, ws, wnb, wv, act: (we[w], 0, nf + ff(f, act, w))),
            pl.BlockSpec((1, 1, d), lambda w, f, we, ws, wnb, wv, act: (we[w], 0, 0)),
        ],
        out_specs=pl.BlockSpec(memory_space=pl.ANY),
        scratch_shapes=[
            pltpu.VMEM((2, SUPER_ROWS, d), BF16),
            pltpu.VMEM((2, SUPER_ROWS, d), F32),
            pltpu.VMEM((d, 2 * tf), BF16), pltpu.VMEM((tf, d), BF16),
            pltpu.VMEM((ROW_BLK, d), F32),
            pltpu.SemaphoreType.DMA(()), pltpu.SemaphoreType.DMA((2,)),
        ],
    )
    vmem = (2 * SUPER_ROWS * d * 6 + 2 * 3 * d * tf * 4 + 3 * d * tf * 2 + ROW_BLK * d * 4
            + 6 * ROW_BLK * d * 4 + (6 << 20))
    return pl.pallas_call(
        _expert_kernel,
        grid_spec=grid_spec,
        out_shape=jax.ShapeDtypeStruct((n_slots, d), F32),
        compiler_params=pltpu.CompilerParams(
            dimension_semantics=("arbitrary", "arbitrary"), vmem_limit_bytes=_vmem_limit(vmem)),
        name="experts",
    )(*tables, xs, w_gate_up, w_gate_up, w_down, b_gate_up.reshape(N_EXPERTS, 1, 2 * D_FF),
      b_gate_up.reshape(N_EXPERTS, 1, 2 * D_FF), b_down.reshape(N_EXPERTS, 1, d))


def _work_tables(counts, n_items):
    padded = (counts + ROW_BLK - 1) // ROW_BLK * ROW_BLK
    pstart = jnp.cumsum(padded) - padded
    items_e = (padded + SUPER_ROWS - 1) // SUPER_ROWS
    item_end = jnp.cumsum(items_e)
    item_start = item_end - items_e
    w = jnp.arange(n_items, dtype=I32)
    total = item_end[-1]
    active = (w < total).astype(I32)
    wl = jnp.minimum(w, total - 1)
    e = jnp.minimum(jnp.searchsorted(item_end, wl, side='right'), N_EXPERTS - 1).astype(I32)
    local = wl - item_start[e]
    start = pstart[e] + local * SUPER_ROWS
    rows = jnp.minimum(SUPER_ROWS, padded[e] - local * SUPER_ROWS)
    valid = jnp.clip(counts[e] - local * SUPER_ROWS, 0, rows)
    nb = (rows // ROW_BLK) * active
    return (e.astype(I32), start.astype(I32), nb.astype(I32), valid.astype(I32), active)


COMBINE_TILE = 128


def _combine_kernel(dest_ref, dest_next_ref, wts_ref, h_ref, g_ref, ys_hbm, out_ref, buf, sem, *, n):
    i = pl.program_id(0)
    slot = i % 2
    tt = h_ref.shape[0]

    def issue(dref, sl):
        def body(t, c):
            for k in range(TOP_K):
                src = ys_hbm.at[pl.ds(dref[0, TOP_K * t + k], 1)]
                pltpu.make_async_copy(src, buf.at[sl, k, pl.ds(t, 1)], sem.at[sl]).start()
            return c

        lax.fori_loop(0, tt, body, 0, unroll=DMA_UNROLL)

    @pl.when(i == 0)
    def _():
        issue(dest_ref, 0)

    if n > 1:
        @pl.when(i + 1 < n)
        def _():
            issue(dest_next_ref, 1 - slot)

    def drain(t, c):
        for k in range(TOP_K):
            pltpu.make_async_copy(ys_hbm.at[pl.ds(0, 1)], buf.at[slot, k, pl.ds(0, 1)], sem.at[slot]).wait()
        return c

    lax.fori_loop(0, tt, drain, 0, unroll=DMA_UNROLL)
    wts = wts_ref[...]
    moe = wts[:, 0:1] * buf[slot, 0]
    for k in range(1, TOP_K):
        moe = moe + wts[:, k:k + 1] * buf[slot, k]
    y = h_ref[...] + moe
    ms = jnp.mean(y * y, axis=-1, keepdims=True)
    out_ref[...] = (y * lax.rsqrt(ms + NORM_EPS)) * g_ref[...]


def _combine(dest, wts, h, g_final, ys):
    t, d = h.shape
    tt = COMBINE_TILE
    n = t // tt
    return pl.pallas_call(
        functools.partial(_combine_kernel, n=n),
        grid=(n,),
        in_specs=[
            pl.BlockSpec((1, TOP_K * tt), lambda i: (0, i), memory_space=pltpu.SMEM),
            pl.BlockSpec((1, TOP_K * tt), lambda i: (0, jnp.minimum(i + 1, n - 1)), memory_space=pltpu.SMEM),
            pl.BlockSpec((tt, TOP_K), lambda i: (i, 0)),
            pl.BlockSpec((tt, d), lambda i: (i, 0)),
            pl.BlockSpec((1, d), lambda i: (0, 0)),
            pl.BlockSpec(memory_space=pl.ANY),
        ],
        out_specs=pl.BlockSpec((tt, d), lambda i: (i, 0)),
        out_shape=jax.ShapeDtypeStruct((t, d), F32),
        scratch_shapes=[pltpu.VMEM((2, TOP_K, tt, d), F32), pltpu.SemaphoreType.DMA((2,))],
        compiler_params=pltpu.CompilerParams(dimension_semantics=("arbitrary",)),
        name="combine",
    )(dest, dest, wts, h, g_final.reshape(1, d), ys)


def kernel(x_prompt, x_sample, cache_attn_k, cache_attn_v, state_conv, rel_bias_table, norm_mix_g, w_in, b_gate,
           attn_sinks, conv_w, w_o_attn, w_o_conv, w_out, norm_ffn_g, w_router, b_router, w_gate_up, b_gate_up,
           w_down, b_down, norm_final_g):
    batch, seq, d = x_prompt.shape
    dbatch, dseq, _ = x_sample.shape
    assert norm_mix_g.shape[0] == 1 and d == D_MODEL
    t_p, t_s = batch * seq, dbatch * dseq
    xp = x_prompt.reshape(t_p, d)
    xs = x_sample.reshape(t_s, d)

    w_in_bf = w_in[0].astype(BF16)
    woa = w_o_attn[0].astype(BF16)
    woc = w_o_conv[0].astype(BF16)
    wout = w_out[0].astype(BF16)
    wr_t = w_router[0].T.astype(BF16)

    zz_p = _in_proj(xp, norm_mix_g[0], w_in_bf, b_gate[0], tm=1024)
    zz_s = _in_proj(xs, norm_mix_g[0], w_in_bf, b_gate[0], tm=t_s)

    o_p = _attn_prompt(zz_p, batch, seq, rel_bias_table, attn_sinks[0])
    k_new = zz_s[:, ZC_K:ZC_K + KV_DIM].reshape(dbatch, dseq, KV_DIM)
    v_new = zz_s[:, ZC_V:ZC_V + KV_DIM].reshape(dbatch, dseq, KV_DIM)
    k_ext = jnp.concatenate([cache_attn_k[0].reshape(dbatch, WINDOW, KV_DIM), k_new], axis=1)
    v_ext = jnp.concatenate([cache_attn_v[0].reshape(dbatch, WINDOW, KV_DIM), v_new], axis=1)
    o_s = _attn_sample(zz_s, k_ext, v_ext, dbatch, dseq, rel_bias_table, attn_sinks[0])

    hist0 = jnp.repeat(state_conv[0][:, 0, :], dseq, axis=0)
    hist1 = jnp.repeat(state_conv[0][:, 1, :], dseq, axis=0)
    merge_args = (conv_w[0], woa, woc, wout, norm_ffn_g[0], wr_t, b_router[0])
    t_all = t_p + t_s
    h_p, hn, idx_p, wts_p = _merge(o_p, zz_p, xp, None, None, *merge_args, tm=256, seq_rows=None, seq_len=seq,
                                   hn_rows=t_all, hn_row0=0, hn_prev=None)
    h_s, hn, idx_s, wts_s = _merge(o_s, zz_s, xs, hist0, hist1, *merge_args, tm=t_s, seq_rows=dseq, seq_len=dseq,
                                   hn_rows=t_all, hn_row0=t_p, hn_prev=hn)

    idx_t = jnp.concatenate([idx_p, idx_s], axis=1)
    wts = jnp.concatenate([wts_p, wts_s], axis=1).T
    n_assign = t_all * TOP_K
    n_slots = (n_assign + N_EXPERTS * (ROW_BLK - 1) + ROW_BLK - 1) // ROW_BLK * ROW_BLK
    n_items = N_EXPERTS + n_slots // SUPER_ROWS

    dest, cnt = _route(idx_t)
    tables = _work_tables(cnt[:, 0], n_items)
    dest = dest.T.reshape(1, -1)
    xs_sorted = _dispatch(dest, hn, n_slots)
    ys = _experts(xs_sorted, tables, w_gate_up[0], b_gate_up[0], w_down[0], b_down[0])
    y_p = _combine(dest[:, :TOP_K * t_p], wts[:t_p], h_p, norm_final_g, ys)
    y_s = _combine(dest[:, TOP_K * t_p:], wts[t_p:], h_s, norm_final_g, ys)

    zz_p3 = zz_p.reshape(batch, seq, Z_DIM)

    def kv_tail(c0):
        return zz_p3[:, seq - WINDOW:, c0:c0 + KV_DIM].reshape(batch, WINDOW, N_KV_HEADS, HEAD_DIM)[None]

    u_s = zz_s[:, ZC_U:ZC_U + CONV_DIM].reshape(dbatch, dseq, CONV_DIM)
    conv_hist = state_conv.shape[2]
    new_conv_p = zz_p3[:, seq - conv_hist:, ZC_U:ZC_U + CONV_DIM][None]
    new_conv_s = jnp.concatenate([state_conv[0], u_s], axis=1)[:, -conv_hist:][None]
    new_k_s = k_ext[:, -WINDOW:].reshape(dbatch, WINDOW, N_KV_HEADS, HEAD_DIM)[None]
    new_v_s = v_ext[:, -WINDOW:].reshape(dbatch, WINDOW, N_KV_HEADS, HEAD_DIM)[None]
    return (y_p.reshape(batch, seq, d), y_s.reshape(dbatch, dseq, d),
            kv_tail(ZC_K), kv_tail(ZC_V), new_conv_p, new_k_s, new_v_s, new_conv_s)
```

```python
import functools
import math

import numpy as np
import jax
import jax.numpy as jnp
from jax import lax
from jax.experimental import pallas as pl
from jax.experimental.pallas import tpu as pltpu

F32 = jnp.float32
BF16 = jnp.bfloat16
I32 = jnp.int32

D_MODEL = 2048
CHUNK = 64
HEAD_DIM = 64
N_Q_HEADS = 16
N_KV_HEADS = 4
GROUP = N_Q_HEADS // N_KV_HEADS
ATTN_DIM = N_Q_HEADS * HEAD_DIM
KV_DIM = N_KV_HEADS * HEAD_DIM
WINDOW = 128
CONV_DIM = D_MODEL // 2
N_BUCKETS = 32
MAX_DISTANCE = 128
N_EXPERTS = 32
TOP_K = 4
D_FF = D_MODEL
SWIGLU_LIMIT = 7.0
SWIGLU_ALPHA = 1.702
NORM_EPS = 1e-5
IN_DIM = ATTN_DIM + 2 * KV_DIM + 3 * CONV_DIM + 2 * D_MODEL

V7X_VMEM_BYTES = 64 * 1024 * 1024
LANES = 128

ZC_GA = 0
ZC_GC = D_MODEL
ZC_CB = 2 * D_MODEL
ZC_U = ZC_CB + CONV_DIM
ZC_Q = ZC_U + CONV_DIM
ZC_K = ZC_Q + ATTN_DIM
ZC_V = ZC_K + KV_DIM
Z_DIM = ZC_V + KV_DIM
PROJ_TN = 512

ROW_BLK = 256
SUPER_BLKS = 5
SUPER_ROWS = SUPER_BLKS * ROW_BLK
FF_TILE = 256
N_FF_TILES = D_FF // FF_TILE
assert SUPER_BLKS <= N_FF_TILES


def _vmem_limit(nbytes):
    return int(min(nbytes, V7X_VMEM_BYTES - 6 * 1024 * 1024))


_MODE_PLAIN, _MODE_STASH0, _MODE_STASH1, _MODE_MUL0, _MODE_MUL1, _MODE_GATE = range(6)


def _proj_schedule():
    t = PROJ_TN
    cc0 = (ATTN_DIM + 2 * KV_DIM + CONV_DIM) // t
    ch0 = (ATTN_DIM + 2 * KV_DIM + 2 * CONV_DIM) // t
    cb0 = (ATTN_DIM + 2 * KV_DIM) // t
    ga0 = (ATTN_DIM + 2 * KV_DIM + 3 * CONV_DIM) // t
    steps = [
        (cc0, ZC_U // t, 0, _MODE_STASH0),
        (cc0 + 1, ZC_U // t, 0, _MODE_STASH1),
        (ch0, ZC_U // t, 0, _MODE_MUL0),
        (ch0 + 1, ZC_U // t + 1, 0, _MODE_MUL1),
        (0, ZC_Q // t, 0, _MODE_PLAIN),
        (1, ZC_Q // t + 1, 0, _MODE_PLAIN),
        (2, ZC_K // t, 0, _MODE_PLAIN),
        (cb0, ZC_CB // t, 0, _MODE_PLAIN),
        (cb0 + 1, ZC_CB // t + 1, 0, _MODE_PLAIN),
    ]
    for j in range(2 * D_MODEL // t):
        steps.append((ga0 + j, j, j, _MODE_GATE))
    return np.asarray(steps, dtype=np.int32)


def _in_proj_kernel(wt_ref, ot_ref, bt_ref, md_ref, x_ref, g_ref, w_ref, b_ref, o_ref, xn_ref, stash_ref):
    del wt_ref, ot_ref, bt_ref
    n = pl.program_id(1)

    @pl.when(n == 0)
    def _():
        x = x_ref[...]
        ms = jnp.mean(x * x, axis=-1, keepdims=True)
        xn_ref[...] = ((x * lax.rsqrt(ms + NORM_EPS)) * g_ref[...]).astype(BF16)

    def product():
        return jnp.dot(xn_ref[...], w_ref[...].astype(BF16), preferred_element_type=F32)

    md = md_ref[n]

    @pl.when(md == _MODE_PLAIN)
    def _():
        o_ref[...] = product()

    @pl.when(md == _MODE_STASH0)
    def _():
        stash_ref[0] = product()

    @pl.when(md == _MODE_STASH1)
    def _():
        stash_ref[1] = product()

    @pl.when(md == _MODE_MUL0)
    def _():
        o_ref[...] = stash_ref[0] * product()

    @pl.when(md == _MODE_MUL1)
    def _():
        o_ref[...] = stash_ref[1] * product()

    @pl.when(md == _MODE_GATE)
    def _():
        o_ref[...] = jax.nn.sigmoid(product() + b_ref[...])


def _in_proj(x, g_mix, w_in, b_gate, tm):
    t, d = x.shape
    sched = _proj_schedule()
    n_steps = sched.shape[0]
    grid_spec = pltpu.PrefetchScalarGridSpec(
        num_scalar_prefetch=4,
        grid=(t // tm, n_steps),
        in_specs=[
            pl.BlockSpec((tm, d), lambda m, n, wt, ot, bt, md: (m, 0)),
            pl.BlockSpec((1, d), lambda m, n, wt, ot, bt, md: (0, 0)),
            pl.BlockSpec((d, PROJ_TN), lambda m, n, wt, ot, bt, md: (0, wt[n])),
            pl.BlockSpec((1, PROJ_TN), lambda m, n, wt, ot, bt, md: (0, bt[n])),
        ],
        out_specs=pl.BlockSpec((tm, PROJ_TN), lambda m, n, wt, ot, bt, md: (m, ot[n])),
        scratch_shapes=[pltpu.VMEM((tm, d), BF16), pltpu.VMEM((2, tm, PROJ_TN), F32)],
    )
    vmem = 2 * tm * d * 4 + tm * d * 2 + 3 * d * PROJ_TN * 4 + 6 * tm * PROJ_TN * 4 + (8 << 20)
    return pl.pallas_call(
        _in_proj_kernel,
        grid_spec=grid_spec,
        out_shape=jax.ShapeDtypeStruct((t, Z_DIM), F32),
        compiler_params=pltpu.CompilerParams(
            dimension_semantics=("arbitrary", "arbitrary"), vmem_limit_bytes=_vmem_limit(vmem)),
        name="in_proj",
    )(jnp.asarray(sched[:, 0]), jnp.asarray(sched[:, 1]), jnp.asarray(sched[:, 2]), jnp.asarray(sched[:, 3]),
      x, g_mix.reshape(1, d), w_in, b_gate.reshape(1, 2 * D_MODEL))


def _t5_buckets(n_q, n_k):
    half = N_BUCKETS // 2
    max_exact = half // 2
    rel = (np.arange(n_k, dtype=np.int32)[None, :] - WINDOW) - np.arange(n_q, dtype=np.int32)[:, None]
    ret = np.where(rel > 0, half, 0)
    n = np.abs(rel)
    nf = np.maximum(n, 1).astype(np.float32)
    large = max_exact + (np.log(nf / np.float32(max_exact)) / np.float32(math.log(MAX_DISTANCE / max_exact))
                         * np.float32(half - max_exact)).astype(np.int32)
    large = np.minimum(large, half - 1)
    return (ret + np.where(n < max_exact, n, large)).astype(np.int32)


def _build_bias(bias_ref, bucket_ref, table_ref, n_q):
    bucket = bucket_ref[...]
    for kvh in range(N_KV_HEADS):
        for g in range(GROUP):
            h = kvh * GROUP + g
            mat = jnp.zeros(bucket.shape, F32)
            for b in range(N_BUCKETS):
                mat = jnp.where(bucket == b, table_ref[b, h], mat)
            bias_ref[kvh, g * n_q:(g + 1) * n_q, :] = mat


def _attend(q, kwin, vwin, bias_ref, sinks_ref, n_q, invalid):
    outs = []
    grp = lax.broadcasted_iota(I32, (GROUP * n_q, 1), 0) // n_q
    for kvh in range(N_KV_HEADS):
        c0 = kvh * GROUP * HEAD_DIM
        qh = jnp.concatenate([q[:, c0 + g * HEAD_DIM:c0 + (g + 1) * HEAD_DIM] for g in range(GROUP)], axis=0)
        qh = (qh * (HEAD_DIM ** -0.5)).astype(BF16)
        kh = kwin[:, kvh * HEAD_DIM:(kvh + 1) * HEAD_DIM].astype(BF16)
        vh = vwin[:, kvh * HEAD_DIM:(kvh + 1) * HEAD_DIM].astype(BF16)
        s = lax.dot_general(qh, kh, (((1,), (1,)), ((), ())), preferred_element_type=F32) + bias_ref[kvh]
        if invalid is not None:
            s = jnp.where(invalid, -1e30, s)
        sink = jnp.zeros((GROUP * n_q, 1), F32)
        for g in range(GROUP):
            sink = jnp.where(grp == g, sinks_ref[kvh * GROUP + g], sink)
        m = jnp.maximum(jnp.max(s, axis=-1, keepdims=True), sink)
        p = jnp.exp(s - m)
        denom = jnp.sum(p, axis=-1, keepdims=True) + jnp.exp(sink - m)
        o = jnp.dot(p.astype(BF16), vh, preferred_element_type=F32) / denom
        outs.append(o)
    return outs


def _store_heads(o_ref, r0, n_q, outs):
    for kvh, o in enumerate(outs):
        for g in range(GROUP):
            c = (kvh * GROUP + g) * HEAD_DIM
            o_ref[r0:r0 + n_q, c:c + HEAD_DIM] = o[g * n_q:(g + 1) * n_q].astype(o_ref.dtype)


def _attn_prompt_kernel(table_ref, sinks_ref, bucket_ref, q_ref, kp_ref, kc_ref, vp_ref, vc_ref, o_ref, bias_ref):
    first = (pl.program_id(0) == 0) & (pl.program_id(1) == 0)

    @pl.when(first)
    def _():
        _build_bias(bias_ref, bucket_ref, table_ref, CHUNK)

    kwin = jnp.concatenate([kp_ref[...], kc_ref[...]], axis=0)
    vwin = jnp.concatenate([vp_ref[...], vc_ref[...]], axis=0)
    n_k = WINDOW + CHUNK
    seq_start = pl.program_id(1) == 0
    col = lax.broadcasted_iota(I32, (1, n_k), 1)
    for c in range(WINDOW // CHUNK):
        r0 = c * CHUNK
        invalid = seq_start & (col + r0 < WINDOW)
        outs = _attend(q_ref[r0:r0 + CHUNK, :], kwin[r0:r0 + n_k], vwin[r0:r0 + n_k], bias_ref, sinks_ref,
                       CHUNK, invalid)
        _store_heads(o_ref, r0, CHUNK, outs)


def _attn_prompt(zz, batch, seq, table, sinks):
    qb = WINDOW
    nb = seq // qb
    n_k = WINDOW + CHUNK
    bucket = jnp.asarray(_t5_buckets(CHUNK, n_k))
    smem = pl.BlockSpec(memory_space=pltpu.SMEM)
    kblk, vblk = ZC_K // KV_DIM, ZC_V // KV_DIM
    return pl.pallas_call(
        _attn_prompt_kernel,
        grid=(batch, nb),
        in_specs=[
            smem, smem,
            pl.BlockSpec((CHUNK, n_k), lambda b, j: (0, 0)),
            pl.BlockSpec((qb, ATTN_DIM), lambda b, j: (b * nb + j, ZC_Q // ATTN_DIM)),
            pl.BlockSpec((qb, KV_DIM), lambda b, j: (b * nb + jnp.maximum(j - 1, 0), kblk)),
            pl.BlockSpec((qb, KV_DIM), lambda b, j: (b * nb + j, kblk)),
            pl.BlockSpec((qb, KV_DIM), lambda b, j: (b * nb + jnp.maximum(j - 1, 0), vblk)),
            pl.BlockSpec((qb, KV_DIM), lambda b, j: (b * nb + j, vblk)),
        ],
        out_specs=pl.BlockSpec((qb, ATTN_DIM), lambda b, j: (b * nb + j, 0)),
        out_shape=jax.ShapeDtypeStruct((batch * seq, ATTN_DIM), BF16),
        scratch_shapes=[pltpu.VMEM((N_KV_HEADS, GROUP * CHUNK, n_k), F32)],
        compiler_params=pltpu.CompilerParams(dimension_semantics=("arbitrary", "arbitrary")),
        name="attn_prompt",
    )(table, sinks, bucket, zz, zz, zz, zz, zz)


def _attn_sample_kernel(table_ref, sinks_ref, bucket_ref, q_ref, k_ref, v_ref, o_ref, bias_ref, *, n_q):
    @pl.when(pl.program_id(0) == 0)
    def _():
        _build_bias(bias_ref, bucket_ref, table_ref, n_q)

    outs = _attend(q_ref[...], k_ref[0], v_ref[0], bias_ref, sinks_ref, n_q, None)
    _store_heads(o_ref, 0, n_q, outs)


def _attn_sample(zz, k_ext, v_ext, batch, n_q, table, sinks):
    n_k = WINDOW + n_q
    bucket = jnp.asarray(_t5_buckets(n_q, n_k))
    smem = pl.BlockSpec(memory_space=pltpu.SMEM)
    return pl.pallas_call(
        functools.partial(_attn_sample_kernel, n_q=n_q),
        grid=(batch,),
        in_specs=[
            smem, smem,
            pl.BlockSpec((n_q, n_k), lambda b: (0, 0)),
            pl.BlockSpec((n_q, ATTN_DIM), lambda b: (b, ZC_Q // ATTN_DIM)),
            pl.BlockSpec((1, n_k, KV_DIM), lambda b: (b, 0, 0)),
            pl.BlockSpec((1, n_k, KV_DIM), lambda b: (b, 0, 0)),
        ],
        out_specs=pl.BlockSpec((n_q, ATTN_DIM), lambda b: (b, 0)),
        out_shape=jax.ShapeDtypeStruct((batch * n_q, ATTN_DIM), BF16),
        scratch_shapes=[pltpu.VMEM((N_KV_HEADS, GROUP * n_q, n_k), F32)],
        compiler_params=pltpu.CompilerParams(dimension_semantics=("arbitrary",)),
        name="attn_sample",
    )(table, sinks, bucket, zz, k_ext, v_ext)


_MERGE_INPUTS = 15


def _merge_kernel(*refs, tm, seq_rows, tiles_per_seq):
    (o_ref, ga_ref, gc_ref, cb_ref, u_ref, h0_ref, h1_ref, x_ref, cw_ref, woa_ref, woc_ref, wout_ref,
     gf_ref, wr_ref, br_ref) = refs[:_MERGE_INPUTS]
    h_ref, hn_ref, idx_ref, wts_ref = refs[-4:]
    u = u_ref[...]
    row = lax.broadcasted_iota(I32, (tm, 1), 0)
    if seq_rows is None:
        at_start = (pl.program_id(0) % tiles_per_seq) == 0
        hist0 = jnp.where(at_start, 0.0, h0_ref[6:7, :])
        hist1 = jnp.where(at_start, 0.0, h1_ref[7:8, :])
        pos = row
    else:
        hist0 = h0_ref[...]
        hist1 = h1_ref[...]
        pos = row % seq_rows
    r1 = pltpu.roll(u, 1, 0)
    r2 = pltpu.roll(u, 2, 0)
    u1 = jnp.where(pos == 0, hist1, r1)
    u2 = jnp.where(pos == 0, hist0, jnp.where(pos == 1, hist1, r2))
    cw = cw_ref[...]
    y = cw[0:1, :] * u2
    y = y + cw[1:2, :] * u1
    y = y + cw[2:3, :] * u
    c = (cb_ref[...] * y).astype(BF16)
    t = ga_ref[...] * jnp.dot(o_ref[...], woa_ref[...], preferred_element_type=F32)
    t = t + gc_ref[...] * jnp.dot(c, woc_ref[...], preferred_element_type=F32)
    h = x_ref[...] + jnp.dot(t.astype(BF16), wout_ref[...], preferred_element_type=F32)
    h_ref[...] = h
    ms = jnp.mean(h * h, axis=-1, keepdims=True)
    hn = (h * lax.rsqrt(ms + NORM_EPS)) * gf_ref[...]
    hn_ref[...] = hn
    logits = lax.dot_general(wr_ref[...], hn.astype(BF16), (((1,), (1,)), ((), ())),
                             preferred_element_type=F32) + br_ref[...]
    erow = lax.broadcasted_iota(I32, logits.shape, 0).astype(F32)
    vals, idxs = [], []
    for _ in range(TOP_K):
        mx = jnp.max(logits, axis=0, keepdims=True)
        ix = jnp.min(jnp.where(logits == mx, erow, float(N_EXPERTS)), axis=0, keepdims=True)
        vals.append(mx)
        idxs.append(ix)
        logits = jnp.where(erow == ix, -jnp.inf, logits)
    es = [jnp.exp(v - vals[0]) for v in vals]
    tot = es[0]
    for e in es[1:]:
        tot = tot + e
    idx_ref[...] = jnp.concatenate(idxs, axis=0).astype(I32)
    wts_ref[...] = jnp.concatenate([e / tot for e in es], axis=0)


def _merge(o, zz, x, hist0, hist1, conv_w, woa, woc, wout, g_ffn, wr_t, br, tm, seq_rows, seq_len,
           hn_rows, hn_row0, hn_prev):
    t = x.shape[0]
    d = D_MODEL
    assert hn_row0 % tm == 0
    extra_specs, extra_args, aliases = [], [], {}
    if hn_prev is not None:
        extra_specs, extra_args, aliases = [pl.BlockSpec(memory_space=pl.ANY)], [hn_prev], {_MERGE_INPUTS: 1}
    full = lambda shape: pl.BlockSpec(shape, lambda i: (0,) * len(shape))
    if seq_rows is None:
        tiles_per_seq = seq_len // tm
        hist_spec = pl.BlockSpec((8, CONV_DIM), lambda i: (jnp.maximum(i * (tm // 8) - 1, 0), ZC_U // CONV_DIM))
        h0_spec = h1_spec = hist_spec
        h0_arg = h1_arg = zz
    else:
        tiles_per_seq = None
        h0_spec = h1_spec = pl.BlockSpec((tm, CONV_DIM), lambda i: (i, 0))
        h0_arg, h1_arg = hist0, hist1
    kern = functools.partial(_merge_kernel, tm=tm, seq_rows=seq_rows, tiles_per_seq=tiles_per_seq)
    vmem = (2 * tm * (ATTN_DIM * 2 + 5 * d * 4 + 4 * CONV_DIM * 4) + 2 * 2 * (2 * ATTN_DIM * d + d * d)
            + 10 * tm * d * 4 + (8 << 20))
    return pl.pallas_call(
        kern,
        grid=(t // tm,),
        in_specs=[
            pl.BlockSpec((tm, ATTN_DIM), lambda i: (i, 0)),
            pl.BlockSpec((tm, d), lambda i: (i, ZC_GA // d)),
            pl.BlockSpec((tm, d), lambda i: (i, ZC_GC // d)),
            pl.BlockSpec((tm, CONV_DIM), lambda i: (i, ZC_CB // CONV_DIM)),
            pl.BlockSpec((tm, CONV_DIM), lambda i: (i, ZC_U // CONV_DIM)),
            h0_spec, h1_spec,
            pl.BlockSpec((tm, d), lambda i: (i, 0)),
            full((3, CONV_DIM)),
            full((ATTN_DIM, d)), full((CONV_DIM, d)), full((d, d)),
            full((1, d)), full((N_EXPERTS, d)), full((N_EXPERTS, 1)),
        ] + extra_specs,
        out_specs=[
            pl.BlockSpec((tm, d), lambda i: (i, 0)),
            pl.BlockSpec((tm, d), lambda i: (hn_row0 // tm + i, 0)),
            pl.BlockSpec((TOP_K, tm), lambda i: (0, i)),
            pl.BlockSpec((TOP_K, tm), lambda i: (0, i)),
        ],
        out_shape=[
            jax.ShapeDtypeStruct((t, d), F32),
            jax.ShapeDtypeStruct((hn_rows, d), F32),
            jax.ShapeDtypeStruct((TOP_K, t), I32),
            jax.ShapeDtypeStruct((TOP_K, t), F32),
        ],
        input_output_aliases=aliases,
        compiler_params=pltpu.CompilerParams(dimension_semantics=("arbitrary",), vmem_limit_bytes=_vmem_limit(vmem)),
        name="merge_prompt" if seq_rows is None else "merge_sample",
    )(o, zz, zz, zz, zz, h0_arg, h1_arg, x, conv_w, woa, woc, wout, g_ffn.reshape(1, d), wr_t,
      br.reshape(N_EXPERTS, 1), *extra_args)


ROUTE_TILE = 384


def _route_kernel(idx_ref, dest_ref, cnt_ref, cnt_acc, carry, pst, tri_ref):
    p = pl.program_id(0)
    j = pl.program_id(1)
    tr = idx_ref.shape[1]
    idx = idx_ref[...]
    erow = lax.broadcasted_iota(I32, (N_EXPERTS, tr), 0)
    masks = [(erow == idx[k:k + 1, :]).astype(F32) for k in range(TOP_K)]
    mtot = masks[0]
    for m in masks[1:]:
        mtot = mtot + m
    tile_cnt = jnp.sum(mtot, axis=1, keepdims=True)

    @pl.when((p == 0) & (j == 0))
    def _():
        cnt_acc[...] = jnp.zeros_like(cnt_acc)

    @pl.when(p == 0)
    def _():
        cnt_acc[...] += jnp.broadcast_to(tile_cnt, cnt_acc.shape)

    @pl.when((p == 1) & (j == 0))
    def _():
        cnt = cnt_acc[:, 0:1]
        padded = jnp.floor((cnt + float(ROW_BLK - 1)) / float(ROW_BLK)) * float(ROW_BLK)
        r = lax.broadcasted_iota(I32, (N_EXPERTS, N_EXPERTS), 0)
        c = lax.broadcasted_iota(I32, (N_EXPERTS, N_EXPERTS), 1)
        rowv = jnp.sum(jnp.where(r == c, padded, 0.0), axis=0, keepdims=True)
        start = jnp.sum(jnp.where(c < r, rowv, 0.0), axis=1, keepdims=True)
        pst[...] = jnp.broadcast_to(start, pst.shape)
        carry[...] = jnp.zeros_like(carry)
        a = lax.broadcasted_iota(I32, (tr, tr), 0)
        b = lax.broadcasted_iota(I32, (tr, tr), 1)
        tri_ref[...] = (a < b).astype(BF16)

    @pl.when(p == 1)
    def _():
        excl = jnp.dot(mtot.astype(BF16), tri_ref[...], preferred_element_type=F32)
        val = pst[:, 0:1] + carry[:, 0:1] + excl
        dest = jnp.concatenate([jnp.sum(m * val, axis=0, keepdims=True) for m in masks], axis=0)
        dest_ref[...] = dest.astype(I32)
        carry[...] += jnp.broadcast_to(tile_cnt, carry.shape)
        cnt_ref[...] = cnt_acc[...].astype(I32)


def _route(idx_t):
    t = idx_t.shape[1]
    tr = ROUTE_TILE
    return pl.pallas_call(
        _route_kernel,
        grid=(2, t // tr),
        in_specs=[pl.BlockSpec((TOP_K, tr), lambda p, j: (0, j))],
        out_specs=[
            pl.BlockSpec((TOP_K, tr), lambda p, j: (0, j * p)),
            pl.BlockSpec((N_EXPERTS, LANES), lambda p, j: (0, 0)),
        ],
        out_shape=[jax.ShapeDtypeStruct((TOP_K, t), I32), jax.ShapeDtypeStruct((N_EXPERTS, LANES), I32)],
        scratch_shapes=[pltpu.VMEM((N_EXPERTS, LANES), F32), pltpu.VMEM((N_EXPERTS, LANES), F32),
                        pltpu.VMEM((N_EXPERTS, LANES), F32), pltpu.VMEM((tr, tr), BF16)],
        compiler_params=pltpu.CompilerParams(dimension_semantics=("arbitrary", "arbitrary")),
        name="route",
    )(idx_t)


DISPATCH_TILE = 384
DMA_UNROLL = 8


def _dispatch_kernel(dest_ref, hn_ref, xs_hbm, sem):
    tt = hn_ref.shape[0]

    def issue(t, c):
        for k in range(TOP_K):
            dst = xs_hbm.at[pl.ds(dest_ref[0, TOP_K * t + k], 1)]
            pltpu.make_async_copy(hn_ref.at[pl.ds(t, 1)], dst, sem).start(priority=k % 2)
        return c

    lax.fori_loop(0, tt, issue, 0, unroll=DMA_UNROLL)

    def drain(t, c):
        for k in range(TOP_K):
            pltpu.make_async_copy(hn_ref.at[pl.ds(0, 1)], xs_hbm.at[pl.ds(0, 1)], sem).wait()
        return c

    lax.fori_loop(0, tt, drain, 0, unroll=DMA_UNROLL)


def _dispatch(dest, hn, n_slots):
    t, d = hn.shape
    tt = DISPATCH_TILE
    return pl.pallas_call(
        _dispatch_kernel,
        grid=(t // tt,),
        in_specs=[pl.BlockSpec((1, TOP_K * tt), lambda i: (0, i), memory_space=pltpu.SMEM),
                  pl.BlockSpec((tt, d), lambda i: (i, 0))],
        out_specs=pl.BlockSpec(memory_space=pl.ANY),
        out_shape=jax.ShapeDtypeStruct((n_slots, d), hn.dtype),
        scratch_shapes=[pltpu.SemaphoreType.DMA(())],
        compiler_params=pltpu.CompilerParams(dimension_semantics=("arbitrary",)),
        name="dispatch",
    )(dest, hn)


def _expert_kernel(we_ref, ws_ref, wnb_ref, wvalid_ref, wact_ref,
                   xs_hbm, wg_ref, wu_ref, wd_ref, bg_ref, bu_ref, bd_ref, ys_hbm,
                   x_sb, acc, wgu_bf, wd_bf, stage, sem_in, sem_out):
    del we_ref, wact_ref
    w = pl.program_id(0)
    f = pl.program_id(1)
    n_items = wnb_ref.shape[0]
    last_f = N_FF_TILES - 1
    slot = w % 2
    nb = wnb_ref[w]
    row = lax.broadcasted_iota(I32, (ROW_BLK, 1), 0)

    def x_copy(item, j):
        src = xs_hbm.at[pl.ds(pl.multiple_of(ws_ref[item] + j * ROW_BLK, ROW_BLK), ROW_BLK)]
        return pltpu.make_async_copy(src, stage, sem_in)

    def x_cast(item, j, sl):
        r0 = pl.multiple_of(j * ROW_BLK, ROW_BLK)
        x_sb[sl, pl.ds(r0, ROW_BLK), :] = jnp.where(row + r0 < wvalid_ref[item], stage[...], 0.0).astype(BF16)

    def y_copy(item, j, sl):
        r0 = pl.multiple_of(j * ROW_BLK, ROW_BLK)
        dst = ys_hbm.at[pl.ds(pl.multiple_of(ws_ref[item] + r0, ROW_BLK), ROW_BLK)]
        return pltpu.make_async_copy(acc.at[sl, pl.ds(r0, ROW_BLK)], dst, sem_out.at[sl])

    def y_wait_all(item, sl):
        def body(j, c):
            y_copy(item, j, sl).wait()
            return c

        lax.fori_loop(0, wnb_ref[item], body, 0)

    @pl.when((w == 0) & (f == 0))
    def _first_rows():
        def body(j, c):
            cp = x_copy(0, j)
            cp.start()
            cp.wait()
            x_cast(0, j, 0)
            return c

        lax.fori_loop(0, wnb_ref[0], body, 0)

    nxt = jnp.minimum(w + 1, n_items - 1)
    prefetch = (w + 1 < n_items) & (f < wnb_ref[nxt])

    @pl.when(prefetch)
    def _():
        x_copy(nxt, f).start()

    @pl.when((f == 0) & (w >= 2))
    def _reclaim_acc():
        y_wait_all(jnp.maximum(w - 2, 0), slot)

    @pl.when(nb > 0)
    def _compute():
        bg = bg_ref[0]
        bu = bu_ref[0]
        bd = bd_ref[0]

        def cast_weights():
            wgu_bf[:, :FF_TILE] = wg_ref[0].astype(BF16)
            wgu_bf[:, FF_TILE:] = wu_ref[0].astype(BF16)
            wd_bf[...] = wd_ref[0].astype(BF16)

        def update(r0, rows, first):
            x = x_sb[slot, pl.ds(r0, rows), :]
            gu = jnp.dot(x, wgu_bf[...], preferred_element_type=F32)
            g = jnp.minimum(gu[:, :FF_TILE] + bg, SWIGLU_LIMIT)
            u = jnp.clip(gu[:, FF_TILE:] + bu, -SWIGLU_LIMIT, SWIGLU_LIMIT)
            hh = (u + 1.0) * (g * jax.nn.sigmoid(SWIGLU_ALPHA * g))
            part = jnp.dot(hh.astype(BF16), wd_bf[...], preferred_element_type=F32)
            if first:
                acc[slot, pl.ds(r0, rows), :] = part + bd
            else:
                acc[slot, pl.ds(r0, rows), :] += part

        def sweep(first):
            def body(p, c):
                update(pl.multiple_of(p * (2 * ROW_BLK), 2 * ROW_BLK), 2 * ROW_BLK, first)
                return c

            @pl.when(nb >= 2)
            def _():
                cast_weights()
                update(0, 2 * ROW_BLK, first)
                lax.fori_loop(1, nb // 2, body, 0)

                @pl.when(nb % 2 == 1)
                def _():
                    update(pl.multiple_of((nb - 1) * ROW_BLK, ROW_BLK), ROW_BLK, first)

            @pl.when(nb == 1)
            def _():
                cast_weights()
                update(0, ROW_BLK, first)

        @pl.when(f == 0)
        def _():
            sweep(True)

        @pl.when(f > 0)
        def _():
            sweep(False)

    @pl.when((f == last_f) & (nb > 0))
    def _store():
        def body(j, c):
            y_copy(w, j, slot).start()
            return c

        lax.fori_loop(0, nb, body, 0)

    @pl.when(prefetch)
    def _():
        x_copy(nxt, f).wait()
        x_cast(nxt, f, 1 - slot)

    @pl.when((w == n_items - 1) & (f == last_f))
    def _drain():
        y_wait_all(jnp.maximum(w - 1, 0), 1 - slot)
        y_wait_all(w, slot)


def _experts(xs, tables, w_gate_up, b_gate_up, w_down, b_down):
    n_slots, d = xs.shape
    n_items = tables[0].shape[0]
    tf = FF_TILE
    nf = N_FF_TILES

    def ff(f, act, w):
        return f * act[w] + (nf - 1) * (1 - act[w])

    grid_spec = pltpu.PrefetchScalarGridSpec(
        num_scalar_prefetch=5,
        grid=(n_items, nf),
        in_specs=[
            pl.BlockSpec(memory_space=pl.ANY),
            pl.BlockSpec((1, d, tf), lambda w, f, we, ws, wnb, wv, act: (we[w], 0, ff(f, act, w))),
            pl.BlockSpec((1, d, tf), lambda w, f, we, ws, wnb, wv, act: (we[w], 0, nf + ff(f, act, w))),
            pl.BlockSpec((1, tf, d), lambda w, f, we, ws, wnb, wv, act: (we[w], ff(f, act, w), 0)),
            pl.BlockSpec((1, 1, tf), lambda w, f, we, ws, wnb, wv, act: (we[w], 0, ff(f, act, w))),
            pl.BlockSpec((1, 1, tf), lambda w, f, we, ws, wnb, wv, act: (we[w], 0, nf + ff(f, act, w))),
            pl.BlockSpec((1, 1, d), lambda w, f, we, ws, wnb, wv, act: (we[w], 0, 0)),
        ],
        out_specs=pl.BlockSpec(memory_space=pl.ANY),
        scratch_shapes=[
            pltpu.VMEM((2, SUPER_ROWS, d), BF16),
            pltpu.VMEM((2, SUPER_ROWS, d), F32),
            pltpu.VMEM((d, 2 * tf), BF16), pltpu.VMEM((tf, d), BF16),
            pltpu.VMEM((ROW_BLK, d), F32),
            pltpu.SemaphoreType.DMA(()), pltpu.SemaphoreType.DMA((2,)),
        ],
    )
    vmem = (2 * SUPER_ROWS * d * 6 + 2 * 3 * d * tf * 4 + 3 * d * tf * 2 + ROW_BLK * d * 4
            + 6 * ROW_BLK * d * 4 + (6 << 20))
    return pl.pallas_call(
        _expert_kernel,
        grid_spec=grid_spec,
        out_shape=jax.ShapeDtypeStruct((n_slots, d), F32),
        compiler_params=pltpu.CompilerParams(
            dimension_semantics=("arbitrary", "arbitrary"), vmem_limit_bytes=_vmem_limit(vmem)),
        name="experts",
    )(*tables, xs, w_gate_up, w_gate_up, w_down, b_gate_up.reshape(N_EXPERTS, 1, 2 * D_FF),
      b_gate_up.reshape(N_EXPERTS, 1, 2 * D_FF), b_down.reshape(N_EXPERTS, 1, d))


def _work_tables(counts, n_items):
    padded = (counts + ROW_BLK - 1) // ROW_BLK * ROW_BLK
    pstart = jnp.cumsum(padded) - padded
    items_e = (padded + SUPER_ROWS - 1) // SUPER_ROWS
    item_end = jnp.cumsum(items_e)
    item_start = item_end - items_e
    w = jnp.arange(n_items, dtype=I32)
    total = item_end[-1]
    active = (w < total).astype(I32)
    wl = jnp.minimum(w, total - 1)
    e = jnp.minimum(jnp.searchsorted(item_end, wl, side='right'), N_EXPERTS - 1).astype(I32)
    local = wl - item_start[e]
    start = pstart[e] + local * SUPER_ROWS
    rows = jnp.minimum(SUPER_ROWS, padded[e] - local * SUPER_ROWS)
    valid = jnp.clip(counts[e] - local * SUPER_ROWS, 0, rows)
    nb = (rows // ROW_BLK) * active
    return (e.astype(I32), start.astype(I32), nb.astype(I32), valid.astype(I32), active)


COMBINE_TILE = 128


def _combine_kernel(dest_ref, dest_next_ref, wts_ref, h_ref, g_ref, ys_hbm, out_ref, buf, sem, *, n):
    i = pl.program_id(0)
    slot = i % 2
    tt = h_ref.shape[0]

    def issue(dref, sl):
        def body(t, c):
            for k in range(TOP_K):
                src = ys_hbm.at[pl.ds(dref[0, TOP_K * t + k], 1)]
                pltpu.make_async_copy(src, buf.at[sl, k, pl.ds(t, 1)], sem.at[sl]).start(priority=k % 2)
            return c

        lax.fori_loop(0, tt, body, 0, unroll=DMA_UNROLL)

    @pl.when(i == 0)
    def _():
        issue(dest_ref, 0)

    if n > 1:
        @pl.when(i + 1 < n)
        def _():
            issue(dest_next_ref, 1 - slot)

    def drain(t, c):
        for k in range(TOP_K):
            pltpu.make_async_copy(ys_hbm.at[pl.ds(0, 1)], buf.at[slot, k, pl.ds(0, 1)], sem.at[slot]).wait()
        return c

    lax.fori_loop(0, tt, drain, 0, unroll=DMA_UNROLL)
    wts = wts_ref[...]
    moe = wts[:, 0:1] * buf[slot, 0]
    for k in range(1, TOP_K):
        moe = moe + wts[:, k:k + 1] * buf[slot, k]
    y = h_ref[...] + moe
    ms = jnp.mean(y * y, axis=-1, keepdims=True)
    out_ref[...] = (y * lax.rsqrt(ms + NORM_EPS)) * g_ref[...]


def _combine(dest, wts, h, g_final, ys):
    t, d = h.shape
    tt = COMBINE_TILE
    n = t // tt
    return pl.pallas_call(
        functools.partial(_combine_kernel, n=n),
        grid=(n,),
        in_specs=[
            pl.BlockSpec((1, TOP_K * tt), lambda i: (0, i), memory_space=pltpu.SMEM),
            pl.BlockSpec((1, TOP_K * tt), lambda i: (0, jnp.minimum(i + 1, n - 1)), memory_space=pltpu.SMEM),
            pl.BlockSpec((tt, TOP_K), lambda i: (i, 0)),
            pl.BlockSpec((tt, d), lambda i: (i, 0)),
            pl.BlockSpec((1, d), lambda i: (0, 0)),
            pl.BlockSpec(memory_space=pl.ANY),
        ],
        out_specs=pl.BlockSpec((tt, d), lambda i: (i, 0)),
        out_shape=jax.ShapeDtypeStruct((t, d), F32),
        scratch_shapes=[pltpu.VMEM((2, TOP_K, tt, d), F32), pltpu.SemaphoreType.DMA((2,))],
        compiler_params=pltpu.CompilerParams(dimension_semantics=("arbitrary",)),
        name="combine",
    )(dest, dest, wts, h, g_final.reshape(1, d), ys)


def kernel(x_prompt, x_sample, cache_attn_k, cache_attn_v, state_conv, rel_bias_table, norm_mix_g, w_in, b_gate,
           attn_sinks, conv_w, w_o_attn, w_o_conv, w_out, norm_ffn_g, w_router, b_router, w_gate_up, b_gate_up,
           w_down, b_down, norm_final_g):
    batch, seq, d = x_prompt.shape
    dbatch, dseq, _ = x_sample.shape
    assert norm_mix_g.shape[0] == 1 and d == D_MODEL
    t_p, t_s = batch * seq, dbatch * dseq
    xp = x_prompt.reshape(t_p, d)
    xs = x_sample.reshape(t_s, d)

    woa = w_o_attn[0].astype(BF16)
    woc = w_o_conv[0].astype(BF16)
    wout = w_out[0].astype(BF16)
    wr_t = w_router[0].T.astype(BF16)

    zz_p = _in_proj(xp, norm_mix_g[0], w_in[0], b_gate[0], tm=1024)
    zz_s = _in_proj(xs, norm_mix_g[0], w_in[0], b_gate[0], tm=t_s)

    o_p = _attn_prompt(zz_p, batch, seq, rel_bias_table, attn_sinks[0])
    k_new = zz_s[:, ZC_K:ZC_K + KV_DIM].reshape(dbatch, dseq, KV_DIM)
    v_new = zz_s[:, ZC_V:ZC_V + KV_DIM].reshape(dbatch, dseq, KV_DIM)
    k_ext = jnp.concatenate([cache_attn_k[0].reshape(dbatch, WINDOW, KV_DIM), k_new], axis=1)
    v_ext = jnp.concatenate([cache_attn_v[0].reshape(dbatch, WINDOW, KV_DIM), v_new], axis=1)
    o_s = _attn_sample(zz_s, k_ext, v_ext, dbatch, dseq, rel_bias_table, attn_sinks[0])

    hist0 = jnp.repeat(state_conv[0][:, 0, :], dseq, axis=0)
    hist1 = jnp.repeat(state_conv[0][:, 1, :], dseq, axis=0)
    merge_args = (conv_w[0], woa, woc, wout, norm_ffn_g[0], wr_t, b_router[0])
    t_all = t_p + t_s
    h_p, hn, idx_p, wts_p = _merge(o_p, zz_p, xp, None, None, *merge_args, tm=256, seq_rows=None, seq_len=seq,
                                   hn_rows=t_all, hn_row0=0, hn_prev=None)
    h_s, hn, idx_s, wts_s = _merge(o_s, zz_s, xs, hist0, hist1, *merge_args, tm=t_s, seq_rows=dseq, seq_len=dseq,
                                   hn_rows=t_all, hn_row0=t_p, hn_prev=hn)

    idx_t = jnp.concatenate([idx_p, idx_s], axis=1)
    wts = jnp.concatenate([wts_p, wts_s], axis=1).T
    n_assign = t_all * TOP_K
    n_slots = (n_assign + N_EXPERTS * (ROW_BLK - 1) + ROW_BLK - 1) // ROW_BLK * ROW_BLK
    n_items = N_EXPERTS + n_slots // SUPER_ROWS

    dest, cnt = _route(idx_t)
    tables = _work_tables(cnt[:, 0], n_items)
    dest = dest.T.reshape(1, -1)
    xs_sorted = _dispatch(dest, hn, n_slots)
    ys = _experts(xs_sorted, tables, w_gate_up[0], b_gate_up[0], w_down[0], b_down[0])
    y_p = _combine(dest[:, :TOP_K * t_p], wts[:t_p], h_p, norm_final_g, ys)
    y_s = _combine(dest[:, TOP_K * t_p:], wts[t_p:], h_s, norm_final_g, ys)

    zz_p3 = zz_p.reshape(batch, seq, Z_DIM)

    def kv_tail(c0):
        return zz_p3[:, seq - WINDOW:, c0:c0 + KV_DIM].reshape(batch, WINDOW, N_KV_HEADS, HEAD_DIM)[None]

    u_s = zz_s[:, ZC_U:ZC_U + CONV_DIM].reshape(dbatch, dseq, CONV_DIM)
    conv_hist = state_conv.shape[2]
    new_conv_p = zz_p3[:, seq - conv_hist:, ZC_U:ZC_U + CONV_DIM][None]
    new_conv_s = jnp.concatenate([state_conv[0], u_s], axis=1)[:, -conv_hist:][None]
    new_k_s = k_ext[:, -WINDOW:].reshape(dbatch, WINDOW, N_KV_HEADS, HEAD_DIM)[None]
    new_v_s = v_ext[:, -WINDOW:].reshape(dbatch, WINDOW, N_KV_HEADS, HEAD_DIM)[None]
    return (y_p.reshape(batch, seq, d), y_s.reshape(dbatch, dseq, d),
            kv_tail(ZC_K), kv_tail(ZC_V), new_conv_p, new_k_s, new_v_s, new_conv_s)
```

```python
import functools
import math

import numpy as np
import jax
import jax.numpy as jnp
from jax import lax
from jax.experimental import pallas as pl
from jax.experimental.pallas import tpu as pltpu

F32 = jnp.float32
BF16 = jnp.bfloat16
I32 = jnp.int32

D_MODEL = 2048
CHUNK = 64
HEAD_DIM = 64
N_Q_HEADS = 16
N_KV_HEADS = 4
GROUP = N_Q_HEADS // N_KV_HEADS
ATTN_DIM = N_Q_HEADS * HEAD_DIM
KV_DIM = N_KV_HEADS * HEAD_DIM
WINDOW = 128
CONV_DIM = D_MODEL // 2
N_BUCKETS = 32
MAX_DISTANCE = 128
N_EXPERTS = 32
TOP_K = 4
D_FF = D_MODEL
SWIGLU_LIMIT = 7.0
SWIGLU_ALPHA = 1.702
NORM_EPS = 1e-5
IN_DIM = ATTN_DIM + 2 * KV_DIM + 3 * CONV_DIM + 2 * D_MODEL

V7X_VMEM_BYTES = 64 * 1024 * 1024
LANES = 128

ZC_GA = 0
ZC_GC = D_MODEL
ZC_CB = 2 * D_MODEL
ZC_U = ZC_CB + CONV_DIM
ZC_Q = ZC_U + CONV_DIM
ZC_K = ZC_Q + ATTN_DIM
ZC_V = ZC_K + KV_DIM
Z_DIM = ZC_V + KV_DIM
PROJ_TN = 512

ROW_BLK = 256
SUPER_BLKS = 5
SUPER_ROWS = SUPER_BLKS * ROW_BLK
FF_TILE = 256
N_FF_TILES = D_FF // FF_TILE
assert SUPER_BLKS <= N_FF_TILES


def _vmem_limit(nbytes):
    return int(min(nbytes, V7X_VMEM_BYTES - 6 * 1024 * 1024))


_MODE_PLAIN, _MODE_STASH0, _MODE_STASH1, _MODE_MUL0, _MODE_MUL1, _MODE_GATE = range(6)


def _proj_schedule():
    t = PROJ_TN
    cc0 = (ATTN_DIM + 2 * KV_DIM + CONV_DIM) // t
    ch0 = (ATTN_DIM + 2 * KV_DIM + 2 * CONV_DIM) // t
    cb0 = (ATTN_DIM + 2 * KV_DIM) // t
    ga0 = (ATTN_DIM + 2 * KV_DIM + 3 * CONV_DIM) // t
    steps = [
        (cc0, ZC_U // t, 0, _MODE_STASH0),
        (cc0 + 1, ZC_U // t, 0, _MODE_STASH1),
        (ch0, ZC_U // t, 0, _MODE_MUL0),
        (ch0 + 1, ZC_U // t + 1, 0, _MODE_MUL1),
        (0, ZC_Q // t, 0, _MODE_PLAIN),
        (1, ZC_Q // t + 1, 0, _MODE_PLAIN),
        (2, ZC_K // t, 0, _MODE_PLAIN),
        (cb0, ZC_CB // t, 0, _MODE_PLAIN),
        (cb0 + 1, ZC_CB // t + 1, 0, _MODE_PLAIN),
    ]
    for j in range(2 * D_MODEL // t):
        steps.append((ga0 + j, j, j, _MODE_GATE))
    return np.asarray(steps, dtype=np.int32)


def _in_proj_kernel(wt_ref, ot_ref, bt_ref, md_ref, x_ref, g_ref, w_ref, b_ref, o_ref, xn_ref, stash_ref):
    del wt_ref, ot_ref, bt_ref
    n = pl.program_id(1)

    @pl.when(n == 0)
    def _():
        x = x_ref[...]
        ms = jnp.mean(x * x, axis=-1, keepdims=True)
        xn_ref[...] = ((x * lax.rsqrt(ms + NORM_EPS)) * g_ref[...]).astype(BF16)

    def product():
        return jnp.dot(xn_ref[...], w_ref[...], preferred_element_type=F32)

    md = md_ref[n]

    @pl.when(md == _MODE_PLAIN)
    def _():
        o_ref[...] = product()

    @pl.when(md == _MODE_STASH0)
    def _():
        stash_ref[0] = product()

    @pl.when(md == _MODE_STASH1)
    def _():
        stash_ref[1] = product()

    @pl.when(md == _MODE_MUL0)
    def _():
        o_ref[...] = stash_ref[0] * product()

    @pl.when(md == _MODE_MUL1)
    def _():
        o_ref[...] = stash_ref[1] * product()

    @pl.when(md == _MODE_GATE)
    def _():
        o_ref[...] = jax.nn.sigmoid(product() + b_ref[...])


def _in_proj(x, g_mix, w_in, b_gate, tm):
    t, d = x.shape
    sched = _proj_schedule()
    n_steps = sched.shape[0]
    grid_spec = pltpu.PrefetchScalarGridSpec(
        num_scalar_prefetch=4,
        grid=(t // tm, n_steps),
        in_specs=[
            pl.BlockSpec((tm, d), lambda m, n, wt, ot, bt, md: (m, 0)),
            pl.BlockSpec((1, d), lambda m, n, wt, ot, bt, md: (0, 0)),
            pl.BlockSpec((d, PROJ_TN), lambda m, n, wt, ot, bt, md: (0, wt[n])),
            pl.BlockSpec((1, PROJ_TN), lambda m, n, wt, ot, bt, md: (0, bt[n])),
        ],
        out_specs=pl.BlockSpec((tm, PROJ_TN), lambda m, n, wt, ot, bt, md: (m, ot[n])),
        scratch_shapes=[pltpu.VMEM((tm, d), BF16), pltpu.VMEM((2, tm, PROJ_TN), F32)],
    )
    vmem = 2 * tm * d * 4 + tm * d * 2 + 2 * d * PROJ_TN * 2 + 6 * tm * PROJ_TN * 4 + (8 << 20)
    return pl.pallas_call(
        _in_proj_kernel,
        grid_spec=grid_spec,
        out_shape=jax.ShapeDtypeStruct((t, Z_DIM), F32),
        compiler_params=pltpu.CompilerParams(
            dimension_semantics=("arbitrary", "arbitrary"), vmem_limit_bytes=_vmem_limit(vmem)),
        name="in_proj",
    )(jnp.asarray(sched[:, 0]), jnp.asarray(sched[:, 1]), jnp.asarray(sched[:, 2]), jnp.asarray(sched[:, 3]),
      x, g_mix.reshape(1, d), w_in, b_gate.reshape(1, 2 * D_MODEL))


def _t5_buckets(n_q, n_k):
    half = N_BUCKETS // 2
    max_exact = half // 2
    rel = (np.arange(n_k, dtype=np.int32)[None, :] - WINDOW) - np.arange(n_q, dtype=np.int32)[:, None]
    ret = np.where(rel > 0, half, 0)
    n = np.abs(rel)
    nf = np.maximum(n, 1).astype(np.float32)
    large = max_exact + (np.log(nf / np.float32(max_exact)) / np.float32(math.log(MAX_DISTANCE / max_exact))
                         * np.float32(half - max_exact)).astype(np.int32)
    large = np.minimum(large, half - 1)
    return (ret + np.where(n < max_exact, n, large)).astype(np.int32)


def _build_bias(bias_ref, bucket_ref, table_ref, n_q):
    bucket = bucket_ref[...]
    for kvh in range(N_KV_HEADS):
        for g in range(GROUP):
            h = kvh * GROUP + g
            mat = jnp.zeros(bucket.shape, F32)
            for b in range(N_BUCKETS):
                mat = jnp.where(bucket == b, table_ref[b, h], mat)
            bias_ref[kvh, g * n_q:(g + 1) * n_q, :] = mat


def _attend(q, kwin, vwin, bias_ref, sinks_ref, n_q, invalid):
    outs = []
    grp = lax.broadcasted_iota(I32, (GROUP * n_q, 1), 0) // n_q
    for kvh in range(N_KV_HEADS):
        c0 = kvh * GROUP * HEAD_DIM
        qh = jnp.concatenate([q[:, c0 + g * HEAD_DIM:c0 + (g + 1) * HEAD_DIM] for g in range(GROUP)], axis=0)
        qh = (qh * (HEAD_DIM ** -0.5)).astype(BF16)
        kh = kwin[:, kvh * HEAD_DIM:(kvh + 1) * HEAD_DIM].astype(BF16)
        vh = vwin[:, kvh * HEAD_DIM:(kvh + 1) * HEAD_DIM].astype(BF16)
        s = lax.dot_general(qh, kh, (((1,), (1,)), ((), ())), preferred_element_type=F32) + bias_ref[kvh]
        if invalid is not None:
            s = jnp.where(invalid, -1e30, s)
        sink = jnp.zeros((GROUP * n_q, 1), F32)
        for g in range(GROUP):
            sink = jnp.where(grp == g, sinks_ref[kvh * GROUP + g], sink)
        m = jnp.maximum(jnp.max(s, axis=-1, keepdims=True), sink)
        p = jnp.exp(s - m)
        denom = jnp.sum(p, axis=-1, keepdims=True) + jnp.exp(sink - m)
        o = jnp.dot(p.astype(BF16), vh, preferred_element_type=F32) / denom
        outs.append(o)
    return outs


def _store_heads(o_ref, r0, n_q, outs):
    for kvh, o in enumerate(outs):
        for g in range(GROUP):
            c = (kvh * GROUP + g) * HEAD_DIM
            o_ref[r0:r0 + n_q, c:c + HEAD_DIM] = o[g * n_q:(g + 1) * n_q].astype(o_ref.dtype)


def _attn_prompt_kernel(table_ref, sinks_ref, bucket_ref, q_ref, kp_ref, kc_ref, vp_ref, vc_ref, o_ref, bias_ref):
    first = (pl.program_id(0) == 0) & (pl.program_id(1) == 0)

    @pl.when(first)
    def _():
        _build_bias(bias_ref, bucket_ref, table_ref, CHUNK)

    kwin = jnp.concatenate([kp_ref[...], kc_ref[...]], axis=0)
    vwin = jnp.concatenate([vp_ref[...], vc_ref[...]], axis=0)
    n_k = WINDOW + CHUNK
    seq_start = pl.program_id(1) == 0
    col = lax.broadcasted_iota(I32, (1, n_k), 1)
    for c in range(WINDOW // CHUNK):
        r0 = c * CHUNK
        invalid = seq_start & (col + r0 < WINDOW)
        outs = _attend(q_ref[r0:r0 + CHUNK, :], kwin[r0:r0 + n_k], vwin[r0:r0 + n_k], bias_ref, sinks_ref,
                       CHUNK, invalid)
        _store_heads(o_ref, r0, CHUNK, outs)


def _attn_prompt(zz, batch, seq, table, sinks):
    qb = WINDOW
    nb = seq // qb
    n_k = WINDOW + CHUNK
    bucket = jnp.asarray(_t5_buckets(CHUNK, n_k))
    smem = pl.BlockSpec(memory_space=pltpu.SMEM)
    kblk, vblk = ZC_K // KV_DIM, ZC_V // KV_DIM
    return pl.pallas_call(
        _attn_prompt_kernel,
        grid=(batch, nb),
        in_specs=[
            smem, smem,
            pl.BlockSpec((CHUNK, n_k), lambda b, j: (0, 0)),
            pl.BlockSpec((qb, ATTN_DIM), lambda b, j: (b * nb + j, ZC_Q // ATTN_DIM)),
            pl.BlockSpec((qb, KV_DIM), lambda b, j: (b * nb + jnp.maximum(j - 1, 0), kblk)),
            pl.BlockSpec((qb, KV_DIM), lambda b, j: (b * nb + j, kblk)),
            pl.BlockSpec((qb, KV_DIM), lambda b, j: (b * nb + jnp.maximum(j - 1, 0), vblk)),
            pl.BlockSpec((qb, KV_DIM), lambda b, j: (b * nb + j, vblk)),
        ],
        out_specs=pl.BlockSpec((qb, ATTN_DIM), lambda b, j: (b * nb + j, 0)),
        out_shape=jax.ShapeDtypeStruct((batch * seq, ATTN_DIM), BF16),
        scratch_shapes=[pltpu.VMEM((N_KV_HEADS, GROUP * CHUNK, n_k), F32)],
        compiler_params=pltpu.CompilerParams(dimension_semantics=("arbitrary", "arbitrary")),
        name="attn_prompt",
    )(table, sinks, bucket, zz, zz, zz, zz, zz)


def _attn_sample_kernel(table_ref, sinks_ref, bucket_ref, q_ref, k_ref, v_ref, o_ref, bias_ref, *, n_q):
    @pl.when(pl.program_id(0) == 0)
    def _():
        _build_bias(bias_ref, bucket_ref, table_ref, n_q)

    outs = _attend(q_ref[...], k_ref[0], v_ref[0], bias_ref, sinks_ref, n_q, None)
    _store_heads(o_ref, 0, n_q, outs)


def _attn_sample(zz, k_ext, v_ext, batch, n_q, table, sinks):
    n_k = WINDOW + n_q
    bucket = jnp.asarray(_t5_buckets(n_q, n_k))
    smem = pl.BlockSpec(memory_space=pltpu.SMEM)
    return pl.pallas_call(
        functools.partial(_attn_sample_kernel, n_q=n_q),
        grid=(batch,),
        in_specs=[
            smem, smem,
            pl.BlockSpec((n_q, n_k), lambda b: (0, 0)),
            pl.BlockSpec((n_q, ATTN_DIM), lambda b: (b, ZC_Q // ATTN_DIM)),
            pl.BlockSpec((1, n_k, KV_DIM), lambda b: (b, 0, 0)),
            pl.BlockSpec((1, n_k, KV_DIM), lambda b: (b, 0, 0)),
        ],
        out_specs=pl.BlockSpec((n_q, ATTN_DIM), lambda b: (b, 0)),
        out_shape=jax.ShapeDtypeStruct((batch * n_q, ATTN_DIM), BF16),
        scratch_shapes=[pltpu.VMEM((N_KV_HEADS, GROUP * n_q, n_k), F32)],
        compiler_params=pltpu.CompilerParams(dimension_semantics=("arbitrary",)),
        name="attn_sample",
    )(table, sinks, bucket, zz, k_ext, v_ext)


_MERGE_INPUTS = 15


def _merge_kernel(*refs, tm, seq_rows, tiles_per_seq):
    (o_ref, ga_ref, gc_ref, cb_ref, u_ref, h0_ref, h1_ref, x_ref, cw_ref, woa_ref, woc_ref, wout_ref,
     gf_ref, wr_ref, br_ref) = refs[:_MERGE_INPUTS]
    h_ref, hn_ref, idx_ref, wts_ref = refs[-4:]
    u = u_ref[...]
    row = lax.broadcasted_iota(I32, (tm, 1), 0)
    if seq_rows is None:
        at_start = (pl.program_id(0) % tiles_per_seq) == 0
        hist0 = jnp.where(at_start, 0.0, h0_ref[6:7, :])
        hist1 = jnp.where(at_start, 0.0, h1_ref[7:8, :])
        pos = row
    else:
        hist0 = h0_ref[...]
        hist1 = h1_ref[...]
        pos = row % seq_rows
    r1 = pltpu.roll(u, 1, 0)
    r2 = pltpu.roll(u, 2, 0)
    u1 = jnp.where(pos == 0, hist1, r1)
    u2 = jnp.where(pos == 0, hist0, jnp.where(pos == 1, hist1, r2))
    cw = cw_ref[...]
    y = cw[0:1, :] * u2
    y = y + cw[1:2, :] * u1
    y = y + cw[2:3, :] * u
    c = (cb_ref[...] * y).astype(BF16)
    t = ga_ref[...] * jnp.dot(o_ref[...], woa_ref[...], preferred_element_type=F32)
    t = t + gc_ref[...] * jnp.dot(c, woc_ref[...], preferred_element_type=F32)
    h = x_ref[...] + jnp.dot(t.astype(BF16), wout_ref[...], preferred_element_type=F32)
    h_ref[...] = h
    ms = jnp.mean(h * h, axis=-1, keepdims=True)
    hn = (h * lax.rsqrt(ms + NORM_EPS)) * gf_ref[...]
    hn_ref[...] = hn
    logits = lax.dot_general(wr_ref[...], hn.astype(BF16), (((1,), (1,)), ((), ())),
                             preferred_element_type=F32) + br_ref[...]
    erow = lax.broadcasted_iota(I32, logits.shape, 0).astype(F32)
    vals, idxs = [], []
    for _ in range(TOP_K):
        mx = jnp.max(logits, axis=0, keepdims=True)
        ix = jnp.min(jnp.where(logits == mx, erow, float(N_EXPERTS)), axis=0, keepdims=True)
        vals.append(mx)
        idxs.append(ix)
        logits = jnp.where(erow == ix, -jnp.inf, logits)
    es = [jnp.exp(v - vals[0]) for v in vals]
    tot = es[0]
    for e in es[1:]:
        tot = tot + e
    idx_ref[...] = jnp.concatenate(idxs, axis=0).astype(I32)
    wts_ref[...] = jnp.concatenate([e / tot for e in es], axis=0)


def _merge(o, zz, x, hist0, hist1, conv_w, woa, woc, wout, g_ffn, wr_t, br, tm, seq_rows, seq_len,
           hn_rows, hn_row0, hn_prev):
    t = x.shape[0]
    d = D_MODEL
    assert hn_row0 % tm == 0
    extra_specs, extra_args, aliases = [], [], {}
    if hn_prev is not None:
        extra_specs, extra_args, aliases = [pl.BlockSpec(memory_space=pl.ANY)], [hn_prev], {_MERGE_INPUTS: 1}
    full = lambda shape: pl.BlockSpec(shape, lambda i: (0,) * len(shape))
    if seq_rows is None:
        tiles_per_seq = seq_len // tm
        hist_spec = pl.BlockSpec((8, CONV_DIM), lambda i: (jnp.maximum(i * (tm // 8) - 1, 0), ZC_U // CONV_DIM))
        h0_spec = h1_spec = hist_spec
        h0_arg = h1_arg = zz
    else:
        tiles_per_seq = None
        h0_spec = h1_spec = pl.BlockSpec((tm, CONV_DIM), lambda i: (i, 0))
        h0_arg, h1_arg = hist0, hist1
    kern = functools.partial(_merge_kernel, tm=tm, seq_rows=seq_rows, tiles_per_seq=tiles_per_seq)
    vmem = (2 * tm * (ATTN_DIM * 2 + 5 * d * 4 + 4 * CONV_DIM * 4) + 2 * 2 * (2 * ATTN_DIM * d + d * d)
            + 10 * tm * d * 4 + (8 << 20))
    return pl.pallas_call(
        kern,
        grid=(t // tm,),
        in_specs=[
            pl.BlockSpec((tm, ATTN_DIM), lambda i: (i, 0)),
            pl.BlockSpec((tm, d), lambda i: (i, ZC_GA // d)),
            pl.BlockSpec((tm, d), lambda i: (i, ZC_GC // d)),
            pl.BlockSpec((tm, CONV_DIM), lambda i: (i, ZC_CB // CONV_DIM)),
            pl.BlockSpec((tm, CONV_DIM), lambda i: (i, ZC_U // CONV_DIM)),
            h0_spec, h1_spec,
            pl.BlockSpec((tm, d), lambda i: (i, 0)),
            full((3, CONV_DIM)),
            full((ATTN_DIM, d)), full((CONV_DIM, d)), full((d, d)),
            full((1, d)), full((N_EXPERTS, d)), full((N_EXPERTS, 1)),
        ] + extra_specs,
        out_specs=[
            pl.BlockSpec((tm, d), lambda i: (i, 0)),
            pl.BlockSpec((tm, d), lambda i: (hn_row0 // tm + i, 0)),
            pl.BlockSpec((TOP_K, tm), lambda i: (0, i)),
            pl.BlockSpec((TOP_K, tm), lambda i: (0, i)),
        ],
        out_shape=[
            jax.ShapeDtypeStruct((t, d), F32),
            jax.ShapeDtypeStruct((hn_rows, d), F32),
            jax.ShapeDtypeStruct((TOP_K, t), I32),
            jax.ShapeDtypeStruct((TOP_K, t), F32),
        ],
        input_output_aliases=aliases,
        compiler_params=pltpu.CompilerParams(dimension_semantics=("arbitrary",), vmem_limit_bytes=_vmem_limit(vmem)),
        name="merge_prompt" if seq_rows is None else "merge_sample",
    )(o, zz, zz, zz, zz, h0_arg, h1_arg, x, conv_w, woa, woc, wout, g_ffn.reshape(1, d), wr_t,
      br.reshape(N_EXPERTS, 1), *extra_args)


ROUTE_TILE = 384


def _route_kernel(idx_ref, dest_ref, cnt_ref, cnt_acc, carry, pst, tri_ref):
    p = pl.program_id(0)
    j = pl.program_id(1)
    tr = idx_ref.shape[1]
    idx = idx_ref[...]
    erow = lax.broadcasted_iota(I32, (N_EXPERTS, tr), 0)
    masks = [(erow == idx[k:k + 1, :]).astype(F32) for k in range(TOP_K)]
    mtot = masks[0]
    for m in masks[1:]:
        mtot = mtot + m
    tile_cnt = jnp.sum(mtot, axis=1, keepdims=True)

    @pl.when((p == 0) & (j == 0))
    def _():
        cnt_acc[...] = jnp.zeros_like(cnt_acc)

    @pl.when(p == 0)
    def _():
        cnt_acc[...] += jnp.broadcast_to(tile_cnt, cnt_acc.shape)

    @pl.when((p == 1) & (j == 0))
    def _():
        cnt = cnt_acc[:, 0:1]
        padded = jnp.floor((cnt + float(ROW_BLK - 1)) / float(ROW_BLK)) * float(ROW_BLK)
        r = lax.broadcasted_iota(I32, (N_EXPERTS, N_EXPERTS), 0)
        c = lax.broadcasted_iota(I32, (N_EXPERTS, N_EXPERTS), 1)
        rowv = jnp.sum(jnp.where(r == c, padded, 0.0), axis=0, keepdims=True)
        start = jnp.sum(jnp.where(c < r, rowv, 0.0), axis=1, keepdims=True)
        pst[...] = jnp.broadcast_to(start, pst.shape)
        carry[...] = jnp.zeros_like(carry)
        a = lax.broadcasted_iota(I32, (tr, tr), 0)
        b = lax.broadcasted_iota(I32, (tr, tr), 1)
        tri_ref[...] = (a < b).astype(BF16)

    @pl.when(p == 1)
    def _():
        excl = jnp.dot(mtot.astype(BF16), tri_ref[...], preferred_element_type=F32)
        val = pst[:, 0:1] + carry[:, 0:1] + excl
        dest = jnp.concatenate([jnp.sum(m * val, axis=0, keepdims=True) for m in masks], axis=0)
        dest_ref[...] = dest.astype(I32)
        carry[...] += jnp.broadcast_to(tile_cnt, carry.shape)
        cnt_ref[...] = cnt_acc[...].astype(I32)


def _route(idx_t):
    t = idx_t.shape[1]
    tr = ROUTE_TILE
    return pl.pallas_call(
        _route_kernel,
        grid=(2, t // tr),
        in_specs=[pl.BlockSpec((TOP_K, tr), lambda p, j: (0, j))],
        out_specs=[
            pl.BlockSpec((TOP_K, tr), lambda p, j: (0, j * p)),
            pl.BlockSpec((N_EXPERTS, LANES), lambda p, j: (0, 0)),
        ],
        out_shape=[jax.ShapeDtypeStruct((TOP_K, t), I32), jax.ShapeDtypeStruct((N_EXPERTS, LANES), I32)],
        scratch_shapes=[pltpu.VMEM((N_EXPERTS, LANES), F32), pltpu.VMEM((N_EXPERTS, LANES), F32),
                        pltpu.VMEM((N_EXPERTS, LANES), F32), pltpu.VMEM((tr, tr), BF16)],
        compiler_params=pltpu.CompilerParams(dimension_semantics=("arbitrary", "arbitrary")),
        name="route",
    )(idx_t)


DISPATCH_TILE = 384
DMA_UNROLL = 8


def _dispatch_kernel(dest_ref, hn_ref, xs_hbm, sem):
    tt = hn_ref.shape[0]

    def issue(t, c):
        for k in range(TOP_K):
            dst = xs_hbm.at[pl.ds(dest_ref[0, TOP_K * t + k], 1)]
            pltpu.make_async_copy(hn_ref.at[pl.ds(t, 1)], dst, sem).start(priority=k % 2)
        return c

    lax.fori_loop(0, tt, issue, 0, unroll=DMA_UNROLL)

    def drain(t, c):
        for k in range(TOP_K):
            pltpu.make_async_copy(hn_ref.at[pl.ds(0, 1)], xs_hbm.at[pl.ds(0, 1)], sem).wait()
        return c

    lax.fori_loop(0, tt, drain, 0, unroll=DMA_UNROLL)


def _dispatch(dest, hn, n_slots):
    t, d = hn.shape
    tt = DISPATCH_TILE
    return pl.pallas_call(
        _dispatch_kernel,
        grid=(t // tt,),
        in_specs=[pl.BlockSpec((1, TOP_K * tt), lambda i: (0, i), memory_space=pltpu.SMEM),
                  pl.BlockSpec((tt, d), lambda i: (i, 0))],
        out_specs=pl.BlockSpec(memory_space=pl.ANY),
        out_shape=jax.ShapeDtypeStruct((n_slots, d), hn.dtype),
        scratch_shapes=[pltpu.SemaphoreType.DMA(())],
        compiler_params=pltpu.CompilerParams(dimension_semantics=("arbitrary",)),
        name="dispatch",
    )(dest, hn)


def _matmul_units(n_blocks):
    units, b = [], 0
    for size in (4, 2, 1):
        while n_blocks - b >= size:
            units.append((b, size))
            b += size
    return units


def _expert_kernel(we_ref, ws_ref, wnb_ref, wvalid_ref, wact_ref,
                   xs_hbm, wg_ref, wu_ref, wd_ref, bg_ref, bu_ref, bd_ref, ys_hbm,
                   x_sb, acc, wgu_bf, wd_bf, stage, sem_in, sem_out):
    del we_ref, wact_ref
    w = pl.program_id(0)
    f = pl.program_id(1)
    n_items = wnb_ref.shape[0]
    last_f = N_FF_TILES - 1
    slot = w % 2
    nb = wnb_ref[w]
    row = lax.broadcasted_iota(I32, (ROW_BLK, 1), 0)

    def x_copy(item, j):
        src = xs_hbm.at[pl.ds(pl.multiple_of(ws_ref[item] + j * ROW_BLK, ROW_BLK), ROW_BLK)]
        return pltpu.make_async_copy(src, stage, sem_in)

    def x_cast(item, j, sl):
        r0 = pl.multiple_of(j * ROW_BLK, ROW_BLK)
        x_sb[sl, pl.ds(r0, ROW_BLK), :] = jnp.where(row + r0 < wvalid_ref[item], stage[...], 0.0).astype(BF16)

    def y_copy(item, j, sl):
        r0 = pl.multiple_of(j * ROW_BLK, ROW_BLK)
        dst = ys_hbm.at[pl.ds(pl.multiple_of(ws_ref[item] + r0, ROW_BLK), ROW_BLK)]
        return pltpu.make_async_copy(acc.at[sl, pl.ds(r0, ROW_BLK)], dst, sem_out.at[sl])

    def y_wait_all(item, sl):
        def body(j, c):
            y_copy(item, j, sl).wait()
            return c

        lax.fori_loop(0, wnb_ref[item], body, 0)

    @pl.when((w == 0) & (f == 0))
    def _first_rows():
        def body(j, c):
            cp = x_copy(0, j)
            cp.start()
            cp.wait()
            x_cast(0, j, 0)
            return c

        lax.fori_loop(0, wnb_ref[0], body, 0)

    nxt = jnp.minimum(w + 1, n_items - 1)
    prefetch = (w + 1 < n_items) & (f < wnb_ref[nxt])

    @pl.when(prefetch)
    def _():
        x_copy(nxt, f).start()

    @pl.when((f == 0) & (w >= 2))
    def _reclaim_acc():
        y_wait_all(jnp.maximum(w - 2, 0), slot)

    @pl.when(nb > 0)
    def _compute():
        bg = bg_ref[0]
        bu = bu_ref[0]
        bd = bd_ref[0]

        def cast_weights():
            wgu_bf[:, :FF_TILE] = wg_ref[0].astype(BF16)
            wgu_bf[:, FF_TILE:] = wu_ref[0].astype(BF16)
            wd_bf[...] = wd_ref[0].astype(BF16)

        def update(r0, rows, first):
            x = x_sb[slot, r0:r0 + rows, :]
            gu = jnp.dot(x, wgu_bf[...], preferred_element_type=F32)
            g = jnp.minimum(gu[:, :FF_TILE] + bg, SWIGLU_LIMIT)
            u = jnp.clip(gu[:, FF_TILE:] + bu, -SWIGLU_LIMIT, SWIGLU_LIMIT)
            hh = (u + 1.0) * (g * jax.nn.sigmoid(SWIGLU_ALPHA * g))
            part = jnp.dot(hh.astype(BF16), wd_bf[...], preferred_element_type=F32)
            if first:
                acc[slot, r0:r0 + rows, :] = part + bd
            else:
                acc[slot, r0:r0 + rows, :] += part

        def sweep(first):
            for n in range(1, SUPER_BLKS + 1):
                @pl.when(nb == n)
                def _(n=n):
                    cast_weights()
                    for b0, size in _matmul_units(n):
                        update(b0 * ROW_BLK, size * ROW_BLK, first)

        @pl.when(f == 0)
        def _():
            sweep(True)

        @pl.when(f > 0)
        def _():
            sweep(False)

    @pl.when((f == last_f) & (nb > 0))
    def _store():
        def body(j, c):
            y_copy(w, j, slot).start()
            return c

        lax.fori_loop(0, nb, body, 0)

    @pl.when(prefetch)
    def _():
        x_copy(nxt, f).wait()
        x_cast(nxt, f, 1 - slot)

    @pl.when((w == n_items - 1) & (f == last_f))
    def _drain():
        y_wait_all(jnp.maximum(w - 1, 0), 1 - slot)
        y_wait_all(w, slot)


def _experts(xs, tables, w_gate_up, b_gate_up, w_down, b_down):
    n_slots, d = xs.shape
    n_items = tables[0].shape[0]
    tf = FF_TILE
    nf = N_FF_TILES

    def ff(f, act, w):
        return f * act[w] + (nf - 1) * (1 - act[w])

    grid_spec = pltpu.PrefetchScalarGridSpec(
        num_scalar_prefetch=5,
        grid=(n_items, nf),
        in_specs=[
            pl.BlockSpec(memory_space=pl.ANY),
            pl.BlockSpec((1, d, tf), lambda w, f, we, ws, wnb, wv, act: (we[w], 0, ff(f, act, w))),
            pl.BlockSpec((1, d, tf), lambda w, f, we, ws, wnb, wv, act: (we[w], 0, nf + ff(f, act, w))),
            pl.BlockSpec((1, tf, d), lambda w, f, we, ws, wnb, wv, act: (we[w], ff(f, act, w), 0)),
            pl.BlockSpec((1, 1, tf), lambda w, f, we, ws, wnb, wv, act: (we[w], 0, ff(f, act, w))),
            pl.BlockSpec((1, 1, tf), lambda w, f, we, ws, wnb, wv, act: (we[w], 0, nf + ff(f, act, w))),
            pl.BlockSpec((1, 1, d), lambda w, f, we, ws, wnb, wv, act: (we[w], 0, 0)),
        ],
        out_specs=pl.BlockSpec(memory_space=pl.ANY),
        scratch_shapes=[
            pltpu.VMEM((2, SUPER_ROWS, d), BF16),
            pltpu.VMEM((2, SUPER_ROWS, d), F32),
            pltpu.VMEM((d, 2 * tf), BF16), pltpu.VMEM((tf, d), BF16),
            pltpu.VMEM((ROW_BLK, d), F32),
            pltpu.SemaphoreType.DMA(()), pltpu.SemaphoreType.DMA((2,)),
        ],
    )
    vmem = (2 * SUPER_ROWS * d * 6 + 2 * 3 * d * tf * 4 + 3 * d * tf * 2 + ROW_BLK * d * 4
            + 6 * ROW_BLK * d * 4 + (6 << 20))
    return pl.pallas_call(
        _expert_kernel,
        grid_spec=grid_spec,
        out_shape=jax.ShapeDtypeStruct((n_slots, d), F32),
        compiler_params=pltpu.CompilerParams(
            dimension_semantics=("arbitrary", "arbitrary"), vmem_limit_bytes=_vmem_limit(vmem)),
        name="experts",
    )(*tables, xs, w_gate_up, w_gate_up, w_down, b_gate_up.reshape(N_EXPERTS, 1, 2 * D_FF),
      b_gate_up.reshape(N_EXPERTS, 1, 2 * D_FF), b_down.reshape(N_EXPERTS, 1, d))


def _work_tables(counts, n_items):
    padded = (counts + ROW_BLK - 1) // ROW_BLK * ROW_BLK
    pstart = jnp.cumsum(padded) - padded
    items_e = (padded + SUPER_ROWS - 1) // SUPER_ROWS
    item_end = jnp.cumsum(items_e)
    item_start = item_end - items_e
    w = jnp.arange(n_items, dtype=I32)
    total = item_end[-1]
    active = (w < total).astype(I32)
    wl = jnp.minimum(w, total - 1)
    e = jnp.minimum(jnp.searchsorted(item_end, wl, side='right'), N_EXPERTS - 1).astype(I32)
    local = wl - item_start[e]
    start = pstart[e] + local * SUPER_ROWS
    rows = jnp.minimum(SUPER_ROWS, padded[e] - local * SUPER_ROWS)
    valid = jnp.clip(counts[e] - local * SUPER_ROWS, 0, rows)
    nb = (rows // ROW_BLK) * active
    return (e.astype(I32), start.astype(I32), nb.astype(I32), valid.astype(I32), active)


COMBINE_TILE = 128


def _combine_kernel(dest_ref, dest_next_ref, wts_ref, h_ref, g_ref, ys_hbm, out_ref, buf, sem, *, n):
    i = pl.program_id(0)
    slot = i % 2
    tt = h_ref.shape[0]

    def issue(dref, sl):
        def body(t, c):
            for k in range(TOP_K):
                src = ys_hbm.at[pl.ds(dref[0, TOP_K * t + k], 1)]
                pltpu.make_async_copy(src, buf.at[sl, k, pl.ds(t, 1)], sem.at[sl]).start(priority=k % 2)
            return c

        lax.fori_loop(0, tt, body, 0, unroll=DMA_UNROLL)

    @pl.when(i == 0)
    def _():
        issue(dest_ref, 0)

    if n > 1:
        @pl.when(i + 1 < n)
        def _():
            issue(dest_next_ref, 1 - slot)

    def drain(t, c):
        for k in range(TOP_K):
            pltpu.make_async_copy(ys_hbm.at[pl.ds(0, 1)], buf.at[slot, k, pl.ds(0, 1)], sem.at[slot]).wait()
        return c

    lax.fori_loop(0, tt, drain, 0, unroll=DMA_UNROLL)
    wts = wts_ref[...]
    moe = wts[:, 0:1] * buf[slot, 0]
    for k in range(1, TOP_K):
        moe = moe + wts[:, k:k + 1] * buf[slot, k]
    y = h_ref[...] + moe
    ms = jnp.mean(y * y, axis=-1, keepdims=True)
    out_ref[...] = (y * lax.rsqrt(ms + NORM_EPS)) * g_ref[...]


def _combine(dest, wts, h, g_final, ys):
    t, d = h.shape
    tt = COMBINE_TILE
    n = t // tt
    return pl.pallas_call(
        functools.partial(_combine_kernel, n=n),
        grid=(n,),
        in_specs=[
            pl.BlockSpec((1, TOP_K * tt), lambda i: (0, i), memory_space=pltpu.SMEM),
            pl.BlockSpec((1, TOP_K * tt), lambda i: (0, jnp.minimum(i + 1, n - 1)), memory_space=pltpu.SMEM),
            pl.BlockSpec((tt, TOP_K), lambda i: (i, 0)),
            pl.BlockSpec((tt, d), lambda i: (i, 0)),
            pl.BlockSpec((1, d), lambda i: (0, 0)),
            pl.BlockSpec(memory_space=pl.ANY),
        ],
        out_specs=pl.BlockSpec((tt, d), lambda i: (i, 0)),
        out_shape=jax.ShapeDtypeStruct((t, d), F32),
        scratch_shapes=[pltpu.VMEM((2, TOP_K, tt, d), F32), pltpu.SemaphoreType.DMA((2,))],
        compiler_params=pltpu.CompilerParams(dimension_semantics=("arbitrary",)),
        name="combine",
    )(dest, dest, wts, h, g_final.reshape(1, d), ys)


def kernel(x_prompt, x_sample, cache_attn_k, cache_attn_v, state_conv, rel_bias_table, norm_mix_g, w_in, b_gate,
           attn_sinks, conv_w, w_o_attn, w_o_conv, w_out, norm_ffn_g, w_router, b_router, w_gate_up, b_gate_up,
           w_down, b_down, norm_final_g):
    batch, seq, d = x_prompt.shape
    dbatch, dseq, _ = x_sample.shape
    assert norm_mix_g.shape[0] == 1 and d == D_MODEL
    t_p, t_s = batch * seq, dbatch * dseq
    xp = x_prompt.reshape(t_p, d)
    xs = x_sample.reshape(t_s, d)

    woa = w_o_attn[0].astype(BF16)
    woc = w_o_conv[0].astype(BF16)
    wout = w_out[0].astype(BF16)
    wr_t = w_router[0].T.astype(BF16)

    w_in_bf = w_in[0].astype(BF16)
    zz_p = _in_proj(xp, norm_mix_g[0], w_in_bf, b_gate[0], tm=1024)
    zz_s = _in_proj(xs, norm_mix_g[0], w_in_bf, b_gate[0], tm=t_s)

    o_p = _attn_prompt(zz_p, batch, seq, rel_bias_table, attn_sinks[0])
    k_new = zz_s[:, ZC_K:ZC_K + KV_DIM].reshape(dbatch, dseq, KV_DIM)
    v_new = zz_s[:, ZC_V:ZC_V + KV_DIM].reshape(dbatch, dseq, KV_DIM)
    k_ext = jnp.concatenate([cache_attn_k[0].reshape(dbatch, WINDOW, KV_DIM), k_new], axis=1)
    v_ext = jnp.concatenate([cache_attn_v[0].reshape(dbatch, WINDOW, KV_DIM), v_new], axis=1)
    o_s = _attn_sample(zz_s, k_ext, v_ext, dbatch, dseq, rel_bias_table, attn_sinks[0])

    hist0 = jnp.repeat(state_conv[0][:, 0, :], dseq, axis=0)
    hist1 = jnp.repeat(state_conv[0][:, 1, :], dseq, axis=0)
    merge_args = (conv_w[0], woa, woc, wout, norm_ffn_g[0], wr_t, b_router[0])
    t_all = t_p + t_s
    h_p, hn, idx_p, wts_p = _merge(o_p, zz_p, xp, None, None, *merge_args, tm=256, seq_rows=None, seq_len=seq,
                                   hn_rows=t_all, hn_row0=0, hn_prev=None)
    h_s, hn, idx_s, wts_s = _merge(o_s, zz_s, xs, hist0, hist1, *merge_args, tm=t_s, seq_rows=dseq, seq_len=dseq,
                                   hn_rows=t_all, hn_row0=t_p, hn_prev=hn)

    idx_t = jnp.concatenate([idx_p, idx_s], axis=1)
    wts = jnp.concatenate([wts_p, wts_s], axis=1).T
    n_assign = t_all * TOP_K
    n_slots = (n_assign + N_EXPERTS * (ROW_BLK - 1) + ROW_BLK - 1) // ROW_BLK * ROW_BLK
    n_items = N_EXPERTS + n_slots // SUPER_ROWS

    dest, cnt = _route(idx_t)
    tables = _work_tables(cnt[:, 0], n_items)
    dest = dest.T.reshape(1, -1)
    xs_sorted = _dispatch(dest, hn, n_slots)
    ys = _experts(xs_sorted, tables, w_gate_up[0], b_gate_up[0], w_down[0], b_down[0])
    y_p = _combine(dest[:, :TOP_K * t_p], wts[:t_p], h_p, norm_final_g, ys)
    y_s = _combine(dest[:, TOP_K * t_p:], wts[t_p:], h_s, norm_final_g, ys)

    zz_p3 = zz_p.reshape(batch, seq, Z_DIM)

    def kv_tail(c0):
        return zz_p3[:, seq - WINDOW:, c0:c0 + KV_DIM].reshape(batch, WINDOW, N_KV_HEADS, HEAD_DIM)[None]

    u_s = zz_s[:, ZC_U:ZC_U + CONV_DIM].reshape(dbatch, dseq, CONV_DIM)
    conv_hist = state_conv.shape[2]
    new_conv_p = zz_p3[:, seq - conv_hist:, ZC_U:ZC_U + CONV_DIM][None]
    new_conv_s = jnp.concatenate([state_conv[0], u_s], axis=1)[:, -conv_hist:][None]
    new_k_s = k_ext[:, -WINDOW:].reshape(dbatch, WINDOW, N_KV_HEADS, HEAD_DIM)[None]
    new_v_s = v_ext[:, -WINDOW:].reshape(dbatch, WINDOW, N_KV_HEADS, HEAD_DIM)[None]
    return (y_p.reshape(batch, seq, d), y_s.reshape(dbatch, dseq, d),
            kv_tail(ZC_K), kv_tail(ZC_V), new_conv_p, new_k_s, new_v_s, new_conv_s)
```

```python
import functools
import math

import numpy as np
import jax
import jax.numpy as jnp
from jax import lax
from jax.experimental import pallas as pl
from jax.experimental.pallas import tpu as pltpu

F32 = jnp.float32
BF16 = jnp.bfloat16
I32 = jnp.int32

D_MODEL = 2048
CHUNK = 64
HEAD_DIM = 64
N_Q_HEADS = 16
N_KV_HEADS = 4
GROUP = N_Q_HEADS // N_KV_HEADS
ATTN_DIM = N_Q_HEADS * HEAD_DIM
KV_DIM = N_KV_HEADS * HEAD_DIM
WINDOW = 128
CONV_DIM = D_MODEL // 2
N_BUCKETS = 32
MAX_DISTANCE = 128
N_EXPERTS = 32
TOP_K = 4
D_FF = D_MODEL
SWIGLU_LIMIT = 7.0
SWIGLU_ALPHA = 1.702
NORM_EPS = 1e-5
IN_DIM = ATTN_DIM + 2 * KV_DIM + 3 * CONV_DIM + 2 * D_MODEL

V7X_VMEM_BYTES = 64 * 1024 * 1024
LANES = 128

ZC_GA = 0
ZC_GC = D_MODEL
ZC_CB = 2 * D_MODEL
ZC_U = ZC_CB + CONV_DIM
ZC_Q = ZC_U + CONV_DIM
ZC_K = ZC_Q + ATTN_DIM
ZC_V = ZC_K + KV_DIM
Z_DIM = ZC_V + KV_DIM
PROJ_TN = 512

ROW_BLK = 256
SUPER_BLKS = 5
SUPER_ROWS = SUPER_BLKS * ROW_BLK
FF_TILE = 256
N_FF_TILES = D_FF // FF_TILE
assert SUPER_BLKS <= N_FF_TILES


def _vmem_limit(nbytes):
    return int(min(nbytes, V7X_VMEM_BYTES - 6 * 1024 * 1024))


_MODE_PLAIN, _MODE_STASH0, _MODE_STASH1, _MODE_MUL0, _MODE_MUL1, _MODE_GATE = range(6)


def _proj_schedule():
    t = PROJ_TN
    cc0 = (ATTN_DIM + 2 * KV_DIM + CONV_DIM) // t
    ch0 = (ATTN_DIM + 2 * KV_DIM + 2 * CONV_DIM) // t
    cb0 = (ATTN_DIM + 2 * KV_DIM) // t
    ga0 = (ATTN_DIM + 2 * KV_DIM + 3 * CONV_DIM) // t
    steps = [
        (cc0, ZC_U // t, 0, _MODE_STASH0),
        (cc0 + 1, ZC_U // t, 0, _MODE_STASH1),
        (ch0, ZC_U // t, 0, _MODE_MUL0),
        (ch0 + 1, ZC_U // t + 1, 0, _MODE_MUL1),
        (0, ZC_Q // t, 0, _MODE_PLAIN),
        (1, ZC_Q // t + 1, 0, _MODE_PLAIN),
        (2, ZC_K // t, 0, _MODE_PLAIN),
        (cb0, ZC_CB // t, 0, _MODE_PLAIN),
        (cb0 + 1, ZC_CB // t + 1, 0, _MODE_PLAIN),
    ]
    for j in range(2 * D_MODEL // t):
        steps.append((ga0 + j, j, j, _MODE_GATE))
    return np.asarray(steps, dtype=np.int32)


def _in_proj_kernel(wt_ref, ot_ref, bt_ref, md_ref, x_ref, g_ref, w_ref, b_ref, o_ref, xn_ref, stash_ref):
    del wt_ref, ot_ref, bt_ref
    n = pl.program_id(1)

    def product():
        return jnp.dot(xn_ref[...], w_ref[...], preferred_element_type=F32)

    md = md_ref[n]

    @pl.when(md == _MODE_PLAIN)
    def _():
        o_ref[...] = product()

    @pl.when(md == _MODE_STASH0)
    def _():
        x = x_ref[...]
        ms = jnp.mean(x * x, axis=-1, keepdims=True)
        xn_ref[...] = ((x * lax.rsqrt(ms + NORM_EPS)) * g_ref[...]).astype(BF16)
        stash_ref[0] = product()

    @pl.when(md == _MODE_STASH1)
    def _():
        stash_ref[1] = product()

    @pl.when(md == _MODE_MUL0)
    def _():
        o_ref[...] = stash_ref[0] * product()

    @pl.when(md == _MODE_MUL1)
    def _():
        o_ref[...] = stash_ref[1] * product()

    @pl.when(md == _MODE_GATE)
    def _():
        o_ref[...] = jax.nn.sigmoid(product() + b_ref[...])


def _in_proj(x, g_mix, w_in, b_gate, tm):
    t, d = x.shape
    sched = _proj_schedule()
    n_steps = sched.shape[0]
    grid_spec = pltpu.PrefetchScalarGridSpec(
        num_scalar_prefetch=4,
        grid=(t // tm, n_steps),
        in_specs=[
            pl.BlockSpec((tm, d), lambda m, n, wt, ot, bt, md: (m, 0)),
            pl.BlockSpec((1, d), lambda m, n, wt, ot, bt, md: (0, 0)),
            pl.BlockSpec((d, PROJ_TN), lambda m, n, wt, ot, bt, md: (0, wt[n])),
            pl.BlockSpec((1, PROJ_TN), lambda m, n, wt, ot, bt, md: (0, bt[n])),
        ],
        out_specs=pl.BlockSpec((tm, PROJ_TN), lambda m, n, wt, ot, bt, md: (m, ot[n])),
        scratch_shapes=[pltpu.VMEM((tm, d), BF16), pltpu.VMEM((2, tm, PROJ_TN), F32)],
    )
    vmem = 2 * tm * d * 4 + tm * d * 2 + 2 * d * PROJ_TN * 2 + 6 * tm * PROJ_TN * 4 + (8 << 20)
    return pl.pallas_call(
        _in_proj_kernel,
        grid_spec=grid_spec,
        out_shape=jax.ShapeDtypeStruct((t, Z_DIM), F32),
        compiler_params=pltpu.CompilerParams(
            dimension_semantics=("arbitrary", "arbitrary"), vmem_limit_bytes=_vmem_limit(vmem)),
        name="in_proj",
    )(jnp.asarray(sched[:, 0]), jnp.asarray(sched[:, 1]), jnp.asarray(sched[:, 2]), jnp.asarray(sched[:, 3]),
      x, g_mix.reshape(1, d), w_in, b_gate.reshape(1, 2 * D_MODEL))


def _t5_buckets(n_q, n_k):
    half = N_BUCKETS // 2
    max_exact = half // 2
    rel = (np.arange(n_k, dtype=np.int32)[None, :] - WINDOW) - np.arange(n_q, dtype=np.int32)[:, None]
    ret = np.where(rel > 0, half, 0)
    n = np.abs(rel)
    nf = np.maximum(n, 1).astype(np.float32)
    large = max_exact + (np.log(nf / np.float32(max_exact)) / np.float32(math.log(MAX_DISTANCE / max_exact))
                         * np.float32(half - max_exact)).astype(np.int32)
    large = np.minimum(large, half - 1)
    return (ret + np.where(n < max_exact, n, large)).astype(np.int32)


def _build_bias(bias_ref, bucket_ref, table_ref, n_q):
    bucket = bucket_ref[...]
    for kvh in range(N_KV_HEADS):
        for g in range(GROUP):
            h = kvh * GROUP + g
            mat = jnp.zeros(bucket.shape, F32)
            for b in range(N_BUCKETS):
                mat = jnp.where(bucket == b, table_ref[b, h], mat)
            bias_ref[kvh, g * n_q:(g + 1) * n_q, :] = mat


def _attend(q, kwin, vwin, bias_ref, sinks_ref, n_q, invalid):
    outs = []
    grp = lax.broadcasted_iota(I32, (GROUP * n_q, 1), 0) // n_q
    for kvh in range(N_KV_HEADS):
        c0 = kvh * GROUP * HEAD_DIM
        qh = jnp.concatenate([q[:, c0 + g * HEAD_DIM:c0 + (g + 1) * HEAD_DIM] for g in range(GROUP)], axis=0)
        qh = (qh * (HEAD_DIM ** -0.5)).astype(BF16)
        kh = kwin[:, kvh * HEAD_DIM:(kvh + 1) * HEAD_DIM].astype(BF16)
        vh = vwin[:, kvh * HEAD_DIM:(kvh + 1) * HEAD_DIM].astype(BF16)
        s = lax.dot_general(qh, kh, (((1,), (1,)), ((), ())), preferred_element_type=F32) + bias_ref[kvh]
        if invalid is not None:
            s = jnp.where(invalid, -1e30, s)
        sink = jnp.zeros((GROUP * n_q, 1), F32)
        for g in range(GROUP):
            sink = jnp.where(grp == g, sinks_ref[kvh * GROUP + g], sink)
        m = jnp.maximum(jnp.max(s, axis=-1, keepdims=True), sink)
        p = jnp.exp(s - m)
        denom = jnp.sum(p, axis=-1, keepdims=True) + jnp.exp(sink - m)
        o = jnp.dot(p.astype(BF16), vh, preferred_element_type=F32) / denom
        outs.append(o)
    return outs


def _store_heads(o_ref, r0, n_q, outs):
    for kvh, o in enumerate(outs):
        for g in range(GROUP):
            c = (kvh * GROUP + g) * HEAD_DIM
            o_ref[r0:r0 + n_q, c:c + HEAD_DIM] = o[g * n_q:(g + 1) * n_q].astype(o_ref.dtype)


def _attn_prompt_kernel(table_ref, sinks_ref, bucket_ref, q_ref, kp_ref, kc_ref, vp_ref, vc_ref, o_ref, bias_ref):
    first = (pl.program_id(0) == 0) & (pl.program_id(1) == 0)

    @pl.when(first)
    def _():
        _build_bias(bias_ref, bucket_ref, table_ref, CHUNK)

    kwin = jnp.concatenate([kp_ref[...], kc_ref[...]], axis=0)
    vwin = jnp.concatenate([vp_ref[...], vc_ref[...]], axis=0)
    n_k = WINDOW + CHUNK
    seq_start = pl.program_id(1) == 0
    col = lax.broadcasted_iota(I32, (1, n_k), 1)
    for c in range(WINDOW // CHUNK):
        r0 = c * CHUNK
        invalid = seq_start & (col + r0 < WINDOW)
        outs = _attend(q_ref[r0:r0 + CHUNK, :], kwin[r0:r0 + n_k], vwin[r0:r0 + n_k], bias_ref, sinks_ref,
                       CHUNK, invalid)
        _store_heads(o_ref, r0, CHUNK, outs)


def _attn_prompt(zz, batch, seq, table, sinks):
    qb = WINDOW
    nb = seq // qb
    n_k = WINDOW + CHUNK
    bucket = jnp.asarray(_t5_buckets(CHUNK, n_k))
    smem = pl.BlockSpec(memory_space=pltpu.SMEM)
    kblk, vblk = ZC_K // KV_DIM, ZC_V // KV_DIM
    return pl.pallas_call(
        _attn_prompt_kernel,
        grid=(batch, nb),
        in_specs=[
            smem, smem,
            pl.BlockSpec((CHUNK, n_k), lambda b, j: (0, 0)),
            pl.BlockSpec((qb, ATTN_DIM), lambda b, j: (b * nb + j, ZC_Q // ATTN_DIM)),
            pl.BlockSpec((qb, KV_DIM), lambda b, j: (b * nb + jnp.maximum(j - 1, 0), kblk)),
            pl.BlockSpec((qb, KV_DIM), lambda b, j: (b * nb + j, kblk)),
            pl.BlockSpec((qb, KV_DIM), lambda b, j: (b * nb + jnp.maximum(j - 1, 0), vblk)),
            pl.BlockSpec((qb, KV_DIM), lambda b, j: (b * nb + j, vblk)),
        ],
        out_specs=pl.BlockSpec((qb, ATTN_DIM), lambda b, j: (b * nb + j, 0)),
        out_shape=jax.ShapeDtypeStruct((batch * seq, ATTN_DIM), BF16),
        scratch_shapes=[pltpu.VMEM((N_KV_HEADS, GROUP * CHUNK, n_k), F32)],
        compiler_params=pltpu.CompilerParams(dimension_semantics=("arbitrary", "arbitrary")),
        name="attn_prompt",
    )(table, sinks, bucket, zz, zz, zz, zz, zz)


def _attn_sample_kernel(table_ref, sinks_ref, bucket_ref, q_ref, k_ref, v_ref, o_ref, bias_ref, *, n_q):
    @pl.when(pl.program_id(0) == 0)
    def _():
        _build_bias(bias_ref, bucket_ref, table_ref, n_q)

    outs = _attend(q_ref[...], k_ref[0], v_ref[0], bias_ref, sinks_ref, n_q, None)
    _store_heads(o_ref, 0, n_q, outs)


def _attn_sample(zz, k_ext, v_ext, batch, n_q, table, sinks):
    n_k = WINDOW + n_q
    bucket = jnp.asarray(_t5_buckets(n_q, n_k))
    smem = pl.BlockSpec(memory_space=pltpu.SMEM)
    return pl.pallas_call(
        functools.partial(_attn_sample_kernel, n_q=n_q),
        grid=(batch,),
        in_specs=[
            smem, smem,
            pl.BlockSpec((n_q, n_k), lambda b: (0, 0)),
            pl.BlockSpec((n_q, ATTN_DIM), lambda b: (b, ZC_Q // ATTN_DIM)),
            pl.BlockSpec((1, n_k, KV_DIM), lambda b: (b, 0, 0)),
            pl.BlockSpec((1, n_k, KV_DIM), lambda b: (b, 0, 0)),
        ],
        out_specs=pl.BlockSpec((n_q, ATTN_DIM), lambda b: (b, 0)),
        out_shape=jax.ShapeDtypeStruct((batch * n_q, ATTN_DIM), BF16),
        scratch_shapes=[pltpu.VMEM((N_KV_HEADS, GROUP * n_q, n_k), F32)],
        compiler_params=pltpu.CompilerParams(dimension_semantics=("arbitrary",)),
        name="attn_sample",
    )(table, sinks, bucket, zz, k_ext, v_ext)


_MERGE_INPUTS = 15


def _merge_kernel(*refs, tm, seq_rows, tiles_per_seq):
    (o_ref, ga_ref, gc_ref, cb_ref, u_ref, h0_ref, h1_ref, x_ref, cw_ref, woa_ref, woc_ref, wout_ref,
     gf_ref, wr_ref, br_ref) = refs[:_MERGE_INPUTS]
    h_ref, hn_ref, idx_ref, wts_ref = refs[-4:]
    u = u_ref[...]
    row = lax.broadcasted_iota(I32, (tm, 1), 0)
    if seq_rows is None:
        at_start = (pl.program_id(0) % tiles_per_seq) == 0
        hist0 = jnp.where(at_start, 0.0, h0_ref[6:7, :])
        hist1 = jnp.where(at_start, 0.0, h1_ref[7:8, :])
        pos = row
    else:
        hist0 = h0_ref[...]
        hist1 = h1_ref[...]
        pos = row % seq_rows
    r1 = pltpu.roll(u, 1, 0)
    r2 = pltpu.roll(u, 2, 0)
    u1 = jnp.where(pos == 0, hist1, r1)
    u2 = jnp.where(pos == 0, hist0, jnp.where(pos == 1, hist1, r2))
    cw = cw_ref[...]
    y = cw[0:1, :] * u2
    y = y + cw[1:2, :] * u1
    y = y + cw[2:3, :] * u
    c = (cb_ref[...] * y).astype(BF16)
    t = ga_ref[...] * jnp.dot(o_ref[...], woa_ref[...], preferred_element_type=F32)
    t = t + gc_ref[...] * jnp.dot(c, woc_ref[...], preferred_element_type=F32)
    h = x_ref[...] + jnp.dot(t.astype(BF16), wout_ref[...], preferred_element_type=F32)
    h_ref[...] = h
    ms = jnp.mean(h * h, axis=-1, keepdims=True)
    hn = (h * lax.rsqrt(ms + NORM_EPS)) * gf_ref[...]
    hn_ref[...] = hn
    logits = lax.dot_general(wr_ref[...], hn.astype(BF16), (((1,), (1,)), ((), ())),
                             preferred_element_type=F32) + br_ref[...]
    erow = lax.broadcasted_iota(I32, logits.shape, 0).astype(F32)
    vals, idxs = [], []
    for _ in range(TOP_K):
        mx = jnp.max(logits, axis=0, keepdims=True)
        ix = jnp.min(jnp.where(logits == mx, erow, float(N_EXPERTS)), axis=0, keepdims=True)
        vals.append(mx)
        idxs.append(ix)
        logits = jnp.where(erow == ix, -jnp.inf, logits)
    es = [jnp.exp(v - vals[0]) for v in vals]
    tot = es[0]
    for e in es[1:]:
        tot = tot + e
    idx_ref[...] = jnp.concatenate(idxs, axis=0).astype(I32)
    wts_ref[...] = jnp.concatenate([e / tot for e in es], axis=0)


def _merge(o, zz, x, hist0, hist1, conv_w, woa, woc, wout, g_ffn, wr_t, br, tm, seq_rows, seq_len,
           hn_rows, hn_row0, hn_prev):
    t = x.shape[0]
    d = D_MODEL
    assert hn_row0 % tm == 0
    extra_specs, extra_args, aliases = [], [], {}
    if hn_prev is not None:
        extra_specs, extra_args, aliases = [pl.BlockSpec(memory_space=pl.ANY)], [hn_prev], {_MERGE_INPUTS: 1}
    full = lambda shape: pl.BlockSpec(shape, lambda i: (0,) * len(shape))
    if seq_rows is None:
        tiles_per_seq = seq_len // tm
        hist_spec = pl.BlockSpec((8, CONV_DIM), lambda i: (jnp.maximum(i * (tm // 8) - 1, 0), ZC_U // CONV_DIM))
        h0_spec = h1_spec = hist_spec
        h0_arg = h1_arg = zz
    else:
        tiles_per_seq = None
        h0_spec = h1_spec = pl.BlockSpec((tm, CONV_DIM), lambda i: (i, 0))
        h0_arg, h1_arg = hist0, hist1
    kern = functools.partial(_merge_kernel, tm=tm, seq_rows=seq_rows, tiles_per_seq=tiles_per_seq)
    vmem = (2 * tm * (ATTN_DIM * 2 + 5 * d * 4 + 4 * CONV_DIM * 4) + 2 * 2 * (2 * ATTN_DIM * d + d * d)
            + 10 * tm * d * 4 + (8 << 20))
    return pl.pallas_call(
        kern,
        grid=(t // tm,),
        in_specs=[
            pl.BlockSpec((tm, ATTN_DIM), lambda i: (i, 0)),
            pl.BlockSpec((tm, d), lambda i: (i, ZC_GA // d)),
            pl.BlockSpec((tm, d), lambda i: (i, ZC_GC // d)),
            pl.BlockSpec((tm, CONV_DIM), lambda i: (i, ZC_CB // CONV_DIM)),
            pl.BlockSpec((tm, CONV_DIM), lambda i: (i, ZC_U // CONV_DIM)),
            h0_spec, h1_spec,
            pl.BlockSpec((tm, d), lambda i: (i, 0)),
            full((3, CONV_DIM)),
            full((ATTN_DIM, d)), full((CONV_DIM, d)), full((d, d)),
            full((1, d)), full((N_EXPERTS, d)), full((N_EXPERTS, 1)),
        ] + extra_specs,
        out_specs=[
            pl.BlockSpec((tm, d), lambda i: (i, 0)),
            pl.BlockSpec((tm, d), lambda i: (hn_row0 // tm + i, 0)),
            pl.BlockSpec((TOP_K, tm), lambda i: (0, i)),
            pl.BlockSpec((TOP_K, tm), lambda i: (0, i)),
        ],
        out_shape=[
            jax.ShapeDtypeStruct((t, d), F32),
            jax.ShapeDtypeStruct((hn_rows, d), F32),
            jax.ShapeDtypeStruct((TOP_K, t), I32),
            jax.ShapeDtypeStruct((TOP_K, t), F32),
        ],
        input_output_aliases=aliases,
        compiler_params=pltpu.CompilerParams(dimension_semantics=("arbitrary",), vmem_limit_bytes=_vmem_limit(vmem)),
        name="merge_prompt" if seq_rows is None else "merge_sample",
    )(o, zz, zz, zz, zz, h0_arg, h1_arg, x, conv_w, woa, woc, wout, g_ffn.reshape(1, d), wr_t,
      br.reshape(N_EXPERTS, 1), *extra_args)


ROUTE_TILE = 384


def _route_kernel(idx_ref, dest_ref, cnt_ref, cnt_acc, carry, pst, tri_ref):
    p = pl.program_id(0)
    j = pl.program_id(1)
    tr = idx_ref.shape[1]
    idx = idx_ref[...]
    erow = lax.broadcasted_iota(I32, (N_EXPERTS, tr), 0)
    masks = [(erow == idx[k:k + 1, :]).astype(F32) for k in range(TOP_K)]
    mtot = masks[0]
    for m in masks[1:]:
        mtot = mtot + m
    tile_cnt = jnp.sum(mtot, axis=1, keepdims=True)

    @pl.when((p == 0) & (j == 0))
    def _():
        cnt_acc[...] = jnp.zeros_like(cnt_acc)

    @pl.when(p == 0)
    def _():
        cnt_acc[...] += jnp.broadcast_to(tile_cnt, cnt_acc.shape)

    @pl.when((p == 1) & (j == 0))
    def _():
        cnt = cnt_acc[:, 0:1]
        padded = jnp.floor((cnt + float(ROW_BLK - 1)) / float(ROW_BLK)) * float(ROW_BLK)
        r = lax.broadcasted_iota(I32, (N_EXPERTS, N_EXPERTS), 0)
        c = lax.broadcasted_iota(I32, (N_EXPERTS, N_EXPERTS), 1)
        rowv = jnp.sum(jnp.where(r == c, padded, 0.0), axis=0, keepdims=True)
        start = jnp.sum(jnp.where(c < r, rowv, 0.0), axis=1, keepdims=True)
        pst[...] = jnp.broadcast_to(start, pst.shape)
        carry[...] = jnp.zeros_like(carry)
        a = lax.broadcasted_iota(I32, (tr, tr), 0)
        b = lax.broadcasted_iota(I32, (tr, tr), 1)
        tri_ref[...] = (a < b).astype(BF16)

    @pl.when(p == 1)
    def _():
        excl = jnp.dot(mtot.astype(BF16), tri_ref[...], preferred_element_type=F32)
        val = pst[:, 0:1] + carry[:, 0:1] + excl
        dest = jnp.concatenate([jnp.sum(m * val, axis=0, keepdims=True) for m in masks], axis=0)
        dest_ref[...] = dest.astype(I32)
        carry[...] += jnp.broadcast_to(tile_cnt, carry.shape)
        cnt_ref[...] = cnt_acc[...].astype(I32)


def _route(idx_t):
    t = idx_t.shape[1]
    tr = ROUTE_TILE
    return pl.pallas_call(
        _route_kernel,
        grid=(2, t // tr),
        in_specs=[pl.BlockSpec((TOP_K, tr), lambda p, j: (0, j))],
        out_specs=[
            pl.BlockSpec((TOP_K, tr), lambda p, j: (0, j * p)),
            pl.BlockSpec((N_EXPERTS, LANES), lambda p, j: (0, 0)),
        ],
        out_shape=[jax.ShapeDtypeStruct((TOP_K, t), I32), jax.ShapeDtypeStruct((N_EXPERTS, LANES), I32)],
        scratch_shapes=[pltpu.VMEM((N_EXPERTS, LANES), F32), pltpu.VMEM((N_EXPERTS, LANES), F32),
                        pltpu.VMEM((N_EXPERTS, LANES), F32), pltpu.VMEM((tr, tr), BF16)],
        compiler_params=pltpu.CompilerParams(dimension_semantics=("arbitrary", "arbitrary")),
        name="route",
    )(idx_t)


DISPATCH_TILE = 384
DMA_UNROLL = 8


def _dispatch_kernel(dest_ref, hn_ref, xs_hbm, sem):
    tt = hn_ref.shape[0]

    def issue(t, c):
        for k in range(TOP_K):
            dst = xs_hbm.at[pl.ds(dest_ref[0, TOP_K * t + k], 1)]
            pltpu.make_async_copy(hn_ref.at[pl.ds(t, 1)], dst, sem).start(priority=k % 2)
        return c

    lax.fori_loop(0, tt, issue, 0, unroll=DMA_UNROLL)

    def drain(t, c):
        for k in range(TOP_K):
            pltpu.make_async_copy(hn_ref.at[pl.ds(0, 1)], xs_hbm.at[pl.ds(0, 1)], sem).wait()
        return c

    lax.fori_loop(0, tt, drain, 0, unroll=DMA_UNROLL)


def _dispatch(dest, hn, n_slots):
    t, d = hn.shape
    tt = DISPATCH_TILE
    return pl.pallas_call(
        _dispatch_kernel,
        grid=(t // tt,),
        in_specs=[pl.BlockSpec((1, TOP_K * tt), lambda i: (0, i), memory_space=pltpu.SMEM),
                  pl.BlockSpec((tt, d), lambda i: (i, 0))],
        out_specs=pl.BlockSpec(memory_space=pl.ANY),
        out_shape=jax.ShapeDtypeStruct((n_slots, d), hn.dtype),
        scratch_shapes=[pltpu.SemaphoreType.DMA(())],
        compiler_params=pltpu.CompilerParams(dimension_semantics=("arbitrary",)),
        name="dispatch",
    )(dest, hn)


MAX_UNIT_BLKS = 5


def _matmul_units(n_blocks):
    units, b = [], 0
    while b < n_blocks:
        size = min(MAX_UNIT_BLKS, n_blocks - b)
        units.append((b, size))
        b += size
    return units


def _expert_kernel(we_ref, ws_ref, wnb_ref, wvalid_ref, wact_ref,
                   xs_hbm, wg_ref, wu_ref, wd_ref, bg_ref, bu_ref, bd_ref, ys_hbm,
                   x_sb, acc, wgu_bf, wd_bf, stage, sem_in, sem_out):
    del we_ref, wact_ref
    w = pl.program_id(0)
    f = pl.program_id(1)
    n_items = wnb_ref.shape[0]
    last_f = N_FF_TILES - 1
    slot = w % 2
    nb = wnb_ref[w]
    row = lax.broadcasted_iota(I32, (ROW_BLK, 1), 0)

    def x_copy(item, j):
        src = xs_hbm.at[pl.ds(pl.multiple_of(ws_ref[item] + j * ROW_BLK, ROW_BLK), ROW_BLK)]
        return pltpu.make_async_copy(src, stage, sem_in)

    def x_cast(item, j, sl):
        r0 = pl.multiple_of(j * ROW_BLK, ROW_BLK)
        x_sb[sl, pl.ds(r0, ROW_BLK), :] = jnp.where(row + r0 < wvalid_ref[item], stage[...], 0.0).astype(BF16)

    def y_copy(item, j, sl):
        r0 = pl.multiple_of(j * ROW_BLK, ROW_BLK)
        dst = ys_hbm.at[pl.ds(pl.multiple_of(ws_ref[item] + r0, ROW_BLK), ROW_BLK)]
        return pltpu.make_async_copy(acc.at[sl, pl.ds(r0, ROW_BLK)], dst, sem_out.at[sl])

    def y_wait_all(item, sl):
        def body(j, c):
            y_copy(item, j, sl).wait()
            return c

        lax.fori_loop(0, wnb_ref[item], body, 0)

    @pl.when((w == 0) & (f == 0))
    def _first_rows():
        def body(j, c):
            cp = x_copy(0, j)
            cp.start()
            cp.wait()
            x_cast(0, j, 0)
            return c

        lax.fori_loop(0, wnb_ref[0], body, 0)

    nxt = jnp.minimum(w + 1, n_items - 1)
    prefetch = (w + 1 < n_items) & (f < wnb_ref[nxt])

    @pl.when(prefetch)
    def _():
        x_copy(nxt, f).start()

    @pl.when((f == 0) & (w >= 2))
    def _reclaim_acc():
        y_wait_all(jnp.maximum(w - 2, 0), slot)

    @pl.when(nb > 0)
    def _compute():
        bg = bg_ref[0]
        bu = bu_ref[0]
        bd = bd_ref[0]

        def cast_weights():
            wgu_bf[:, :FF_TILE] = wg_ref[0].astype(BF16)
            wgu_bf[:, FF_TILE:] = wu_ref[0].astype(BF16)
            wd_bf[...] = wd_ref[0].astype(BF16)

        def update(r0, rows, first):
            x = x_sb[slot, r0:r0 + rows, :]
            gu = jnp.dot(x, wgu_bf[...], preferred_element_type=F32)
            g = jnp.minimum(gu[:, :FF_TILE] + bg, SWIGLU_LIMIT)
            u = jnp.clip(gu[:, FF_TILE:] + bu, -SWIGLU_LIMIT, SWIGLU_LIMIT)
            hh = (u + 1.0) * (g * jax.nn.sigmoid(SWIGLU_ALPHA * g))
            part = jnp.dot(hh.astype(BF16), wd_bf[...], preferred_element_type=F32)
            if first:
                acc[slot, r0:r0 + rows, :] = part + bd
            else:
                acc[slot, r0:r0 + rows, :] += part

        def sweep(first):
            for n in range(1, SUPER_BLKS + 1):
                @pl.when(nb == n)
                def _(n=n):
                    cast_weights()
                    for b0, size in _matmul_units(n):
                        update(b0 * ROW_BLK, size * ROW_BLK, first)

        @pl.when(f == 0)
        def _():
            sweep(True)

        @pl.when(f > 0)
        def _():
            sweep(False)

    @pl.when((f == last_f) & (nb > 0))
    def _store():
        def body(j, c):
            y_copy(w, j, slot).start()
            return c

        lax.fori_loop(0, nb, body, 0)

    @pl.when(prefetch)
    def _():
        x_copy(nxt, f).wait()
        x_cast(nxt, f, 1 - slot)

    @pl.when((w == n_items - 1) & (f == last_f))
    def _drain():
        y_wait_all(jnp.maximum(w - 1, 0), 1 - slot)
        y_wait_all(w, slot)


def _experts(xs, tables, w_gate_up, b_gate_up, w_down, b_down):
    n_slots, d = xs.shape
    n_items = tables[0].shape[0]
    tf = FF_TILE
    nf = N_FF_TILES

    def ff(f, act, w):
        return f * act[w] + (nf - 1) * (1 - act[w])

    grid_spec = pltpu.PrefetchScalarGridSpec(
        num_scalar_prefetch=5,
        grid=(n_items, nf),
        in_specs=[
            pl.BlockSpec(memory_space=pl.ANY),
            pl.BlockSpec((1, d, tf), lambda w, f, we, ws, wnb, wv, act: (we[w], 0, ff(f, act, w))),
            pl.BlockSpec((1, d, tf), lambda w, f, we, ws, wnb, wv, act: (we[w], 0, nf + ff(f, act, w))),
            pl.BlockSpec((1, tf, d), lambda w, f, we, ws, wnb, wv, act: (we[w], ff(f, act, w), 0)),
            pl.BlockSpec((1, 1, tf), lambda w, f, we, ws, wnb, wv, act: (we[w], 0, ff(f, act, w))),
            pl.BlockSpec((1, 1, tf), lambda w, f, we, ws, wnb, wv, act: (we[w], 0, nf + ff(f, act, w))),
            pl.BlockSpec((1, 1, d), lambda w, f, we, ws, wnb, wv, act: (we[w], 0, 0)),
        ],
        out_specs=pl.BlockSpec(memory_space=pl.ANY),
        scratch_shapes=[
            pltpu.VMEM((2, SUPER_ROWS, d), BF16),
            pltpu.VMEM((2, SUPER_ROWS, d), F32),
            pltpu.VMEM((d, 2 * tf), BF16), pltpu.VMEM((tf, d), BF16),
            pltpu.VMEM((ROW_BLK, d), F32),
            pltpu.SemaphoreType.DMA(()), pltpu.SemaphoreType.DMA((2,)),
        ],
    )
    vmem = (2 * SUPER_ROWS * d * 6 + 2 * 3 * d * tf * 4 + 3 * d * tf * 2 + ROW_BLK * d * 4
            + 6 * ROW_BLK * d * 4 + (6 << 20))
    return pl.pallas_call(
        _expert_kernel,
        grid_spec=grid_spec,
        out_shape=jax.ShapeDtypeStruct((n_slots, d), F32),
        compiler_params=pltpu.CompilerParams(
            dimension_semantics=("arbitrary", "arbitrary"), vmem_limit_bytes=_vmem_limit(vmem)),
        name="experts",
    )(*tables, xs, w_gate_up, w_gate_up, w_down, b_gate_up.reshape(N_EXPERTS, 1, 2 * D_FF),
      b_gate_up.reshape(N_EXPERTS, 1, 2 * D_FF), b_down.reshape(N_EXPERTS, 1, d))


def _work_tables(counts, n_items):
    padded = (counts + ROW_BLK - 1) // ROW_BLK * ROW_BLK
    pstart = jnp.cumsum(padded) - padded
    items_e = (padded + SUPER_ROWS - 1) // SUPER_ROWS
    item_end = jnp.cumsum(items_e)
    item_start = item_end - items_e
    w = jnp.arange(n_items, dtype=I32)
    total = item_end[-1]
    active = (w < total).astype(I32)
    wl = jnp.minimum(w, total - 1)
    e = jnp.minimum(jnp.searchsorted(item_end, wl, side='right'), N_EXPERTS - 1).astype(I32)
    local = wl - item_start[e]
    start = pstart[e] + local * SUPER_ROWS
    rows = jnp.minimum(SUPER_ROWS, padded[e] - local * SUPER_ROWS)
    valid = jnp.clip(counts[e] - local * SUPER_ROWS, 0, rows)
    nb = (rows // ROW_BLK) * active
    return (e.astype(I32), start.astype(I32), nb.astype(I32), valid.astype(I32), active)


COMBINE_TILE = 128


def _combine_kernel(dest_ref, dest_next_ref, wts_ref, h_ref, g_ref, ys_hbm, out_ref, buf, sem, *, n):
    i = pl.program_id(0)
    slot = i % 2
    tt = h_ref.shape[0]

    def issue(dref, sl):
        def body(t, c):
            for k in range(TOP_K):
                src = ys_hbm.at[pl.ds(dref[0, TOP_K * t + k], 1)]
                pltpu.make_async_copy(src, buf.at[sl, k, pl.ds(t, 1)], sem.at[sl]).start(priority=k % 2)
            return c

        lax.fori_loop(0, tt, body, 0, unroll=DMA_UNROLL)

    def issue_unrolled(dref, sl):
        for t in range(tt):
            for k in range(TOP_K):
                src = ys_hbm.at[pl.ds(dref[0, TOP_K * t + k], 1)]
                pltpu.make_async_copy(src, buf.at[sl, k, pl.ds(t, 1)], sem.at[sl]).start(priority=k % 2)

    @pl.when(i == 0)
    def _():
        issue(dest_ref, 0)

    def drain(t, c):
        for k in range(TOP_K):
            pltpu.make_async_copy(ys_hbm.at[pl.ds(0, 1)], buf.at[slot, k, pl.ds(0, 1)], sem.at[slot]).wait()
        return c

    lax.fori_loop(0, tt, drain, 0, unroll=DMA_UNROLL)

    def finish():
        wts = wts_ref[...]
        moe = wts[:, 0:1] * buf[slot, 0]
        for k in range(1, TOP_K):
            moe = moe + wts[:, k:k + 1] * buf[slot, k]
        y = h_ref[...] + moe
        ms = jnp.mean(y * y, axis=-1, keepdims=True)
        out_ref[...] = (y * lax.rsqrt(ms + NORM_EPS)) * g_ref[...]

    if n > 1:
        @pl.when(i + 1 < n)
        def _():
            issue_unrolled(dest_next_ref, 1 - slot)
            finish()

        @pl.when(i + 1 >= n)
        def _():
            finish()
    else:
        finish()


def _combine(dest, wts, h, g_final, ys):
    t, d = h.shape
    tt = COMBINE_TILE
    n = t // tt
    return pl.pallas_call(
        functools.partial(_combine_kernel, n=n),
        grid=(n,),
        in_specs=[
            pl.BlockSpec((1, TOP_K * tt), lambda i: (0, i), memory_space=pltpu.SMEM),
            pl.BlockSpec((1, TOP_K * tt), lambda i: (0, jnp.minimum(i + 1, n - 1)), memory_space=pltpu.SMEM),
            pl.BlockSpec((tt, TOP_K), lambda i: (i, 0)),
            pl.BlockSpec((tt, d), lambda i: (i, 0)),
            pl.BlockSpec((1, d), lambda i: (0, 0)),
            pl.BlockSpec(memory_space=pl.ANY),
        ],
        out_specs=pl.BlockSpec((tt, d), lambda i: (i, 0)),
        out_shape=jax.ShapeDtypeStruct((t, d), F32),
        scratch_shapes=[pltpu.VMEM((2, TOP_K, tt, d), F32), pltpu.SemaphoreType.DMA((2,))],
        compiler_params=pltpu.CompilerParams(dimension_semantics=("arbitrary",)),
        name="combine",
    )(dest, dest, wts, h, g_final.reshape(1, d), ys)


def kernel(x_prompt, x_sample, cache_attn_k, cache_attn_v, state_conv, rel_bias_table, norm_mix_g, w_in, b_gate,
           attn_sinks, conv_w, w_o_attn, w_o_conv, w_out, norm_ffn_g, w_router, b_router, w_gate_up, b_gate_up,
           w_down, b_down, norm_final_g):
    batch, seq, d = x_prompt.shape
    dbatch, dseq, _ = x_sample.shape
    assert norm_mix_g.shape[0] == 1 and d == D_MODEL
    t_p, t_s = batch * seq, dbatch * dseq
    xp = x_prompt.reshape(t_p, d)
    xs = x_sample.reshape(t_s, d)

    woa = w_o_attn[0].astype(BF16)
    woc = w_o_conv[0].astype(BF16)
    wout = w_out[0].astype(BF16)
    wr_t = w_router[0].T.astype(BF16)

    w_in_bf = w_in[0].astype(BF16)
    zz_p = _in_proj(xp, norm_mix_g[0], w_in_bf, b_gate[0], tm=1024)
    zz_s = _in_proj(xs, norm_mix_g[0], w_in_bf, b_gate[0], tm=t_s)

    o_p = _attn_prompt(zz_p, batch, seq, rel_bias_table, attn_sinks[0])
    k_new = zz_s[:, ZC_K:ZC_K + KV_DIM].reshape(dbatch, dseq, KV_DIM)
    v_new = zz_s[:, ZC_V:ZC_V + KV_DIM].reshape(dbatch, dseq, KV_DIM)
    k_ext = jnp.concatenate([cache_attn_k[0].reshape(dbatch, WINDOW, KV_DIM), k_new], axis=1)
    v_ext = jnp.concatenate([cache_attn_v[0].reshape(dbatch, WINDOW, KV_DIM), v_new], axis=1)
    o_s = _attn_sample(zz_s, k_ext, v_ext, dbatch, dseq, rel_bias_table, attn_sinks[0])

    hist0 = jnp.repeat(state_conv[0][:, 0, :], dseq, axis=0)
    hist1 = jnp.repeat(state_conv[0][:, 1, :], dseq, axis=0)
    merge_args = (conv_w[0], woa, woc, wout, norm_ffn_g[0], wr_t, b_router[0])
    t_all = t_p + t_s
    h_p, hn, idx_p, wts_p = _merge(o_p, zz_p, xp, None, None, *merge_args, tm=256, seq_rows=None, seq_len=seq,
                                   hn_rows=t_all, hn_row0=0, hn_prev=None)
    h_s, hn, idx_s, wts_s = _merge(o_s, zz_s, xs, hist0, hist1, *merge_args, tm=t_s, seq_rows=dseq, seq_len=dseq,
                                   hn_rows=t_all, hn_row0=t_p, hn_prev=hn)

    idx_t = jnp.concatenate([idx_p, idx_s], axis=1)
    wts = jnp.concatenate([wts_p, wts_s], axis=1).T
    n_assign = t_all * TOP_K
    n_slots = (n_assign + N_EXPERTS * (ROW_BLK - 1) + ROW_BLK - 1) // ROW_BLK * ROW_BLK
    n_items = N_EXPERTS + n_slots // SUPER_ROWS

    dest, cnt = _route(idx_t)
    tables = _work_tables(cnt[:, 0], n_items)
    dest = dest.T.reshape(1, -1)
    xs_sorted = _dispatch(dest, hn, n_slots)
    ys = _experts(xs_sorted, tables, w_gate_up[0], b_gate_up[0], w_down[0], b_down[0])
    y_p = _combine(dest[:, :TOP_K * t_p], wts[:t_p], h_p, norm_final_g, ys)
    y_s = _combine(dest[:, TOP_K * t_p:], wts[t_p:], h_s, norm_final_g, ys)

    zz_p3 = zz_p.reshape(batch, seq, Z_DIM)

    def kv_tail(c0):
        return zz_p3[:, seq - WINDOW:, c0:c0 + KV_DIM].reshape(batch, WINDOW, N_KV_HEADS, HEAD_DIM)[None]

    u_s = zz_s[:, ZC_U:ZC_U + CONV_DIM].reshape(dbatch, dseq, CONV_DIM)
    conv_hist = state_conv.shape[2]
    new_conv_p = zz_p3[:, seq - conv_hist:, ZC_U:ZC_U + CONV_DIM][None]
    new_conv_s = jnp.concatenate([state_conv[0], u_s], axis=1)[:, -conv_hist:][None]
    new_k_s = k_ext[:, -WINDOW:].reshape(dbatch, WINDOW, N_KV_HEADS, HEAD_DIM)[None]
    new_v_s = v_ext[:, -WINDOW:].reshape(dbatch, WINDOW, N_KV_HEADS, HEAD_DIM)[None]
    return (y_p.reshape(batch, seq, d), y_s.reshape(dbatch, dseq, d),
            kv_tail(ZC_K), kv_tail(ZC_V), new_conv_p, new_k_s, new_v_s, new_conv_s)
```

```python
import functools
import math

import numpy as np
import jax
import jax.numpy as jnp
from jax import lax
from jax.experimental import pallas as pl
from jax.experimental.pallas import tpu as pltpu

F32 = jnp.float32
BF16 = jnp.bfloat16
I32 = jnp.int32

D_MODEL = 2048
CHUNK = 64
HEAD_DIM = 64
N_Q_HEADS = 16
N_KV_HEADS = 4
GROUP = N_Q_HEADS // N_KV_HEADS
ATTN_DIM = N_Q_HEADS * HEAD_DIM
KV_DIM = N_KV_HEADS * HEAD_DIM
WINDOW = 128
CONV_DIM = D_MODEL // 2
N_BUCKETS = 32
MAX_DISTANCE = 128
N_EXPERTS = 32
TOP_K = 4
D_FF = D_MODEL
SWIGLU_LIMIT = 7.0
SWIGLU_ALPHA = 1.702
NORM_EPS = 1e-5
IN_DIM = ATTN_DIM + 2 * KV_DIM + 3 * CONV_DIM + 2 * D_MODEL

V7X_VMEM_BYTES = 64 * 1024 * 1024
LANES = 128

ZC_GA = 0
ZC_GC = D_MODEL
ZC_CB = 2 * D_MODEL
ZC_U = ZC_CB + CONV_DIM
ZC_Q = ZC_U + CONV_DIM
ZC_K = ZC_Q + ATTN_DIM
ZC_V = ZC_K + KV_DIM
Z_DIM = ZC_V + KV_DIM
PROJ_TN = 512

ROW_BLK = 256
SUPER_BLKS = 5
SUPER_ROWS = SUPER_BLKS * ROW_BLK
FF_TILE = 256
N_FF_TILES = D_FF // FF_TILE
assert SUPER_BLKS <= N_FF_TILES


def _vmem_limit(nbytes):
    return int(min(nbytes, V7X_VMEM_BYTES - 6 * 1024 * 1024))


_MODE_PLAIN, _MODE_STASH0, _MODE_STASH1, _MODE_MUL0, _MODE_MUL1, _MODE_GATE = range(6)


def _proj_schedule():
    t = PROJ_TN
    cc0 = (ATTN_DIM + 2 * KV_DIM + CONV_DIM) // t
    ch0 = (ATTN_DIM + 2 * KV_DIM + 2 * CONV_DIM) // t
    cb0 = (ATTN_DIM + 2 * KV_DIM) // t
    ga0 = (ATTN_DIM + 2 * KV_DIM + 3 * CONV_DIM) // t
    steps = [
        (cc0, ZC_U // t, 0, _MODE_STASH0),
        (cc0 + 1, ZC_U // t, 0, _MODE_STASH1),
        (ch0, ZC_U // t, 0, _MODE_MUL0),
        (ch0 + 1, ZC_U // t + 1, 0, _MODE_MUL1),
        (0, ZC_Q // t, 0, _MODE_PLAIN),
        (1, ZC_Q // t + 1, 0, _MODE_PLAIN),
        (2, ZC_K // t, 0, _MODE_PLAIN),
        (cb0, ZC_CB // t, 0, _MODE_PLAIN),
        (cb0 + 1, ZC_CB // t + 1, 0, _MODE_PLAIN),
    ]
    for j in range(2 * D_MODEL // t):
        steps.append((ga0 + j, j, j, _MODE_GATE))
    return np.asarray(steps, dtype=np.int32)


def _in_proj_kernel(wt_ref, ot_ref, bt_ref, md_ref, x_hbm, g_ref, w_ref, b_ref, o_ref, x_buf, xn_ref, stash_ref, sem,
                    *, n_tiles):
    del wt_ref, ot_ref, bt_ref
    m = pl.program_id(0)
    n = pl.program_id(1)
    tm = x_buf.shape[0]

    def x_copy(tile):
        return pltpu.make_async_copy(x_hbm.at[pl.ds(pl.multiple_of(tile * tm, 8), tm)], x_buf, sem)

    def product():
        return jnp.dot(xn_ref[...], w_ref[...], preferred_element_type=F32)

    md = md_ref[n]

    @pl.when(md == _MODE_PLAIN)
    def _():
        o_ref[...] = product()

    @pl.when(md == _MODE_STASH0)
    def _():
        @pl.when(m == 0)
        def _():
            x_copy(0).start()

        x_copy(m).wait()
        x = x_buf[...]
        ms = jnp.mean(x * x, axis=-1, keepdims=True)
        xn_ref[...] = ((x * lax.rsqrt(ms + NORM_EPS)) * g_ref[...]).astype(BF16)
        stash_ref[0] = product()

    @pl.when(md == _MODE_STASH1)
    def _():
        if n_tiles > 1:
            @pl.when(m + 1 < n_tiles)
            def _():
                x_copy(m + 1).start()

        stash_ref[1] = product()

    @pl.when(md == _MODE_MUL0)
    def _():
        o_ref[...] = stash_ref[0] * product()

    @pl.when(md == _MODE_MUL1)
    def _():
        o_ref[...] = stash_ref[1] * product()

    @pl.when(md == _MODE_GATE)
    def _():
        o_ref[...] = jax.nn.sigmoid(product() + b_ref[...])


def _in_proj(x, g_mix, w_in, b_gate, tm):
    t, d = x.shape
    sched = _proj_schedule()
    n_steps = sched.shape[0]
    n_tiles = t // tm
    grid_spec = pltpu.PrefetchScalarGridSpec(
        num_scalar_prefetch=4,
        grid=(n_tiles, n_steps),
        in_specs=[
            pl.BlockSpec(memory_space=pl.ANY),
            pl.BlockSpec((1, d), lambda m, n, wt, ot, bt, md: (0, 0)),
            pl.BlockSpec((d, PROJ_TN), lambda m, n, wt, ot, bt, md: (0, wt[n])),
            pl.BlockSpec((1, PROJ_TN), lambda m, n, wt, ot, bt, md: (0, bt[n])),
        ],
        out_specs=pl.BlockSpec((tm, PROJ_TN), lambda m, n, wt, ot, bt, md: (m, ot[n])),
        scratch_shapes=[pltpu.VMEM((tm, d), F32), pltpu.VMEM((tm, d), BF16), pltpu.VMEM((2, tm, PROJ_TN), F32),
                        pltpu.SemaphoreType.DMA(())],
    )
    vmem = tm * d * 4 + tm * d * 2 + 2 * d * PROJ_TN * 2 + 6 * tm * PROJ_TN * 4 + (8 << 20)
    return pl.pallas_call(
        functools.partial(_in_proj_kernel, n_tiles=n_tiles),
        grid_spec=grid_spec,
        out_shape=jax.ShapeDtypeStruct((t, Z_DIM), F32),
        compiler_params=pltpu.CompilerParams(
            dimension_semantics=("arbitrary", "arbitrary"), vmem_limit_bytes=_vmem_limit(vmem)),
        name="in_proj",
    )(jnp.asarray(sched[:, 0]), jnp.asarray(sched[:, 1]), jnp.asarray(sched[:, 2]), jnp.asarray(sched[:, 3]),
      x, g_mix.reshape(1, d), w_in, b_gate.reshape(1, 2 * D_MODEL))


def _t5_buckets(n_q, n_k):
    half = N_BUCKETS // 2
    max_exact = half // 2
    rel = (np.arange(n_k, dtype=np.int32)[None, :] - WINDOW) - np.arange(n_q, dtype=np.int32)[:, None]
    ret = np.where(rel > 0, half, 0)
    n = np.abs(rel)
    nf = np.maximum(n, 1).astype(np.float32)
    large = max_exact + (np.log(nf / np.float32(max_exact)) / np.float32(math.log(MAX_DISTANCE / max_exact))
                         * np.float32(half - max_exact)).astype(np.int32)
    large = np.minimum(large, half - 1)
    return (ret + np.where(n < max_exact, n, large)).astype(np.int32)


def _build_bias(bias_ref, bucket_ref, table_ref, n_q):
    bucket = bucket_ref[...]
    for kvh in range(N_KV_HEADS):
        for g in range(GROUP):
            h = kvh * GROUP + g
            mat = jnp.zeros(bucket.shape, F32)
            for b in range(N_BUCKETS):
                mat = jnp.where(bucket == b, table_ref[b, h], mat)
            bias_ref[kvh, g * n_q:(g + 1) * n_q, :] = mat


def _attend(q, kwin, vwin, bias_ref, sinks_ref, n_q, invalid):
    outs = []
    grp = lax.broadcasted_iota(I32, (GROUP * n_q, 1), 0) // n_q
    for kvh in range(N_KV_HEADS):
        c0 = kvh * GROUP * HEAD_DIM
        qh = jnp.concatenate([q[:, c0 + g * HEAD_DIM:c0 + (g + 1) * HEAD_DIM] for g in range(GROUP)], axis=0)
        qh = (qh * (HEAD_DIM ** -0.5)).astype(BF16)
        kh = kwin[:, kvh * HEAD_DIM:(kvh + 1) * HEAD_DIM].astype(BF16)
        vh = vwin[:, kvh * HEAD_DIM:(kvh + 1) * HEAD_DIM].astype(BF16)
        s = lax.dot_general(qh, kh, (((1,), (1,)), ((), ())), preferred_element_type=F32) + bias_ref[kvh]
        if invalid is not None:
            s = jnp.where(invalid, -1e30, s)
        sink = jnp.zeros((GROUP * n_q, 1), F32)
        for g in range(GROUP):
            sink = jnp.where(grp == g, sinks_ref[kvh * GROUP + g], sink)
        m = jnp.maximum(jnp.max(s, axis=-1, keepdims=True), sink)
        p = jnp.exp(s - m)
        denom = jnp.sum(p, axis=-1, keepdims=True) + jnp.exp(sink - m)
        o = jnp.dot(p.astype(BF16), vh, preferred_element_type=F32) / denom
        outs.append(o)
    return outs


def _store_heads(o_ref, r0, n_q, outs):
    for kvh, o in enumerate(outs):
        for g in range(GROUP):
            c = (kvh * GROUP + g) * HEAD_DIM
            o_ref[r0:r0 + n_q, c:c + HEAD_DIM] = o[g * n_q:(g + 1) * n_q].astype(o_ref.dtype)


def _attn_prompt_kernel(table_ref, sinks_ref, bucket_ref, q_ref, kp_ref, kc_ref, vp_ref, vc_ref, o_ref, bias_ref):
    first = (pl.program_id(0) == 0) & (pl.program_id(1) == 0)

    @pl.when(first)
    def _():
        _build_bias(bias_ref, bucket_ref, table_ref, CHUNK)

    kwin = jnp.concatenate([kp_ref[...], kc_ref[...]], axis=0)
    vwin = jnp.concatenate([vp_ref[...], vc_ref[...]], axis=0)
    n_k = WINDOW + CHUNK
    seq_start = pl.program_id(1) == 0
    col = lax.broadcasted_iota(I32, (1, n_k), 1)
    for c in range(WINDOW // CHUNK):
        r0 = c * CHUNK
        invalid = seq_start & (col + r0 < WINDOW)
        outs = _attend(q_ref[r0:r0 + CHUNK, :], kwin[r0:r0 + n_k], vwin[r0:r0 + n_k], bias_ref, sinks_ref,
                       CHUNK, invalid)
        _store_heads(o_ref, r0, CHUNK, outs)


def _attn_prompt(zz, batch, seq, table, sinks):
    qb = WINDOW
    nb = seq // qb
    n_k = WINDOW + CHUNK
    bucket = jnp.asarray(_t5_buckets(CHUNK, n_k))
    smem = pl.BlockSpec(memory_space=pltpu.SMEM)
    kblk, vblk = ZC_K // KV_DIM, ZC_V // KV_DIM
    return pl.pallas_call(
        _attn_prompt_kernel,
        grid=(batch, nb),
        in_specs=[
            smem, smem,
            pl.BlockSpec((CHUNK, n_k), lambda b, j: (0, 0)),
            pl.BlockSpec((qb, ATTN_DIM), lambda b, j: (b * nb + j, ZC_Q // ATTN_DIM)),
            pl.BlockSpec((qb, KV_DIM), lambda b, j: (b * nb + jnp.maximum(j - 1, 0), kblk)),
            pl.BlockSpec((qb, KV_DIM), lambda b, j: (b * nb + j, kblk)),
            pl.BlockSpec((qb, KV_DIM), lambda b, j: (b * nb + jnp.maximum(j - 1, 0), vblk)),
            pl.BlockSpec((qb, KV_DIM), lambda b, j: (b * nb + j, vblk)),
        ],
        out_specs=pl.BlockSpec((qb, ATTN_DIM), lambda b, j: (b * nb + j, 0)),
        out_shape=jax.ShapeDtypeStruct((batch * seq, ATTN_DIM), BF16),
        scratch_shapes=[pltpu.VMEM((N_KV_HEADS, GROUP * CHUNK, n_k), F32)],
        compiler_params=pltpu.CompilerParams(dimension_semantics=("arbitrary", "arbitrary")),
        name="attn_prompt",
    )(table, sinks, bucket, zz, zz, zz, zz, zz)


def _attn_sample_kernel(table_ref, sinks_ref, bucket_ref, q_ref, k_ref, v_ref, o_ref, bias_ref, *, n_q):
    @pl.when(pl.program_id(0) == 0)
    def _():
        _build_bias(bias_ref, bucket_ref, table_ref, n_q)

    outs = _attend(q_ref[...], k_ref[0], v_ref[0], bias_ref, sinks_ref, n_q, None)
    _store_heads(o_ref, 0, n_q, outs)


def _attn_sample(zz, k_ext, v_ext, batch, n_q, table, sinks):
    n_k = WINDOW + n_q
    bucket = jnp.asarray(_t5_buckets(n_q, n_k))
    smem = pl.BlockSpec(memory_space=pltpu.SMEM)
    return pl.pallas_call(
        functools.partial(_attn_sample_kernel, n_q=n_q),
        grid=(batch,),
        in_specs=[
            smem, smem,
            pl.BlockSpec((n_q, n_k), lambda b: (0, 0)),
            pl.BlockSpec((n_q, ATTN_DIM), lambda b: (b, ZC_Q // ATTN_DIM)),
            pl.BlockSpec((1, n_k, KV_DIM), lambda b: (b, 0, 0)),
            pl.BlockSpec((1, n_k, KV_DIM), lambda b: (b, 0, 0)),
        ],
        out_specs=pl.BlockSpec((n_q, ATTN_DIM), lambda b: (b, 0)),
        out_shape=jax.ShapeDtypeStruct((batch * n_q, ATTN_DIM), BF16),
        scratch_shapes=[pltpu.VMEM((N_KV_HEADS, GROUP * n_q, n_k), F32)],
        compiler_params=pltpu.CompilerParams(dimension_semantics=("arbitrary",)),
        name="attn_sample",
    )(table, sinks, bucket, zz, k_ext, v_ext)


_MERGE_INPUTS = 15


def _merge_kernel(*refs, tm, seq_rows, tiles_per_seq):
    (o_ref, ga_ref, gc_ref, cb_ref, u_ref, h0_ref, h1_ref, x_ref, cw_ref, woa_ref, woc_ref, wout_ref,
     gf_ref, wr_ref, br_ref) = refs[:_MERGE_INPUTS]
    h_ref, hn_ref, idx_ref, wts_ref = refs[-4:]
    u = u_ref[...]
    row = lax.broadcasted_iota(I32, (tm, 1), 0)
    if seq_rows is None:
        at_start = (pl.program_id(0) % tiles_per_seq) == 0
        hist0 = jnp.where(at_start, 0.0, h0_ref[6:7, :])
        hist1 = jnp.where(at_start, 0.0, h1_ref[7:8, :])
        pos = row
    else:
        hist0 = h0_ref[...]
        hist1 = h1_ref[...]
        pos = row % seq_rows
    r1 = pltpu.roll(u, 1, 0)
    r2 = pltpu.roll(u, 2, 0)
    u1 = jnp.where(pos == 0, hist1, r1)
    u2 = jnp.where(pos == 0, hist0, jnp.where(pos == 1, hist1, r2))
    cw = cw_ref[...]
    y = cw[0:1, :] * u2
    y = y + cw[1:2, :] * u1
    y = y + cw[2:3, :] * u
    c = (cb_ref[...] * y).astype(BF16)
    t = ga_ref[...] * jnp.dot(o_ref[...], woa_ref[...], preferred_element_type=F32)
    t = t + gc_ref[...] * jnp.dot(c, woc_ref[...], preferred_element_type=F32)
    h = x_ref[...] + jnp.dot(t.astype(BF16), wout_ref[...], preferred_element_type=F32)
    h_ref[...] = h
    ms = jnp.mean(h * h, axis=-1, keepdims=True)
    hn = (h * lax.rsqrt(ms + NORM_EPS)) * gf_ref[...]
    hn_ref[...] = hn
    logits = lax.dot_general(wr_ref[...], hn.astype(BF16), (((1,), (1,)), ((), ())),
                             preferred_element_type=F32) + br_ref[...]
    erow = lax.broadcasted_iota(I32, logits.shape, 0).astype(F32)
    vals, idxs = [], []
    for _ in range(TOP_K):
        mx = jnp.max(logits, axis=0, keepdims=True)
        ix = jnp.min(jnp.where(logits == mx, erow, float(N_EXPERTS)), axis=0, keepdims=True)
        vals.append(mx)
        idxs.append(ix)
        logits = jnp.where(erow == ix, -jnp.inf, logits)
    es = [jnp.exp(v - vals[0]) for v in vals]
    tot = es[0]
    for e in es[1:]:
        tot = tot + e
    idx_ref[...] = jnp.concatenate(idxs, axis=0).astype(I32)
    wts_ref[...] = jnp.concatenate([e / tot for e in es], axis=0)


def _merge(o, zz, x, hist0, hist1, conv_w, woa, woc, wout, g_ffn, wr_t, br, tm, seq_rows, seq_len,
           hn_rows, hn_row0, hn_prev):
    t = x.shape[0]
    d = D_MODEL
    assert hn_row0 % tm == 0
    extra_specs, extra_args, aliases = [], [], {}
    if hn_prev is not None:
        extra_specs, extra_args, aliases = [pl.BlockSpec(memory_space=pl.ANY)], [hn_prev], {_MERGE_INPUTS: 1}
    full = lambda shape: pl.BlockSpec(shape, lambda i: (0,) * len(shape))
    if seq_rows is None:
        tiles_per_seq = seq_len // tm
        hist_spec = pl.BlockSpec((8, CONV_DIM), lambda i: (jnp.maximum(i * (tm // 8) - 1, 0), ZC_U // CONV_DIM))
        h0_spec = h1_spec = hist_spec
        h0_arg = h1_arg = zz
    else:
        tiles_per_seq = None
        h0_spec = h1_spec = pl.BlockSpec((tm, CONV_DIM), lambda i: (i, 0))
        h0_arg, h1_arg = hist0, hist1
    kern = functools.partial(_merge_kernel, tm=tm, seq_rows=seq_rows, tiles_per_seq=tiles_per_seq)
    vmem = (2 * tm * (ATTN_DIM * 2 + 5 * d * 4 + 4 * CONV_DIM * 4) + 2 * 2 * (2 * ATTN_DIM * d + d * d)
            + 10 * tm * d * 4 + (8 << 20))
    return pl.pallas_call(
        kern,
        grid=(t // tm,),
        in_specs=[
            pl.BlockSpec((tm, ATTN_DIM), lambda i: (i, 0)),
            pl.BlockSpec((tm, d), lambda i: (i, ZC_GA // d)),
            pl.BlockSpec((tm, d), lambda i: (i, ZC_GC // d)),
            pl.BlockSpec((tm, CONV_DIM), lambda i: (i, ZC_CB // CONV_DIM)),
            pl.BlockSpec((tm, CONV_DIM), lambda i: (i, ZC_U // CONV_DIM)),
            h0_spec, h1_spec,
            pl.BlockSpec((tm, d), lambda i: (i, 0)),
            full((3, CONV_DIM)),
            full((ATTN_DIM, d)), full((CONV_DIM, d)), full((d, d)),
            full((1, d)), full((N_EXPERTS, d)), full((N_EXPERTS, 1)),
        ] + extra_specs,
        out_specs=[
            pl.BlockSpec((tm, d), lambda i: (i, 0)),
            pl.BlockSpec((tm, d), lambda i: (hn_row0 // tm + i, 0)),
            pl.BlockSpec((TOP_K, tm), lambda i: (0, i)),
            pl.BlockSpec((TOP_K, tm), lambda i: (0, i)),
        ],
        out_shape=[
            jax.ShapeDtypeStruct((t, d), F32),
            jax.ShapeDtypeStruct((hn_rows, d), F32),
            jax.ShapeDtypeStruct((TOP_K, t), I32),
            jax.ShapeDtypeStruct((TOP_K, t), F32),
        ],
        input_output_aliases=aliases,
        compiler_params=pltpu.CompilerParams(dimension_semantics=("arbitrary",), vmem_limit_bytes=_vmem_limit(vmem)),
        name="merge_prompt" if seq_rows is None else "merge_sample",
    )(o, zz, zz, zz, zz, h0_arg, h1_arg, x, conv_w, woa, woc, wout, g_ffn.reshape(1, d), wr_t,
      br.reshape(N_EXPERTS, 1), *extra_args)


ROUTE_TILE = 384


def _route_kernel(idx_ref, dest_ref, cnt_ref, cnt_acc, carry, pst, tri_ref):
    p = pl.program_id(0)
    j = pl.program_id(1)
    tr = idx_ref.shape[1]
    idx = idx_ref[...]
    erow = lax.broadcasted_iota(I32, (N_EXPERTS, tr), 0)
    masks = [(erow == idx[k:k + 1, :]).astype(F32) for k in range(TOP_K)]
    mtot = masks[0]
    for m in masks[1:]:
        mtot = mtot + m
    tile_cnt = jnp.sum(mtot, axis=1, keepdims=True)

    @pl.when((p == 0) & (j == 0))
    def _():
        cnt_acc[...] = jnp.zeros_like(cnt_acc)

    @pl.when(p == 0)
    def _():
        cnt_acc[...] += jnp.broadcast_to(tile_cnt, cnt_acc.shape)

    @pl.when((p == 1) & (j == 0))
    def _():
        cnt = cnt_acc[:, 0:1]
        padded = jnp.floor((cnt + float(ROW_BLK - 1)) / float(ROW_BLK)) * float(ROW_BLK)
        r = lax.broadcasted_iota(I32, (N_EXPERTS, N_EXPERTS), 0)
        c = lax.broadcasted_iota(I32, (N_EXPERTS, N_EXPERTS), 1)
        rowv = jnp.sum(jnp.where(r == c, padded, 0.0), axis=0, keepdims=True)
        start = jnp.sum(jnp.where(c < r, rowv, 0.0), axis=1, keepdims=True)
        pst[...] = jnp.broadcast_to(start, pst.shape)
        carry[...] = jnp.zeros_like(carry)
        a = lax.broadcasted_iota(I32, (tr, tr), 0)
        b = lax.broadcasted_iota(I32, (tr, tr), 1)
        tri_ref[...] = (a < b).astype(BF16)

    @pl.when(p == 1)
    def _():
        excl = jnp.dot(mtot.astype(BF16), tri_ref[...], preferred_element_type=F32)
        val = pst[:, 0:1] + carry[:, 0:1] + excl
        dest = jnp.concatenate([jnp.sum(m * val, axis=0, keepdims=True) for m in masks], axis=0)
        dest_ref[...] = dest.astype(I32)
        carry[...] += jnp.broadcast_to(tile_cnt, carry.shape)
        cnt_ref[...] = cnt_acc[...].astype(I32)


def _route(idx_t):
    t = idx_t.shape[1]
    tr = ROUTE_TILE
    return pl.pallas_call(
        _route_kernel,
        grid=(2, t // tr),
        in_specs=[pl.BlockSpec((TOP_K, tr), lambda p, j: (0, j))],
        out_specs=[
            pl.BlockSpec((TOP_K, tr), lambda p, j: (0, j * p)),
            pl.BlockSpec((N_EXPERTS, LANES), lambda p, j: (0, 0)),
        ],
        out_shape=[jax.ShapeDtypeStruct((TOP_K, t), I32), jax.ShapeDtypeStruct((N_EXPERTS, LANES), I32)],
        scratch_shapes=[pltpu.VMEM((N_EXPERTS, LANES), F32), pltpu.VMEM((N_EXPERTS, LANES), F32),
                        pltpu.VMEM((N_EXPERTS, LANES), F32), pltpu.VMEM((tr, tr), BF16)],
        compiler_params=pltpu.CompilerParams(dimension_semantics=("arbitrary", "arbitrary")),
        name="route",
    )(idx_t)


DISPATCH_TILE = 384
DMA_UNROLL = 8


def _dispatch_kernel(dest_ref, hn_ref, xs_hbm, sem):
    tt = hn_ref.shape[0]

    def issue(t, c):
        for k in range(TOP_K):
            dst = xs_hbm.at[pl.ds(dest_ref[0, TOP_K * t + k], 1)]
            pltpu.make_async_copy(hn_ref.at[pl.ds(t, 1)], dst, sem).start(priority=k % 2)
        return c

    lax.fori_loop(0, tt, issue, 0, unroll=DMA_UNROLL)

    def drain(t, c):
        for k in range(TOP_K):
            pltpu.make_async_copy(hn_ref.at[pl.ds(0, 1)], xs_hbm.at[pl.ds(0, 1)], sem).wait()
        return c

    lax.fori_loop(0, tt, drain, 0, unroll=DMA_UNROLL)


def _dispatch(dest, hn, n_slots):
    t, d = hn.shape
    tt = DISPATCH_TILE
    return pl.pallas_call(
        _dispatch_kernel,
        grid=(t // tt,),
        in_specs=[pl.BlockSpec((1, TOP_K * tt), lambda i: (0, i), memory_space=pltpu.SMEM),
                  pl.BlockSpec((tt, d), lambda i: (i, 0))],
        out_specs=pl.BlockSpec(memory_space=pl.ANY),
        out_shape=jax.ShapeDtypeStruct((n_slots, d), hn.dtype),
        scratch_shapes=[pltpu.SemaphoreType.DMA(())],
        compiler_params=pltpu.CompilerParams(dimension_semantics=("arbitrary",)),
        name="dispatch",
    )(dest, hn)


MAX_UNIT_BLKS = 5


def _matmul_units(n_blocks):
    units, b = [], 0
    while b < n_blocks:
        size = min(MAX_UNIT_BLKS, n_blocks - b)
        units.append((b, size))
        b += size
    return units


def _expert_kernel(we_ref, ws_ref, wnb_ref, wvalid_ref, wact_ref,
                   xs_hbm, wg_ref, wu_ref, wd_ref, bg_ref, bu_ref, bd_ref, ys_hbm,
                   x_sb, acc, wgu_bf, wd_bf, stage, sem_in, sem_out):
    del we_ref, wact_ref
    w = pl.program_id(0)
    f = pl.program_id(1)
    n_items = wnb_ref.shape[0]
    last_f = N_FF_TILES - 1
    slot = w % 2
    nb = wnb_ref[w]
    row = lax.broadcasted_iota(I32, (ROW_BLK, 1), 0)

    def x_copy(item, j):
        src = xs_hbm.at[pl.ds(pl.multiple_of(ws_ref[item] + j * ROW_BLK, ROW_BLK), ROW_BLK)]
        return pltpu.make_async_copy(src, stage, sem_in)

    def x_cast(item, j, sl):
        r0 = pl.multiple_of(j * ROW_BLK, ROW_BLK)
        x_sb[sl, pl.ds(r0, ROW_BLK), :] = jnp.where(row + r0 < wvalid_ref[item], stage[...], 0.0).astype(BF16)

    def y_copy(item, j, sl):
        r0 = pl.multiple_of(j * ROW_BLK, ROW_BLK)
        dst = ys_hbm.at[pl.ds(pl.multiple_of(ws_ref[item] + r0, ROW_BLK), ROW_BLK)]
        return pltpu.make_async_copy(acc.at[sl, pl.ds(r0, ROW_BLK)], dst, sem_out.at[sl])

    def y_wait_all(item, sl):
        def body(j, c):
            y_copy(item, j, sl).wait()
            return c

        lax.fori_loop(0, wnb_ref[item], body, 0)

    @pl.when((w == 0) & (f == 0))
    def _first_rows():
        def body(j, c):
            cp = x_copy(0, j)
            cp.start()
            cp.wait()
            x_cast(0, j, 0)
            return c

        lax.fori_loop(0, wnb_ref[0], body, 0)

    nxt = jnp.minimum(w + 1, n_items - 1)
    prefetch = (w + 1 < n_items) & (f < wnb_ref[nxt])

    @pl.when(prefetch)
    def _():
        x_copy(nxt, f).start()

    @pl.when((f == 0) & (w >= 2))
    def _reclaim_acc():
        y_wait_all(jnp.maximum(w - 2, 0), slot)

    @pl.when(nb > 0)
    def _compute():
        bg = bg_ref[0]
        bu = bu_ref[0]
        bd = bd_ref[0]

        def cast_weights():
            wgu_bf[:, :FF_TILE] = wg_ref[0].astype(BF16)
            wgu_bf[:, FF_TILE:] = wu_ref[0].astype(BF16)
            wd_bf[...] = wd_ref[0].astype(BF16)

        def update(r0, rows, first):
            x = x_sb[slot, r0:r0 + rows, :]
            gu = jnp.dot(x, wgu_bf[...], preferred_element_type=F32)
            g = jnp.minimum(gu[:, :FF_TILE] + bg, SWIGLU_LIMIT)
            u = jnp.clip(gu[:, FF_TILE:] + bu, -SWIGLU_LIMIT, SWIGLU_LIMIT)
            hh = (u + 1.0) * (g * jax.nn.sigmoid(SWIGLU_ALPHA * g))
            part = jnp.dot(hh.astype(BF16), wd_bf[...], preferred_element_type=F32)
            if first:
                acc[slot, r0:r0 + rows, :] = part + bd
            else:
                acc[slot, r0:r0 + rows, :] += part

        def sweep(first):
            for n in range(1, SUPER_BLKS + 1):
                @pl.when(nb == n)
                def _(n=n):
                    cast_weights()
                    for b0, size in _matmul_units(n):
                        update(b0 * ROW_BLK, size * ROW_BLK, first)

        @pl.when(f == 0)
        def _():
            sweep(True)

        @pl.when(f > 0)
        def _():
            sweep(False)

    @pl.when((f == last_f) & (nb > 0))
    def _store():
        def body(j, c):
            y_copy(w, j, slot).start()
            return c

        lax.fori_loop(0, nb, body, 0)

    @pl.when(prefetch)
    def _():
        x_copy(nxt, f).wait()
        x_cast(nxt, f, 1 - slot)

    @pl.when((w == n_items - 1) & (f == last_f))
    def _drain():
        y_wait_all(jnp.maximum(w - 1, 0), 1 - slot)
        y_wait_all(w, slot)


def _experts(xs, tables, w_gate_up, b_gate_up, w_down, b_down):
    n_slots, d = xs.shape
    n_items = tables[0].shape[0]
    tf = FF_TILE
    nf = N_FF_TILES

    def ff(f, act, w):
        return f * act[w] + (nf - 1) * (1 - act[w])

    grid_spec = pltpu.PrefetchScalarGridSpec(
        num_scalar_prefetch=5,
        grid=(n_items, nf),
        in_specs=[
            pl.BlockSpec(memory_space=pl.ANY),
            pl.BlockSpec((1, d, tf), lambda w, f, we, ws, wnb, wv, act: (we[w], 0, ff(f, act, w))),
            pl.BlockSpec((1, d, tf), lambda w, f, we, ws, wnb, wv, act: (we[w], 0, nf + ff(f, act, w))),
            pl.BlockSpec((1, tf, d), lambda w, f, we, ws, wnb, wv, act: (we[w], ff(f, act, w), 0)),
            pl.BlockSpec((1, 1, tf), lambda w, f, we, ws, wnb, wv, act: (we[w], 0, ff(f, act, w))),
            pl.BlockSpec((1, 1, tf), lambda w, f, we, ws, wnb, wv, act: (we[w], 0, nf + ff(f, act, w))),
            pl.BlockSpec((1, 1, d), lambda w, f, we, ws, wnb, wv, act: (we[w], 0, 0)),
        ],
        out_specs=pl.BlockSpec(memory_space=pl.ANY),
        scratch_shapes=[
            pltpu.VMEM((2, SUPER_ROWS, d), BF16),
            pltpu.VMEM((2, SUPER_ROWS, d), F32),
            pltpu.VMEM((d, 2 * tf), BF16), pltpu.VMEM((tf, d), BF16),
            pltpu.VMEM((ROW_BLK, d), F32),
            pltpu.SemaphoreType.DMA(()), pltpu.SemaphoreType.DMA((2,)),
        ],
    )
    vmem = (2 * SUPER_ROWS * d * 6 + 2 * 3 * d * tf * 4 + 3 * d * tf * 2 + ROW_BLK * d * 4
            + 6 * ROW_BLK * d * 4 + (6 << 20))
    return pl.pallas_call(
        _expert_kernel,
        grid_spec=grid_spec,
        out_shape=jax.ShapeDtypeStruct((n_slots, d), F32),
        compiler_params=pltpu.CompilerParams(
            dimension_semantics=("arbitrary", "arbitrary"), vmem_limit_bytes=_vmem_limit(vmem)),
        name="experts",
    )(*tables, xs, w_gate_up, w_gate_up, w_down, b_gate_up.reshape(N_EXPERTS, 1, 2 * D_FF),
      b_gate_up.reshape(N_EXPERTS, 1, 2 * D_FF), b_down.reshape(N_EXPERTS, 1, d))


def _work_tables(counts, n_items):
    padded = (counts + ROW_BLK - 1) // ROW_BLK * ROW_BLK
    pstart = jnp.cumsum(padded) - padded
    items_e = (padded + SUPER_ROWS - 1) // SUPER_ROWS
    item_end = jnp.cumsum(items_e)
    item_start = item_end - items_e
    w = jnp.arange(n_items, dtype=I32)
    total = item_end[-1]
    active = (w < total).astype(I32)
    wl = jnp.minimum(w, total - 1)
    e = jnp.minimum(jnp.searchsorted(item_end, wl, side='right'), N_EXPERTS - 1).astype(I32)
    local = wl - item_start[e]
    start = pstart[e] + local * SUPER_ROWS
    rows = jnp.minimum(SUPER_ROWS, padded[e] - local * SUPER_ROWS)
    valid = jnp.clip(counts[e] - local * SUPER_ROWS, 0, rows)
    nb = (rows // ROW_BLK) * active
    return (e.astype(I32), start.astype(I32), nb.astype(I32), valid.astype(I32), active)


COMBINE_TILE = 256


def _combine_kernel(dest_ref, dest_next_ref, wts_ref, h_ref, g_ref, ys_hbm, out_ref, buf, sem, *, n):
    i = pl.program_id(0)
    slot = i % 2
    tt = h_ref.shape[0]

    def issue(dref, sl):
        def body(t, c):
            for k in range(TOP_K):
                src = ys_hbm.at[pl.ds(dref[0, TOP_K * t + k], 1)]
                pltpu.make_async_copy(src, buf.at[sl, k, pl.ds(t, 1)], sem.at[sl]).start(priority=k % 2)
            return c

        lax.fori_loop(0, tt, body, 0, unroll=DMA_UNROLL)

    def issue_unrolled(dref, sl):
        for t in range(tt):
            for k in range(TOP_K):
                src = ys_hbm.at[pl.ds(dref[0, TOP_K * t + k], 1)]
                pltpu.make_async_copy(src, buf.at[sl, k, pl.ds(t, 1)], sem.at[sl]).start(priority=k % 2)

    @pl.when(i == 0)
    def _():
        issue(dest_ref, 0)

    def drain(t, c):
        for k in range(TOP_K):
            pltpu.make_async_copy(ys_hbm.at[pl.ds(0, 1)], buf.at[slot, k, pl.ds(0, 1)], sem.at[slot]).wait()
        return c

    lax.fori_loop(0, tt, drain, 0, unroll=DMA_UNROLL)

    def finish():
        wts = wts_ref[...]
        moe = wts[:, 0:1] * buf[slot, 0]
        for k in range(1, TOP_K):
            moe = moe + wts[:, k:k + 1] * buf[slot, k]
        y = h_ref[...] + moe
        ms = jnp.mean(y * y, axis=-1, keepdims=True)
        out_ref[...] = (y * lax.rsqrt(ms + NORM_EPS)) * g_ref[...]

    if n > 1:
        @pl.when(i + 1 < n)
        def _():
            issue_unrolled(dest_next_ref, 1 - slot)
            finish()

        @pl.when(i + 1 >= n)
        def _():
            finish()
    else:
        finish()


def _combine(dest, wts, h, g_final, ys):
    t, d = h.shape
    tt = min(COMBINE_TILE, t)
    assert t % tt == 0
    n = t // tt
    return pl.pallas_call(
        functools.partial(_combine_kernel, n=n),
        grid=(n,),
        in_specs=[
            pl.BlockSpec((1, TOP_K * tt), lambda i: (0, i), memory_space=pltpu.SMEM),
            pl.BlockSpec((1, TOP_K * tt), lambda i: (0, jnp.minimum(i + 1, n - 1)), memory_space=pltpu.SMEM),
            pl.BlockSpec((tt, TOP_K), lambda i: (i, 0)),
            pl.BlockSpec((tt, d), lambda i: (i, 0)),
            pl.BlockSpec((1, d), lambda i: (0, 0)),
            pl.BlockSpec(memory_space=pl.ANY),
        ],
        out_specs=pl.BlockSpec((tt, d), lambda i: (i, 0)),
        out_shape=jax.ShapeDtypeStruct((t, d), F32),
        scratch_shapes=[pltpu.VMEM((2, TOP_K, tt, d), F32), pltpu.SemaphoreType.DMA((2,))],
        compiler_params=pltpu.CompilerParams(dimension_semantics=("arbitrary",)),
        name="combine",
    )(dest, dest, wts, h, g_final.reshape(1, d), ys)


def kernel(x_prompt, x_sample, cache_attn_k, cache_attn_v, state_conv, rel_bias_table, norm_mix_g, w_in, b_gate,
           attn_sinks, conv_w, w_o_attn, w_o_conv, w_out, norm_ffn_g, w_router, b_router, w_gate_up, b_gate_up,
           w_down, b_down, norm_final_g):
    batch, seq, d = x_prompt.shape
    dbatch, dseq, _ = x_sample.shape
    assert norm_mix_g.shape[0] == 1 and d == D_MODEL
    t_p, t_s = batch * seq, dbatch * dseq
    xp = x_prompt.reshape(t_p, d)
    xs = x_sample.reshape(t_s, d)

    woa = w_o_attn[0].astype(BF16)
    woc = w_o_conv[0].astype(BF16)
    wout = w_out[0].astype(BF16)
    wr_t = w_router[0].T.astype(BF16)

    w_in_bf = w_in[0].astype(BF16)
    zz_p = _in_proj(xp, norm_mix_g[0], w_in_bf, b_gate[0], tm=min(2048, t_p))
    zz_s = _in_proj(xs, norm_mix_g[0], w_in_bf, b_gate[0], tm=t_s)

    o_p = _attn_prompt(zz_p, batch, seq, rel_bias_table, attn_sinks[0])
    k_new = zz_s[:, ZC_K:ZC_K + KV_DIM].reshape(dbatch, dseq, KV_DIM)
    v_new = zz_s[:, ZC_V:ZC_V + KV_DIM].reshape(dbatch, dseq, KV_DIM)
    k_ext = jnp.concatenate([cache_attn_k[0].reshape(dbatch, WINDOW, KV_DIM), k_new], axis=1)
    v_ext = jnp.concatenate([cache_attn_v[0].reshape(dbatch, WINDOW, KV_DIM), v_new], axis=1)
    o_s = _attn_sample(zz_s, k_ext, v_ext, dbatch, dseq, rel_bias_table, attn_sinks[0])

    hist0 = jnp.repeat(state_conv[0][:, 0, :], dseq, axis=0)
    hist1 = jnp.repeat(state_conv[0][:, 1, :], dseq, axis=0)
    merge_args = (conv_w[0], woa, woc, wout, norm_ffn_g[0], wr_t, b_router[0])
    t_all = t_p + t_s
    h_p, hn, idx_p, wts_p = _merge(o_p, zz_p, xp, None, None, *merge_args, tm=256, seq_rows=None, seq_len=seq,
                                   hn_rows=t_all, hn_row0=0, hn_prev=None)
    h_s, hn, idx_s, wts_s = _merge(o_s, zz_s, xs, hist0, hist1, *merge_args, tm=t_s, seq_rows=dseq, seq_len=dseq,
                                   hn_rows=t_all, hn_row0=t_p, hn_prev=hn)

    idx_t = jnp.concatenate([idx_p, idx_s], axis=1)
    wts = jnp.concatenate([wts_p, wts_s], axis=1).T
    n_assign = t_all * TOP_K
    n_slots = (n_assign + N_EXPERTS * (ROW_BLK - 1) + ROW_BLK - 1) // ROW_BLK * ROW_BLK
    n_items = N_EXPERTS + (n_assign - N_EXPERTS) // SUPER_ROWS

    dest, cnt = _route(idx_t)
    tables = _work_tables(cnt[:, 0], n_items)
    dest = dest.T.reshape(1, -1)
    xs_sorted = _dispatch(dest, hn, n_slots)
    ys = _experts(xs_sorted, tables, w_gate_up[0], b_gate_up[0], w_down[0], b_down[0])
    y_p = _combine(dest[:, :TOP_K * t_p], wts[:t_p], h_p, norm_final_g, ys)
    y_s = _combine(dest[:, TOP_K * t_p:], wts[t_p:], h_s, norm_final_g, ys)

    zz_p3 = zz_p.reshape(batch, seq, Z_DIM)

    def kv_tail(c0):
        return zz_p3[:, seq - WINDOW:, c0:c0 + KV_DIM].reshape(batch, WINDOW, N_KV_HEADS, HEAD_DIM)[None]

    u_s = zz_s[:, ZC_U:ZC_U + CONV_DIM].reshape(dbatch, dseq, CONV_DIM)
    conv_hist = state_conv.shape[2]
    new_conv_p = zz_p3[:, seq - conv_hist:, ZC_U:ZC_U + CONV_DIM][None]
    new_conv_s = jnp.concatenate([state_conv[0], u_s], axis=1)[:, -conv_hist:][None]
    new_k_s = k_ext[:, -WINDOW:].reshape(dbatch, WINDOW, N_KV_HEADS, HEAD_DIM)[None]
    new_v_s = v_ext[:, -WINDOW:].reshape(dbatch, WINDOW, N_KV_HEADS, HEAD_DIM)[None]
    return (y_p.reshape(batch, seq, d), y_s.reshape(dbatch, dseq, d),
            kv_tail(ZC_K), kv_tail(ZC_V), new_conv_p, new_k_s, new_v_s, new_conv_s)
```

```python
import functools
import math

import numpy as np
import jax
import jax.numpy as jnp
from jax import lax
from jax.experimental import pallas as pl
from jax.experimental.pallas import tpu as pltpu

F32 = jnp.float32
BF16 = jnp.bfloat16
I32 = jnp.int32

D_MODEL = 2048
CHUNK = 64
HEAD_DIM = 64
N_Q_HEADS = 16
N_KV_HEADS = 4
GROUP = N_Q_HEADS // N_KV_HEADS
ATTN_DIM = N_Q_HEADS * HEAD_DIM
KV_DIM = N_KV_HEADS * HEAD_DIM
WINDOW = 128
CONV_DIM = D_MODEL // 2
N_BUCKETS = 32
MAX_DISTANCE = 128
N_EXPERTS = 32
TOP_K = 4
D_FF = D_MODEL
SWIGLU_LIMIT = 7.0
SWIGLU_ALPHA = 1.702
NORM_EPS = 1e-5
IN_DIM = ATTN_DIM + 2 * KV_DIM + 3 * CONV_DIM + 2 * D_MODEL

V7X_VMEM_BYTES = 64 * 1024 * 1024
LANES = 128

ZC_GA = 0
ZC_GC = D_MODEL
ZC_CB = 2 * D_MODEL
ZC_U = ZC_CB + CONV_DIM
ZC_Q = ZC_U + CONV_DIM
ZC_K = ZC_Q + ATTN_DIM
ZC_V = ZC_K + KV_DIM
Z_DIM = ZC_V + KV_DIM
PROJ_TN = 512

ATTN_Q_ROWS = 128

ROW_BLK = 256
SUPER_BLKS = 5
SUPER_ROWS = SUPER_BLKS * ROW_BLK
FF_TILE = 256
N_FF_TILES = D_FF // FF_TILE
assert SUPER_BLKS <= N_FF_TILES


def _vmem_limit(nbytes):
    return int(min(nbytes, V7X_VMEM_BYTES - 6 * 1024 * 1024))


_MODE_PLAIN, _MODE_STASH0, _MODE_STASH1, _MODE_MUL0, _MODE_MUL1, _MODE_GATE = range(6)


def _proj_schedule():
    t = PROJ_TN
    cc0 = (ATTN_DIM + 2 * KV_DIM + CONV_DIM) // t
    ch0 = (ATTN_DIM + 2 * KV_DIM + 2 * CONV_DIM) // t
    cb0 = (ATTN_DIM + 2 * KV_DIM) // t
    ga0 = (ATTN_DIM + 2 * KV_DIM + 3 * CONV_DIM) // t
    steps = [
        (cc0, ZC_U // t, 0, _MODE_STASH0),
        (cc0 + 1, ZC_U // t, 0, _MODE_STASH1),
        (ch0, ZC_U // t, 0, _MODE_MUL0),
        (ch0 + 1, ZC_U // t + 1, 0, _MODE_MUL1),
        (0, ZC_Q // t, 0, _MODE_PLAIN),
        (1, ZC_Q // t + 1, 0, _MODE_PLAIN),
        (2, ZC_K // t, 0, _MODE_PLAIN),
        (cb0, ZC_CB // t, 0, _MODE_PLAIN),
        (cb0 + 1, ZC_CB // t + 1, 0, _MODE_PLAIN),
    ]
    for j in range(2 * D_MODEL // t):
        steps.append((ga0 + j, j, j, _MODE_GATE))
    return np.asarray(steps, dtype=np.int32)


def _in_proj_kernel(wt_ref, ot_ref, bt_ref, md_ref, x_hbm, g_ref, w_ref, b_ref, o_ref, x_buf, xn_ref, stash_ref, sem,
                    *, n_tiles):
    del wt_ref, ot_ref, bt_ref
    m = pl.program_id(0)
    n = pl.program_id(1)
    tm = x_buf.shape[0]

    def x_copy(tile):
        return pltpu.make_async_copy(x_hbm.at[pl.ds(pl.multiple_of(tile * tm, 8), tm)], x_buf, sem)

    def product():
        return jnp.dot(xn_ref[...], w_ref[...], preferred_element_type=F32)

    md = md_ref[n]

    @pl.when(md == _MODE_PLAIN)
    def _():
        o_ref[...] = product()

    @pl.when(md == _MODE_STASH0)
    def _():
        @pl.when(m == 0)
        def _():
            x_copy(0).start()

        x_copy(m).wait()
        x = x_buf[...]
        ms = jnp.mean(x * x, axis=-1, keepdims=True)
        xn_ref[...] = ((x * lax.rsqrt(ms + NORM_EPS)) * g_ref[...]).astype(BF16)
        stash_ref[0] = product()

    @pl.when(md == _MODE_STASH1)
    def _():
        if n_tiles > 1:
            @pl.when(m + 1 < n_tiles)
            def _():
                x_copy(m + 1).start()

        stash_ref[1] = product()

    @pl.when(md == _MODE_MUL0)
    def _():
        o_ref[...] = stash_ref[0] * product()

    @pl.when(md == _MODE_MUL1)
    def _():
        o_ref[...] = stash_ref[1] * product()

    @pl.when(md == _MODE_GATE)
    def _():
        o_ref[...] = jax.nn.sigmoid(product() + b_ref[...])


def _in_proj(x, g_mix, w_in, b_gate, tm):
    t, d = x.shape
    sched = _proj_schedule()
    n_steps = sched.shape[0]
    n_tiles = t // tm
    grid_spec = pltpu.PrefetchScalarGridSpec(
        num_scalar_prefetch=4,
        grid=(n_tiles, n_steps),
        in_specs=[
            pl.BlockSpec(memory_space=pl.ANY),
            pl.BlockSpec((1, d), lambda m, n, wt, ot, bt, md: (0, 0)),
            pl.BlockSpec((d, PROJ_TN), lambda m, n, wt, ot, bt, md: (0, wt[n])),
            pl.BlockSpec((1, PROJ_TN), lambda m, n, wt, ot, bt, md: (0, bt[n])),
        ],
        out_specs=pl.BlockSpec((tm, PROJ_TN), lambda m, n, wt, ot, bt, md: (m, ot[n])),
        scratch_shapes=[pltpu.VMEM((tm, d), F32), pltpu.VMEM((tm, d), BF16), pltpu.VMEM((2, tm, PROJ_TN), F32),
                        pltpu.SemaphoreType.DMA(())],
    )
    vmem = tm * d * 4 + tm * d * 2 + 2 * d * PROJ_TN * 2 + 6 * tm * PROJ_TN * 4 + (8 << 20)
    return pl.pallas_call(
        functools.partial(_in_proj_kernel, n_tiles=n_tiles),
        grid_spec=grid_spec,
        out_shape=jax.ShapeDtypeStruct((t, Z_DIM), F32),
        compiler_params=pltpu.CompilerParams(
            dimension_semantics=("arbitrary", "arbitrary"), vmem_limit_bytes=_vmem_limit(vmem)),
        name="in_proj",
    )(jnp.asarray(sched[:, 0]), jnp.asarray(sched[:, 1]), jnp.asarray(sched[:, 2]), jnp.asarray(sched[:, 3]),
      x, g_mix.reshape(1, d), w_in, b_gate.reshape(1, 2 * D_MODEL))


def _t5_buckets(n_q, n_k):
    half = N_BUCKETS // 2
    max_exact = half // 2
    rel = (np.arange(n_k, dtype=np.int32)[None, :] - WINDOW) - np.arange(n_q, dtype=np.int32)[:, None]
    ret = np.where(rel > 0, half, 0)
    n = np.abs(rel)
    nf = np.maximum(n, 1).astype(np.float32)
    large = max_exact + (np.log(nf / np.float32(max_exact)) / np.float32(math.log(MAX_DISTANCE / max_exact))
                         * np.float32(half - max_exact)).astype(np.int32)
    large = np.minimum(large, half - 1)
    return (ret + np.where(n < max_exact, n, large)).astype(np.int32)


def _build_bias(bias_ref, bucket_ref, table_ref, n_q):
    bucket = bucket_ref[...]
    for kvh in range(N_KV_HEADS):
        for g in range(GROUP):
            h = kvh * GROUP + g
            mat = jnp.zeros(bucket.shape, F32)
            for b in range(N_BUCKETS):
                mat = jnp.where(bucket == b, table_ref[b, h], mat)
            bias_ref[kvh, g * n_q:(g + 1) * n_q, :] = mat


def _attend(q, kwin, vwin, bias_ref, sinks_ref, n_q, invalid):
    outs = []
    grp = lax.broadcasted_iota(I32, (GROUP * n_q, 1), 0) // n_q
    for kvh in range(N_KV_HEADS):
        c0 = kvh * GROUP * HEAD_DIM
        qh = jnp.concatenate([q[:, c0 + g * HEAD_DIM:c0 + (g + 1) * HEAD_DIM] for g in range(GROUP)], axis=0)
        qh = (qh * (HEAD_DIM ** -0.5)).astype(BF16)
        kh = kwin[:, kvh * HEAD_DIM:(kvh + 1) * HEAD_DIM].astype(BF16)
        vh = vwin[:, kvh * HEAD_DIM:(kvh + 1) * HEAD_DIM].astype(BF16)
        s = lax.dot_general(qh, kh, (((1,), (1,)), ((), ())), preferred_element_type=F32) + bias_ref[kvh]
        if invalid is not None:
            s = jnp.where(invalid, -1e30, s)
        sink = jnp.zeros((GROUP * n_q, 1), F32)
        for g in range(GROUP):
            sink = jnp.where(grp == g, sinks_ref[kvh * GROUP + g], sink)
        m = jnp.maximum(jnp.max(s, axis=-1, keepdims=True), sink)
        p = jnp.exp(s - m)
        denom = jnp.sum(p, axis=-1, keepdims=True) + jnp.exp(sink - m)
        o = jnp.dot(p.astype(BF16), vh, preferred_element_type=F32) / denom
        outs.append(o)
    return outs


def _store_heads(o_ref, r0, n_q, outs):
    for kvh, o in enumerate(outs):
        for g in range(GROUP):
            c = (kvh * GROUP + g) * HEAD_DIM
            o_ref[r0:r0 + n_q, c:c + HEAD_DIM] = o[g * n_q:(g + 1) * n_q].astype(o_ref.dtype)


def _attn_prompt_kernel(table_ref, sinks_ref, bucket_ref, q_ref, kp_ref, kc_ref, vp_ref, vc_ref, o_ref, bias_ref):
    first = (pl.program_id(0) == 0) & (pl.program_id(1) == 0)

    @pl.when(first)
    def _():
        _build_bias(bias_ref, bucket_ref, table_ref, CHUNK)

    kwin = jnp.concatenate([kp_ref[...], kc_ref[...]], axis=0)
    vwin = jnp.concatenate([vp_ref[...], vc_ref[...]], axis=0)
    n_k = WINDOW + CHUNK
    seq_start = pl.program_id(1) == 0
    col = lax.broadcasted_iota(I32, (1, n_k), 1)
    for c in range(q_ref.shape[0] // CHUNK):
        r0 = c * CHUNK
        invalid = seq_start & (col + r0 < WINDOW)
        outs = _attend(q_ref[r0:r0 + CHUNK, :], kwin[r0:r0 + n_k], vwin[r0:r0 + n_k], bias_ref, sinks_ref,
                       CHUNK, invalid)
        _store_heads(o_ref, r0, CHUNK, outs)


def _attn_prompt(zz, batch, seq, table, sinks):
    qb = ATTN_Q_ROWS
    assert seq % qb == 0 and qb % WINDOW == 0
    nb = seq // qb
    wpq = qb // WINDOW
    n_k = WINDOW + CHUNK
    bucket = jnp.asarray(_t5_buckets(CHUNK, n_k))
    smem = pl.BlockSpec(memory_space=pltpu.SMEM)
    kblk, vblk = ZC_K // KV_DIM, ZC_V // KV_DIM
    return pl.pallas_call(
        _attn_prompt_kernel,
        grid=(batch, nb),
        in_specs=[
            smem, smem,
            pl.BlockSpec((CHUNK, n_k), lambda b, j: (0, 0)),
            pl.BlockSpec((qb, ATTN_DIM), lambda b, j: (b * nb + j, ZC_Q // ATTN_DIM)),
            pl.BlockSpec((WINDOW, KV_DIM), lambda b, j: ((b * nb + j) * wpq - jnp.minimum(j, 1), kblk)),
            pl.BlockSpec((qb, KV_DIM), lambda b, j: (b * nb + j, kblk)),
            pl.BlockSpec((WINDOW, KV_DIM), lambda b, j: ((b * nb + j) * wpq - jnp.minimum(j, 1), vblk)),
            pl.BlockSpec((qb, KV_DIM), lambda b, j: (b * nb + j, vblk)),
        ],
        out_specs=pl.BlockSpec((qb, ATTN_DIM), lambda b, j: (b * nb + j, 0)),
        out_shape=jax.ShapeDtypeStruct((batch * seq, ATTN_DIM), BF16),
        scratch_shapes=[pltpu.VMEM((N_KV_HEADS, GROUP * CHUNK, n_k), F32)],
        compiler_params=pltpu.CompilerParams(dimension_semantics=("arbitrary", "arbitrary")),
        name="attn_prompt",
    )(table, sinks, bucket, zz, zz, zz, zz, zz)


def _attn_sample_kernel(table_ref, sinks_ref, bucket_ref, q_ref, k_ref, v_ref, o_ref, bias_ref, *, n_q):
    @pl.when(pl.program_id(0) == 0)
    def _():
        _build_bias(bias_ref, bucket_ref, table_ref, n_q)

    outs = _attend(q_ref[...], k_ref[0], v_ref[0], bias_ref, sinks_ref, n_q, None)
    _store_heads(o_ref, 0, n_q, outs)


def _attn_sample(zz, k_ext, v_ext, batch, n_q, table, sinks):
    n_k = WINDOW + n_q
    bucket = jnp.asarray(_t5_buckets(n_q, n_k))
    smem = pl.BlockSpec(memory_space=pltpu.SMEM)
    return pl.pallas_call(
        functools.partial(_attn_sample_kernel, n_q=n_q),
        grid=(batch,),
        in_specs=[
            smem, smem,
            pl.BlockSpec((n_q, n_k), lambda b: (0, 0)),
            pl.BlockSpec((n_q, ATTN_DIM), lambda b: (b, ZC_Q // ATTN_DIM)),
            pl.BlockSpec((1, n_k, KV_DIM), lambda b: (b, 0, 0)),
            pl.BlockSpec((1, n_k, KV_DIM), lambda b: (b, 0, 0)),
        ],
        out_specs=pl.BlockSpec((n_q, ATTN_DIM), lambda b: (b, 0)),
        out_shape=jax.ShapeDtypeStruct((batch * n_q, ATTN_DIM), BF16),
        scratch_shapes=[pltpu.VMEM((N_KV_HEADS, GROUP * n_q, n_k), F32)],
        compiler_params=pltpu.CompilerParams(dimension_semantics=("arbitrary",)),
        name="attn_sample",
    )(table, sinks, bucket, zz, k_ext, v_ext)


_MERGE_INPUTS = 15


def _merge_kernel(*refs, tm, seq_rows, tiles_per_seq):
    (o_ref, ga_ref, gc_ref, cb_ref, u_ref, h0_ref, h1_ref, x_ref, cw_ref, woa_ref, woc_ref, wout_ref,
     gf_ref, wr_ref, br_ref) = refs[:_MERGE_INPUTS]
    h_ref, hn_ref, idx_ref, wts_ref = refs[-4:]
    u = u_ref[...]
    row = lax.broadcasted_iota(I32, (tm, 1), 0)
    if seq_rows is None:
        at_start = (pl.program_id(0) % tiles_per_seq) == 0
        hist0 = jnp.where(at_start, 0.0, h0_ref[6:7, :])
        hist1 = jnp.where(at_start, 0.0, h1_ref[7:8, :])
        pos = row
    else:
        hist0 = h0_ref[...]
        hist1 = h1_ref[...]
        pos = row % seq_rows
    r1 = pltpu.roll(u, 1, 0)
    r2 = pltpu.roll(u, 2, 0)
    u1 = jnp.where(pos == 0, hist1, r1)
    u2 = jnp.where(pos == 0, hist0, jnp.where(pos == 1, hist1, r2))
    cw = cw_ref[...]
    y = cw[0:1, :] * u2
    y = y + cw[1:2, :] * u1
    y = y + cw[2:3, :] * u
    c = (cb_ref[...] * y).astype(BF16)
    t = ga_ref[...] * jnp.dot(o_ref[...], woa_ref[...], preferred_element_type=F32)
    t = t + gc_ref[...] * jnp.dot(c, woc_ref[...], preferred_element_type=F32)
    h = x_ref[...] + jnp.dot(t.astype(BF16), wout_ref[...], preferred_element_type=F32)
    h_ref[...] = h
    ms = jnp.mean(h * h, axis=-1, keepdims=True)
    hn = (h * lax.rsqrt(ms + NORM_EPS)) * gf_ref[...]
    hn_ref[...] = hn
    logits = lax.dot_general(wr_ref[...], hn.astype(BF16), (((1,), (1,)), ((), ())),
                             preferred_element_type=F32) + br_ref[...]
    erow = lax.broadcasted_iota(I32, logits.shape, 0).astype(F32)
    vals, idxs = [], []
    for _ in range(TOP_K):
        mx = jnp.max(logits, axis=0, keepdims=True)
        ix = jnp.min(jnp.where(logits == mx, erow, float(N_EXPERTS)), axis=0, keepdims=True)
        vals.append(mx)
        idxs.append(ix)
        logits = jnp.where(erow == ix, -jnp.inf, logits)
    es = [jnp.exp(v - vals[0]) for v in vals]
    tot = es[0]
    for e in es[1:]:
        tot = tot + e
    idx_ref[...] = jnp.concatenate(idxs, axis=0).astype(I32)
    wts_ref[...] = jnp.concatenate([e / tot for e in es], axis=0)


def _merge(o, zz, x, hist0, hist1, conv_w, woa, woc, wout, g_ffn, wr_t, br, tm, seq_rows, seq_len,
           hn_rows, hn_row0, hn_prev):
    t = x.shape[0]
    d = D_MODEL
    assert hn_row0 % tm == 0
    extra_specs, extra_args, aliases = [], [], {}
    if hn_prev is not None:
        extra_specs, extra_args, aliases = [pl.BlockSpec(memory_space=pl.ANY)], [hn_prev], {_MERGE_INPUTS: 1}
    full = lambda shape: pl.BlockSpec(shape, lambda i: (0,) * len(shape))
    if seq_rows is None:
        tiles_per_seq = seq_len // tm
        hist_spec = pl.BlockSpec((8, CONV_DIM), lambda i: (jnp.maximum(i * (tm // 8) - 1, 0), ZC_U // CONV_DIM))
        h0_spec = h1_spec = hist_spec
        h0_arg = h1_arg = zz
    else:
        tiles_per_seq = None
        h0_spec = h1_spec = pl.BlockSpec((tm, CONV_DIM), lambda i: (i, 0))
        h0_arg, h1_arg = hist0, hist1
    kern = functools.partial(_merge_kernel, tm=tm, seq_rows=seq_rows, tiles_per_seq=tiles_per_seq)
    vmem = (2 * tm * (ATTN_DIM * 2 + 5 * d * 4 + 4 * CONV_DIM * 4) + 2 * 2 * (2 * ATTN_DIM * d + d * d)
            + 10 * tm * d * 4 + (8 << 20))
    return pl.pallas_call(
        kern,
        grid=(t // tm,),
        in_specs=[
            pl.BlockSpec((tm, ATTN_DIM), lambda i: (i, 0)),
            pl.BlockSpec((tm, d), lambda i: (i, ZC_GA // d)),
            pl.BlockSpec((tm, d), lambda i: (i, ZC_GC // d)),
            pl.BlockSpec((tm, CONV_DIM), lambda i: (i, ZC_CB // CONV_DIM)),
            pl.BlockSpec((tm, CONV_DIM), lambda i: (i, ZC_U // CONV_DIM)),
            h0_spec, h1_spec,
            pl.BlockSpec((tm, d), lambda i: (i, 0)),
            full((3, CONV_DIM)),
            full((ATTN_DIM, d)), full((CONV_DIM, d)), full((d, d)),
            full((1, d)), full((N_EXPERTS, d)), full((N_EXPERTS, 1)),
        ] + extra_specs,
        out_specs=[
            pl.BlockSpec((tm, d), lambda i: (i, 0)),
            pl.BlockSpec((tm, d), lambda i: (hn_row0 // tm + i, 0)),
            pl.BlockSpec((TOP_K, tm), lambda i: (0, i)),
            pl.BlockSpec((TOP_K, tm), lambda i: (0, i)),
        ],
        out_shape=[
            jax.ShapeDtypeStruct((t, d), F32),
            jax.ShapeDtypeStruct((hn_rows, d), F32),
            jax.ShapeDtypeStruct((TOP_K, t), I32),
            jax.ShapeDtypeStruct((TOP_K, t), F32),
        ],
        input_output_aliases=aliases,
        compiler_params=pltpu.CompilerParams(dimension_semantics=("arbitrary",), vmem_limit_bytes=_vmem_limit(vmem)),
        name="merge_prompt" if seq_rows is None else "merge_sample",
    )(o, zz, zz, zz, zz, h0_arg, h1_arg, x, conv_w, woa, woc, wout, g_ffn.reshape(1, d), wr_t,
      br.reshape(N_EXPERTS, 1), *extra_args)


ROUTE_TILE = 32768
ROUTE_SUB = 384


def _route_kernel(idx_ref, dest_ref, cnt_ref, cnt_acc, carry, pst, tri_ref):
    p = pl.program_id(0)
    j = pl.program_id(1)
    sub = tri_ref.shape[0]
    n_sub = idx_ref.shape[1] // sub
    erow = lax.broadcasted_iota(I32, (N_EXPERTS, sub), 0)

    def membership(s):
        idx = idx_ref[:, pl.ds(pl.multiple_of(s * sub, LANES), sub)]
        masks = [(erow == idx[k:k + 1, :]).astype(F32) for k in range(TOP_K)]
        mtot = masks[0]
        for m in masks[1:]:
            mtot = mtot + m
        return masks, mtot, jnp.sum(mtot, axis=1, keepdims=True)

    @pl.when((p == 0) & (j == 0))
    def _():
        cnt_acc[...] = jnp.zeros_like(cnt_acc)

    @pl.when(p == 0)
    def _():
        def body(s, c):
            _, _, piece_cnt = membership(s)
            cnt_acc[...] += jnp.broadcast_to(piece_cnt, cnt_acc.shape)
            return c

        lax.fori_loop(0, n_sub, body, 0)

    @pl.when((p == 1) & (j == 0))
    def _():
        cnt = cnt_acc[:, 0:1]
        padded = jnp.floor((cnt + float(ROW_BLK - 1)) / float(ROW_BLK)) * float(ROW_BLK)
        r = lax.broadcasted_iota(I32, (N_EXPERTS, N_EXPERTS), 0)
        c = lax.broadcasted_iota(I32, (N_EXPERTS, N_EXPERTS), 1)
        rowv = jnp.sum(jnp.where(r == c, padded, 0.0), axis=0, keepdims=True)
        start = jnp.sum(jnp.where(c < r, rowv, 0.0), axis=1, keepdims=True)
        pst[...] = jnp.broadcast_to(start, pst.shape)
        carry[...] = jnp.zeros_like(carry)
        a = lax.broadcasted_iota(I32, (sub, sub), 0)
        b = lax.broadcasted_iota(I32, (sub, sub), 1)
        tri_ref[...] = (a < b).astype(BF16)

    @pl.when(p == 1)
    def _():
        def body(s, c):
            masks, mtot, piece_cnt = membership(s)
            excl = jnp.dot(mtot.astype(BF16), tri_ref[...], preferred_element_type=F32)
            val = pst[:, 0:1] + carry[:, 0:1] + excl
            dest = jnp.concatenate([jnp.sum(m * val, axis=0, keepdims=True) for m in masks], axis=0)
            dest_ref[:, pl.ds(pl.multiple_of(s * sub, LANES), sub)] = dest.astype(I32)
            carry[...] += jnp.broadcast_to(piece_cnt, carry.shape)
            return c

        lax.fori_loop(0, n_sub, body, 0)
        cnt_ref[...] = cnt_acc[...].astype(I32)


def _route(idx_t):
    t = idx_t.shape[1]
    assert t % ROUTE_SUB == 0
    tr = max(c for c in range(ROUTE_SUB, min(ROUTE_TILE, t) + 1, ROUTE_SUB) if t % c == 0)
    return pl.pallas_call(
        _route_kernel,
        grid=(2, t // tr),
        in_specs=[pl.BlockSpec((TOP_K, tr), lambda p, j: (0, j))],
        out_specs=[
            pl.BlockSpec((TOP_K, tr), lambda p, j: (0, j * p)),
            pl.BlockSpec((N_EXPERTS, LANES), lambda p, j: (0, 0)),
        ],
        out_shape=[jax.ShapeDtypeStruct((TOP_K, t), I32), jax.ShapeDtypeStruct((N_EXPERTS, LANES), I32)],
        scratch_shapes=[pltpu.VMEM((N_EXPERTS, LANES), F32), pltpu.VMEM((N_EXPERTS, LANES), F32),
                        pltpu.VMEM((N_EXPERTS, LANES), F32), pltpu.VMEM((ROUTE_SUB, ROUTE_SUB), BF16)],
        compiler_params=pltpu.CompilerParams(dimension_semantics=("arbitrary", "arbitrary")),
        name="route",
    )(idx_t)


DISPATCH_TILE = 1376
DMA_UNROLL = 8


def _dispatch_kernel(dest_ref, hn_ref, xs_hbm, sem):
    tt = hn_ref.shape[0]

    def issue(t, c):
        for k in range(TOP_K):
            dst = xs_hbm.at[pl.ds(dest_ref[0, TOP_K * t + k], 1)]
            pltpu.make_async_copy(hn_ref.at[pl.ds(t, 1)], dst, sem).start(priority=k % 2)
        return c

    lax.fori_loop(0, tt, issue, 0, unroll=DMA_UNROLL)

    def drain(t, c):
        for k in range(TOP_K):
            pltpu.make_async_copy(hn_ref.at[pl.ds(0, 1)], xs_hbm.at[pl.ds(0, 1)], sem).wait()
        return c

    lax.fori_loop(0, tt, drain, 0, unroll=DMA_UNROLL)


def _dispatch(dest, hn, n_slots):
    t, d = hn.shape
    tt = max(c for c in range(32, DISPATCH_TILE + 1, 32) if t % c == 0)
    return pl.pallas_call(
        _dispatch_kernel,
        grid=(t // tt,),
        in_specs=[pl.BlockSpec((1, TOP_K * tt), lambda i: (0, i), memory_space=pltpu.SMEM),
                  pl.BlockSpec((tt, d), lambda i: (i, 0))],
        out_specs=pl.BlockSpec(memory_space=pl.ANY),
        out_shape=jax.ShapeDtypeStruct((n_slots, d), hn.dtype),
        scratch_shapes=[pltpu.SemaphoreType.DMA(())],
        compiler_params=pltpu.CompilerParams(dimension_semantics=("arbitrary",)),
        name="dispatch",
    )(dest, hn)


MAX_UNIT_BLKS = 5


def _matmul_units(n_blocks):
    units, b = [], 0
    while b < n_blocks:
        size = min(MAX_UNIT_BLKS, n_blocks - b)
        units.append((b, size))
        b += size
    return units


def _expert_kernel(we_ref, ws_ref, wnb_ref, wvalid_ref, wact_ref,
                   xs_hbm, wg_ref, wu_ref, wd_ref, bg_ref, bu_ref, bd_ref, ys_hbm,
                   x_sb, acc, wgu_bf, wd_bf, stage, sem_in, sem_out):
    del we_ref, wact_ref
    w = pl.program_id(0)
    f = pl.program_id(1)
    n_items = wnb_ref.shape[0]
    last_f = N_FF_TILES - 1
    slot = w % 2
    nb = wnb_ref[w]
    row = lax.broadcasted_iota(I32, (ROW_BLK, 1), 0)

    def x_copy(item, j):
        src = xs_hbm.at[pl.ds(pl.multiple_of(ws_ref[item] + j * ROW_BLK, ROW_BLK), ROW_BLK)]
        return pltpu.make_async_copy(src, stage, sem_in)

    def x_cast(item, j, sl):
        r0 = pl.multiple_of(j * ROW_BLK, ROW_BLK)
        x_sb[sl, pl.ds(r0, ROW_BLK), :] = jnp.where(row + r0 < wvalid_ref[item], stage[...], 0.0).astype(BF16)

    def y_copy(item, j, sl):
        r0 = pl.multiple_of(j * ROW_BLK, ROW_BLK)
        dst = ys_hbm.at[pl.ds(pl.multiple_of(ws_ref[item] + r0, ROW_BLK), ROW_BLK)]
        return pltpu.make_async_copy(acc.at[sl, pl.ds(r0, ROW_BLK)], dst, sem_out.at[sl])

    def y_wait_all(item, sl):
        def body(j, c):
            y_copy(item, j, sl).wait()
            return c

        lax.fori_loop(0, wnb_ref[item], body, 0)

    @pl.when((w == 0) & (f == 0))
    def _first_rows():
        def body(j, c):
            cp = x_copy(0, j)
            cp.start()
            cp.wait()
            x_cast(0, j, 0)
            return c

        lax.fori_loop(0, wnb_ref[0], body, 0)

    nxt = jnp.minimum(w + 1, n_items - 1)
    prefetch = (w + 1 < n_items) & (f < wnb_ref[nxt])

    @pl.when(prefetch)
    def _():
        x_copy(nxt, f).start()

    @pl.when((f == 0) & (w >= 2))
    def _reclaim_acc():
        y_wait_all(jnp.maximum(w - 2, 0), slot)

    @pl.when(nb > 0)
    def _compute():
        bg = bg_ref[0]
        bu = bu_ref[0]
        bd = bd_ref[0]

        def cast_weights():
            wgu_bf[:, :FF_TILE] = wg_ref[0].astype(BF16)
            wgu_bf[:, FF_TILE:] = wu_ref[0].astype(BF16)
            wd_bf[...] = wd_ref[0].astype(BF16)

        def update(r0, rows, first):
            x = x_sb[slot, r0:r0 + rows, :]
            gu = jnp.dot(x, wgu_bf[...], preferred_element_type=F32)
            g = jnp.minimum(gu[:, :FF_TILE] + bg, SWIGLU_LIMIT)
            u = jnp.clip(gu[:, FF_TILE:] + bu, -SWIGLU_LIMIT, SWIGLU_LIMIT)
            hh = (u + 1.0) * (g * jax.nn.sigmoid(SWIGLU_ALPHA * g))
            part = jnp.dot(hh.astype(BF16), wd_bf[...], preferred_element_type=F32)
            if first:
                acc[slot, r0:r0 + rows, :] = part + bd
            else:
                acc[slot, r0:r0 + rows, :] += part

        def sweep(first):
            for n in range(1, SUPER_BLKS + 1):
                @pl.when(nb == n)
                def _(n=n):
                    cast_weights()
                    for b0, size in _matmul_units(n):
                        update(b0 * ROW_BLK, size * ROW_BLK, first)

        @pl.when(f == 0)
        def _():
            sweep(True)

        @pl.when(f > 0)
        def _():
            sweep(False)

    @pl.when((f == last_f) & (nb > 0))
    def _store():
        def body(j, c):
            y_copy(w, j, slot).start()
            return c

        lax.fori_loop(0, nb, body, 0)

    @pl.when(prefetch)
    def _():
        x_copy(nxt, f).wait()
        x_cast(nxt, f, 1 - slot)

    @pl.when((w == n_items - 1) & (f == last_f))
    def _drain():
        y_wait_all(jnp.maximum(w - 1, 0), 1 - slot)
        y_wait_all(w, slot)


def _experts(xs, tables, w_gate_up, b_gate_up, w_down, b_down):
    n_slots, d = xs.shape
    n_items = tables[0].shape[0]
    tf = FF_TILE
    nf = N_FF_TILES

    def ff(f, act, w):
        return f * act[w] + (nf - 1) * (1 - act[w])

    grid_spec = pltpu.PrefetchScalarGridSpec(
        num_scalar_prefetch=5,
        grid=(n_items, nf),
        in_specs=[
            pl.BlockSpec(memory_space=pl.ANY),
            pl.BlockSpec((1, d, tf), lambda w, f, we, ws, wnb, wv, act: (we[w], 0, ff(f, act, w))),
            pl.BlockSpec((1, d, tf), lambda w, f, we, ws, wnb, wv, act: (we[w], 0, nf + ff(f, act, w))),
            pl.BlockSpec((1, tf, d), lambda w, f, we, ws, wnb, wv, act: (we[w], ff(f, act, w), 0)),
            pl.BlockSpec((1, 1, tf), lambda w, f, we, ws, wnb, wv, act: (we[w], 0, ff(f, act, w))),
            pl.BlockSpec((1, 1, tf), lambda w, f, we, ws, wnb, wv, act: (we[w], 0, nf + ff(f, act, w))),
            pl.BlockSpec((1, 1, d), lambda w, f, we, ws, wnb, wv, act: (we[w], 0, 0)),
        ],
        out_specs=pl.BlockSpec(memory_space=pl.ANY),
        scratch_shapes=[
            pltpu.VMEM((2, SUPER_ROWS, d), BF16),
            pltpu.VMEM((2, SUPER_ROWS, d), F32),
            pltpu.VMEM((d, 2 * tf), BF16), pltpu.VMEM((tf, d), BF16),
            pltpu.VMEM((ROW_BLK, d), F32),
            pltpu.SemaphoreType.DMA(()), pltpu.SemaphoreType.DMA((2,)),
        ],
    )
    vmem = (2 * SUPER_ROWS * d * 6 + 2 * 3 * d * tf * 4 + 3 * d * tf * 2 + ROW_BLK * d * 4
            + 6 * ROW_BLK * d * 4 + (6 << 20))
    return pl.pallas_call(
        _expert_kernel,
        grid_spec=grid_spec,
        out_shape=jax.ShapeDtypeStruct((n_slots, d), F32),
        compiler_params=pltpu.CompilerParams(
            dimension_semantics=("arbitrary", "arbitrary"), vmem_limit_bytes=_vmem_limit(vmem)),
        name="experts",
    )(*tables, xs, w_gate_up, w_gate_up, w_down, b_gate_up.reshape(N_EXPERTS, 1, 2 * D_FF),
      b_gate_up.reshape(N_EXPERTS, 1, 2 * D_FF), b_down.reshape(N_EXPERTS, 1, d))


def _work_tables(counts, n_items):
    padded = (counts + ROW_BLK - 1) // ROW_BLK * ROW_BLK
    pstart = jnp.cumsum(padded) - padded
    items_e = (padded + SUPER_ROWS - 1) // SUPER_ROWS
    item_end = jnp.cumsum(items_e)
    item_start = item_end - items_e
    w = jnp.arange(n_items, dtype=I32)
    total = item_end[-1]
    active = (w < total).astype(I32)
    wl = jnp.minimum(w, total - 1)
    e = jnp.minimum(jnp.searchsorted(item_end, wl, side='right'), N_EXPERTS - 1).astype(I32)
    local = wl - item_start[e]
    start = pstart[e] + local * SUPER_ROWS
    rows = jnp.minimum(SUPER_ROWS, padded[e] - local * SUPER_ROWS)
    valid = jnp.clip(counts[e] - local * SUPER_ROWS, 0, rows)
    nb = (rows // ROW_BLK) * active
    return (e.astype(I32), start.astype(I32), nb.astype(I32), valid.astype(I32), active)


COMBINE_TILE = 512


def _combine_kernel(dest_ref, dest_next_ref, wts_ref, h_ref, g_ref, ys_hbm, out_ref, buf, sem, *, n):
    i = pl.program_id(0)
    slot = i % 2
    tt = h_ref.shape[0]

    def issue(dref, sl):
        def body(t, c):
            for k in range(TOP_K):
                src = ys_hbm.at[pl.ds(dref[0, TOP_K * t + k], 1)]
                pltpu.make_async_copy(src, buf.at[sl, k, pl.ds(t, 1)], sem.at[sl]).start(priority=k % 2)
            return c

        lax.fori_loop(0, tt, body, 0, unroll=DMA_UNROLL)

    def issue_unrolled(dref, sl):
        for t in range(tt):
            for k in range(TOP_K):
                src = ys_hbm.at[pl.ds(dref[0, TOP_K * t + k], 1)]
                pltpu.make_async_copy(src, buf.at[sl, k, pl.ds(t, 1)], sem.at[sl]).start(priority=k % 2)

    @pl.when(i == 0)
    def _():
        issue(dest_ref, 0)

    def drain(t, c):
        for k in range(TOP_K):
            pltpu.make_async_copy(ys_hbm.at[pl.ds(0, 1)], buf.at[slot, k, pl.ds(0, 1)], sem.at[slot]).wait()
        return c

    lax.fori_loop(0, tt, drain, 0, unroll=DMA_UNROLL)

    def finish():
        wts = wts_ref[...]
        moe = wts[:, 0:1] * buf[slot, 0]
        for k in range(1, TOP_K):
            moe = moe + wts[:, k:k + 1] * buf[slot, k]
        y = h_ref[...] + moe
        ms = jnp.mean(y * y, axis=-1, keepdims=True)
        out_ref[...] = (y * lax.rsqrt(ms + NORM_EPS)) * g_ref[...]

    if n > 1:
        @pl.when(i + 1 < n)
        def _():
            issue_unrolled(dest_next_ref, 1 - slot)
            finish()

        @pl.when(i + 1 >= n)
        def _():
            finish()
    else:
        finish()


def _combine(dest, wts, h, g_final, ys):
    t, d = h.shape
    tt = min(COMBINE_TILE, t)
    assert t % tt == 0
    n = t // tt
    return pl.pallas_call(
        functools.partial(_combine_kernel, n=n),
        grid=(n,),
        in_specs=[
            pl.BlockSpec((1, TOP_K * tt), lambda i: (0, i), memory_space=pltpu.SMEM),
            pl.BlockSpec((1, TOP_K * tt), lambda i: (0, jnp.minimum(i + 1, n - 1)), memory_space=pltpu.SMEM),
            pl.BlockSpec((tt, TOP_K), lambda i: (i, 0)),
            pl.BlockSpec((tt, d), lambda i: (i, 0)),
            pl.BlockSpec((1, d), lambda i: (0, 0)),
            pl.BlockSpec(memory_space=pl.ANY),
        ],
        out_specs=pl.BlockSpec((tt, d), lambda i: (i, 0)),
        out_shape=jax.ShapeDtypeStruct((t, d), F32),
        scratch_shapes=[pltpu.VMEM((2, TOP_K, tt, d), F32), pltpu.SemaphoreType.DMA((2,))],
        compiler_params=pltpu.CompilerParams(dimension_semantics=("arbitrary",)),
        name="combine",
    )(dest, dest, wts, h, g_final.reshape(1, d), ys)


def kernel(x_prompt, x_sample, cache_attn_k, cache_attn_v, state_conv, rel_bias_table, norm_mix_g, w_in, b_gate,
           attn_sinks, conv_w, w_o_attn, w_o_conv, w_out, norm_ffn_g, w_router, b_router, w_gate_up, b_gate_up,
           w_down, b_down, norm_final_g):
    batch, seq, d = x_prompt.shape
    dbatch, dseq, _ = x_sample.shape
    assert norm_mix_g.shape[0] == 1 and d == D_MODEL
    t_p, t_s = batch * seq, dbatch * dseq
    xp = x_prompt.reshape(t_p, d)
    xs = x_sample.reshape(t_s, d)

    woa = w_o_attn[0].astype(BF16)
    woc = w_o_conv[0].astype(BF16)
    wout = w_out[0].astype(BF16)
    wr_t = w_router[0].T.astype(BF16)

    w_in_bf = w_in[0].astype(BF16)
    zz_p = _in_proj(xp, norm_mix_g[0], w_in_bf, b_gate[0], tm=min(2048, t_p))
    zz_s = _in_proj(xs, norm_mix_g[0], w_in_bf, b_gate[0], tm=t_s)

    o_p = _attn_prompt(zz_p, batch, seq, rel_bias_table, attn_sinks[0])
    k_new = zz_s[:, ZC_K:ZC_K + KV_DIM].reshape(dbatch, dseq, KV_DIM)
    v_new = zz_s[:, ZC_V:ZC_V + KV_DIM].reshape(dbatch, dseq, KV_DIM)
    k_ext = jnp.concatenate([cache_attn_k[0].reshape(dbatch, WINDOW, KV_DIM), k_new], axis=1)
    v_ext = jnp.concatenate([cache_attn_v[0].reshape(dbatch, WINDOW, KV_DIM), v_new], axis=1)
    o_s = _attn_sample(zz_s, k_ext, v_ext, dbatch, dseq, rel_bias_table, attn_sinks[0])

    hist0 = jnp.repeat(state_conv[0][:, 0, :], dseq, axis=0)
    hist1 = jnp.repeat(state_conv[0][:, 1, :], dseq, axis=0)
    merge_args = (conv_w[0], woa, woc, wout, norm_ffn_g[0], wr_t, b_router[0])
    t_all = t_p + t_s
    h_p, hn, idx_p, wts_p = _merge(o_p, zz_p, xp, None, None, *merge_args, tm=256, seq_rows=None, seq_len=seq,
                                   hn_rows=t_all, hn_row0=0, hn_prev=None)
    h_s, hn, idx_s, wts_s = _merge(o_s, zz_s, xs, hist0, hist1, *merge_args, tm=t_s, seq_rows=dseq, seq_len=dseq,
                                   hn_rows=t_all, hn_row0=t_p, hn_prev=hn)

    idx_t = jnp.concatenate([idx_p, idx_s], axis=1)
    wts = jnp.concatenate([wts_p, wts_s], axis=1).T
    n_assign = t_all * TOP_K
    n_slots = (n_assign + N_EXPERTS * (ROW_BLK - 1) + ROW_BLK - 1) // ROW_BLK * ROW_BLK
    n_items = N_EXPERTS + (n_assign - N_EXPERTS) // SUPER_ROWS

    dest, cnt = _route(idx_t)
    tables = _work_tables(cnt[:, 0], n_items)
    dest = dest.T.reshape(1, -1)
    xs_sorted = _dispatch(dest, hn, n_slots)
    ys = _experts(xs_sorted, tables, w_gate_up[0], b_gate_up[0], w_down[0], b_down[0])
    y_p = _combine(dest[:, :TOP_K * t_p], wts[:t_p], h_p, norm_final_g, ys)
    y_s = _combine(dest[:, TOP_K * t_p:], wts[t_p:], h_s, norm_final_g, ys)

    zz_p3 = zz_p.reshape(batch, seq, Z_DIM)

    def kv_tail(c0):
        return zz_p3[:, seq - WINDOW:, c0:c0 + KV_DIM].reshape(batch, WINDOW, N_KV_HEADS, HEAD_DIM)[None]

    u_s = zz_s[:, ZC_U:ZC_U + CONV_DIM].reshape(dbatch, dseq, CONV_DIM)
    conv_hist = state_conv.shape[2]
    new_conv_p = zz_p3[:, seq - conv_hist:, ZC_U:ZC_U + CONV_DIM][None]
    new_conv_s = jnp.concatenate([state_conv[0], u_s], axis=1)[:, -conv_hist:][None]
    new_k_s = k_ext[:, -WINDOW:].reshape(dbatch, WINDOW, N_KV_HEADS, HEAD_DIM)[None]
    new_v_s = v_ext[:, -WINDOW:].reshape(dbatch, WINDOW, N_KV_HEADS, HEAD_DIM)[None]
    return (y_p.reshape(batch, seq, d), y_s.reshape(dbatch, dseq, d),
            kv_tail(ZC_K), kv_tail(ZC_V), new_conv_p, new_k_s, new_v_s, new_conv_s)
```

```python
import functools
import math

import numpy as np
import jax
import jax.numpy as jnp
from jax import lax
from jax.experimental import pallas as pl
from jax.experimental.pallas import tpu as pltpu

F32 = jnp.float32
BF16 = jnp.bfloat16
I32 = jnp.int32

D_MODEL = 2048
CHUNK = 64
HEAD_DIM = 64
N_Q_HEADS = 16
N_KV_HEADS = 4
GROUP = N_Q_HEADS // N_KV_HEADS
ATTN_DIM = N_Q_HEADS * HEAD_DIM
KV_DIM = N_KV_HEADS * HEAD_DIM
WINDOW = 128
CONV_DIM = D_MODEL // 2
N_BUCKETS = 32
MAX_DISTANCE = 128
N_EXPERTS = 32
TOP_K = 4
D_FF = D_MODEL
SWIGLU_LIMIT = 7.0
SWIGLU_ALPHA = 1.702
NORM_EPS = 1e-5
IN_DIM = ATTN_DIM + 2 * KV_DIM + 3 * CONV_DIM + 2 * D_MODEL

V7X_VMEM_BYTES = 64 * 1024 * 1024
LANES = 128

ZC_GA = 0
ZC_GC = D_MODEL
ZC_CB = 2 * D_MODEL
ZC_U = ZC_CB + CONV_DIM
ZC_Q = ZC_U + CONV_DIM
ZC_K = ZC_Q + ATTN_DIM
ZC_V = ZC_K + KV_DIM
Z_DIM = ZC_V + KV_DIM
PROJ_TN = 512

ATTN_Q_ROWS = 128

ROW_BLK = 256
SUPER_BLKS = 5
SUPER_ROWS = SUPER_BLKS * ROW_BLK
FF_TILE = 256
N_FF_TILES = D_FF // FF_TILE
assert SUPER_BLKS <= N_FF_TILES


def _vmem_limit(nbytes):
    return int(min(nbytes, V7X_VMEM_BYTES - 6 * 1024 * 1024))


_MODE_PLAIN, _MODE_STASH0, _MODE_STASH1, _MODE_MUL0, _MODE_MUL1, _MODE_GATE = range(6)


def _proj_schedule():
    t = PROJ_TN
    cc0 = (ATTN_DIM + 2 * KV_DIM + CONV_DIM) // t
    ch0 = (ATTN_DIM + 2 * KV_DIM + 2 * CONV_DIM) // t
    cb0 = (ATTN_DIM + 2 * KV_DIM) // t
    ga0 = (ATTN_DIM + 2 * KV_DIM + 3 * CONV_DIM) // t
    steps = [
        (cc0, ZC_U // t, 0, _MODE_STASH0),
        (cc0 + 1, ZC_U // t, 0, _MODE_STASH1),
        (ch0, ZC_U // t, 0, _MODE_MUL0),
        (ch0 + 1, ZC_U // t + 1, 0, _MODE_MUL1),
        (0, ZC_Q // t, 0, _MODE_PLAIN),
        (1, ZC_Q // t + 1, 0, _MODE_PLAIN),
        (2, ZC_K // t, 0, _MODE_PLAIN),
        (cb0, ZC_CB // t, 0, _MODE_PLAIN),
        (cb0 + 1, ZC_CB // t + 1, 0, _MODE_PLAIN),
    ]
    for j in range(2 * D_MODEL // t):
        steps.append((ga0 + j, j, j, _MODE_GATE))
    return np.asarray(steps, dtype=np.int32)


def _in_proj_kernel(wt_ref, ot_ref, bt_ref, md_ref, x_hbm, g_ref, w_ref, b_ref, o_ref, x_buf, xn_ref, stash_ref, sem,
                    *, n_tiles):
    del wt_ref, ot_ref, bt_ref
    m = pl.program_id(0)
    n = pl.program_id(1)
    tm = x_buf.shape[0]

    def x_copy(tile):
        return pltpu.make_async_copy(x_hbm.at[pl.ds(pl.multiple_of(tile * tm, 8), tm)], x_buf, sem)

    def product():
        return jnp.dot(xn_ref[...], w_ref[...], preferred_element_type=F32)

    md = md_ref[n]

    @pl.when(md == _MODE_PLAIN)
    def _():
        o_ref[...] = product()

    @pl.when(md == _MODE_STASH0)
    def _():
        @pl.when(m == 0)
        def _():
            x_copy(0).start()

        x_copy(m).wait()
        x = x_buf[...]
        ms = jnp.mean(x * x, axis=-1, keepdims=True)
        xn_ref[...] = ((x * lax.rsqrt(ms + NORM_EPS)) * g_ref[...]).astype(BF16)
        stash_ref[0] = product()

    @pl.when(md == _MODE_STASH1)
    def _():
        if n_tiles > 1:
            @pl.when(m + 1 < n_tiles)
            def _():
                x_copy(m + 1).start()

        stash_ref[1] = product()

    @pl.when(md == _MODE_MUL0)
    def _():
        o_ref[...] = stash_ref[0] * product()

    @pl.when(md == _MODE_MUL1)
    def _():
        o_ref[...] = stash_ref[1] * product()

    @pl.when(md == _MODE_GATE)
    def _():
        o_ref[...] = jax.nn.sigmoid(product() + b_ref[...])


def _in_proj(x, g_mix, w_in, b_gate, tm):
    t, d = x.shape
    sched = _proj_schedule()
    n_steps = sched.shape[0]
    n_tiles = t // tm
    grid_spec = pltpu.PrefetchScalarGridSpec(
        num_scalar_prefetch=4,
        grid=(n_tiles, n_steps),
        in_specs=[
            pl.BlockSpec(memory_space=pl.ANY),
            pl.BlockSpec((1, d), lambda m, n, wt, ot, bt, md: (0, 0)),
            pl.BlockSpec((d, PROJ_TN), lambda m, n, wt, ot, bt, md: (0, wt[n])),
            pl.BlockSpec((1, PROJ_TN), lambda m, n, wt, ot, bt, md: (0, bt[n])),
        ],
        out_specs=pl.BlockSpec((tm, PROJ_TN), lambda m, n, wt, ot, bt, md: (m, ot[n])),
        scratch_shapes=[pltpu.VMEM((tm, d), F32), pltpu.VMEM((tm, d), BF16), pltpu.VMEM((2, tm, PROJ_TN), F32),
                        pltpu.SemaphoreType.DMA(())],
    )
    vmem = tm * d * 4 + tm * d * 2 + 2 * d * PROJ_TN * 2 + 6 * tm * PROJ_TN * 4 + (8 << 20)
    return pl.pallas_call(
        functools.partial(_in_proj_kernel, n_tiles=n_tiles),
        grid_spec=grid_spec,
        out_shape=jax.ShapeDtypeStruct((t, Z_DIM), F32),
        compiler_params=pltpu.CompilerParams(
            dimension_semantics=("arbitrary", "arbitrary"), vmem_limit_bytes=_vmem_limit(vmem)),
        name="in_proj",
    )(jnp.asarray(sched[:, 0]), jnp.asarray(sched[:, 1]), jnp.asarray(sched[:, 2]), jnp.asarray(sched[:, 3]),
      x, g_mix.reshape(1, d), w_in, b_gate.reshape(1, 2 * D_MODEL))


def _t5_buckets(n_q, n_k):
    half = N_BUCKETS // 2
    max_exact = half // 2
    rel = (np.arange(n_k, dtype=np.int32)[None, :] - WINDOW) - np.arange(n_q, dtype=np.int32)[:, None]
    ret = np.where(rel > 0, half, 0)
    n = np.abs(rel)
    nf = np.maximum(n, 1).astype(np.float32)
    large = max_exact + (np.log(nf / np.float32(max_exact)) / np.float32(math.log(MAX_DISTANCE / max_exact))
                         * np.float32(half - max_exact)).astype(np.int32)
    large = np.minimum(large, half - 1)
    return (ret + np.where(n < max_exact, n, large)).astype(np.int32)


def _build_bias(bias_ref, bucket_ref, table_ref, n_q):
    bucket = bucket_ref[...]
    for kvh in range(N_KV_HEADS):
        for g in range(GROUP):
            h = kvh * GROUP + g
            mat = jnp.zeros(bucket.shape, F32)
            for b in range(N_BUCKETS):
                mat = jnp.where(bucket == b, table_ref[b, h], mat)
            bias_ref[kvh, g * n_q:(g + 1) * n_q, :] = mat


def _attend(q, kwin, vwin, bias_ref, sinks_ref, n_q, invalid):
    outs = []
    grp = lax.broadcasted_iota(I32, (GROUP * n_q, 1), 0) // n_q
    for kvh in range(N_KV_HEADS):
        c0 = kvh * GROUP * HEAD_DIM
        qh = jnp.concatenate([q[:, c0 + g * HEAD_DIM:c0 + (g + 1) * HEAD_DIM] for g in range(GROUP)], axis=0)
        qh = (qh * (HEAD_DIM ** -0.5)).astype(BF16)
        kh = kwin[:, kvh * HEAD_DIM:(kvh + 1) * HEAD_DIM].astype(BF16)
        vh = vwin[:, kvh * HEAD_DIM:(kvh + 1) * HEAD_DIM].astype(BF16)
        s = lax.dot_general(qh, kh, (((1,), (1,)), ((), ())), preferred_element_type=F32) + bias_ref[kvh]
        if invalid is not None:
            s = jnp.where(invalid, -1e30, s)
        sink = jnp.zeros((GROUP * n_q, 1), F32)
        for g in range(GROUP):
            sink = jnp.where(grp == g, sinks_ref[kvh * GROUP + g], sink)
        m = jnp.maximum(jnp.max(s, axis=-1, keepdims=True), sink)
        p = jnp.exp(s - m)
        denom = jnp.sum(p, axis=-1, keepdims=True) + jnp.exp(sink - m)
        o = jnp.dot(p.astype(BF16), vh, preferred_element_type=F32) / denom
        outs.append(o)
    return outs


def _store_heads(o_ref, r0, n_q, outs):
    for kvh, o in enumerate(outs):
        for g in range(GROUP):
            c = (kvh * GROUP + g) * HEAD_DIM
            o_ref[r0:r0 + n_q, c:c + HEAD_DIM] = o[g * n_q:(g + 1) * n_q].astype(o_ref.dtype)


def _attn_prompt_kernel(table_ref, sinks_ref, bucket_ref, q_ref, kp_ref, kc_ref, vp_ref, vc_ref, o_ref, bias_ref):
    first = (pl.program_id(0) == 0) & (pl.program_id(1) == 0)

    @pl.when(first)
    def _():
        _build_bias(bias_ref, bucket_ref, table_ref, CHUNK)

    kwin = jnp.concatenate([kp_ref[...], kc_ref[...]], axis=0)
    vwin = jnp.concatenate([vp_ref[...], vc_ref[...]], axis=0)
    n_k = WINDOW + CHUNK
    seq_start = pl.program_id(1) == 0
    col = lax.broadcasted_iota(I32, (1, n_k), 1)
    for c in range(q_ref.shape[0] // CHUNK):
        r0 = c * CHUNK
        invalid = seq_start & (col + r0 < WINDOW)
        outs = _attend(q_ref[r0:r0 + CHUNK, :], kwin[r0:r0 + n_k], vwin[r0:r0 + n_k], bias_ref, sinks_ref,
                       CHUNK, invalid)
        _store_heads(o_ref, r0, CHUNK, outs)


def _attn_prompt(zz, batch, seq, table, sinks):
    qb = ATTN_Q_ROWS
    assert seq % qb == 0 and qb % WINDOW == 0
    nb = seq // qb
    wpq = qb // WINDOW
    n_k = WINDOW + CHUNK
    bucket = jnp.asarray(_t5_buckets(CHUNK, n_k))
    smem = pl.BlockSpec(memory_space=pltpu.SMEM)
    kblk, vblk = ZC_K // KV_DIM, ZC_V // KV_DIM
    return pl.pallas_call(
        _attn_prompt_kernel,
        grid=(batch, nb),
        in_specs=[
            smem, smem,
            pl.BlockSpec((CHUNK, n_k), lambda b, j: (0, 0)),
            pl.BlockSpec((qb, ATTN_DIM), lambda b, j: (b * nb + j, ZC_Q // ATTN_DIM)),
            pl.BlockSpec((WINDOW, KV_DIM), lambda b, j: ((b * nb + j) * wpq - jnp.minimum(j, 1), kblk)),
            pl.BlockSpec((qb, KV_DIM), lambda b, j: (b * nb + j, kblk)),
            pl.BlockSpec((WINDOW, KV_DIM), lambda b, j: ((b * nb + j) * wpq - jnp.minimum(j, 1), vblk)),
            pl.BlockSpec((qb, KV_DIM), lambda b, j: (b * nb + j, vblk)),
        ],
        out_specs=pl.BlockSpec((qb, ATTN_DIM), lambda b, j: (b * nb + j, 0)),
        out_shape=jax.ShapeDtypeStruct((batch * seq, ATTN_DIM), BF16),
        scratch_shapes=[pltpu.VMEM((N_KV_HEADS, GROUP * CHUNK, n_k), F32)],
        compiler_params=pltpu.CompilerParams(dimension_semantics=("arbitrary", "arbitrary")),
        name="attn_prompt",
    )(table, sinks, bucket, zz, zz, zz, zz, zz)


def _attn_sample_kernel(table_ref, sinks_ref, bucket_ref, q_ref, k_ref, v_ref, o_ref, bias_ref, *, n_q):
    @pl.when(pl.program_id(0) == 0)
    def _():
        _build_bias(bias_ref, bucket_ref, table_ref, n_q)

    outs = _attend(q_ref[...], k_ref[0], v_ref[0], bias_ref, sinks_ref, n_q, None)
    _store_heads(o_ref, 0, n_q, outs)


def _attn_sample(zz, k_ext, v_ext, batch, n_q, table, sinks):
    n_k = WINDOW + n_q
    bucket = jnp.asarray(_t5_buckets(n_q, n_k))
    smem = pl.BlockSpec(memory_space=pltpu.SMEM)
    return pl.pallas_call(
        functools.partial(_attn_sample_kernel, n_q=n_q),
        grid=(batch,),
        in_specs=[
            smem, smem,
            pl.BlockSpec((n_q, n_k), lambda b: (0, 0)),
            pl.BlockSpec((n_q, ATTN_DIM), lambda b: (b, ZC_Q // ATTN_DIM)),
            pl.BlockSpec((1, n_k, KV_DIM), lambda b: (b, 0, 0)),
            pl.BlockSpec((1, n_k, KV_DIM), lambda b: (b, 0, 0)),
        ],
        out_specs=pl.BlockSpec((n_q, ATTN_DIM), lambda b: (b, 0)),
        out_shape=jax.ShapeDtypeStruct((batch * n_q, ATTN_DIM), BF16),
        scratch_shapes=[pltpu.VMEM((N_KV_HEADS, GROUP * n_q, n_k), F32)],
        compiler_params=pltpu.CompilerParams(dimension_semantics=("arbitrary",)),
        name="attn_sample",
    )(table, sinks, bucket, zz, k_ext, v_ext)


_MERGE_INPUTS = 15


def _merge_kernel(*refs, tm, seq_rows, tiles_per_seq):
    (o_ref, ga_ref, gc_ref, cb_ref, u_ref, h0_ref, h1_ref, x_ref, cw_ref, woa_ref, woc_ref, wout_ref,
     gf_ref, wr_ref, br_ref) = refs[:_MERGE_INPUTS]
    h_ref, hn_ref, idx_ref, wts_ref = refs[-4:]
    u = u_ref[...]
    row = lax.broadcasted_iota(I32, (tm, 1), 0)
    if seq_rows is None:
        at_start = (pl.program_id(0) % tiles_per_seq) == 0
        hist0 = jnp.where(at_start, 0.0, h0_ref[6:7, :])
        hist1 = jnp.where(at_start, 0.0, h1_ref[7:8, :])
        pos = row
    else:
        hist0 = h0_ref[...]
        hist1 = h1_ref[...]
        pos = row % seq_rows
    r1 = pltpu.roll(u, 1, 0)
    r2 = pltpu.roll(u, 2, 0)
    u1 = jnp.where(pos == 0, hist1, r1)
    u2 = jnp.where(pos == 0, hist0, jnp.where(pos == 1, hist1, r2))
    cw = cw_ref[...]
    y = cw[0:1, :] * u2
    y = y + cw[1:2, :] * u1
    y = y + cw[2:3, :] * u
    c = (cb_ref[...] * y).astype(BF16)
    t = ga_ref[...] * jnp.dot(o_ref[...], woa_ref[...], preferred_element_type=F32)
    t = t + gc_ref[...] * jnp.dot(c, woc_ref[...], preferred_element_type=F32)
    h = x_ref[...] + jnp.dot(t.astype(BF16), wout_ref[...], preferred_element_type=F32)
    h_ref[...] = h
    ms = jnp.mean(h * h, axis=-1, keepdims=True)
    hn = (h * lax.rsqrt(ms + NORM_EPS)) * gf_ref[...]
    hn_ref[...] = hn
    logits = lax.dot_general(wr_ref[...], hn.astype(BF16), (((1,), (1,)), ((), ())),
                             preferred_element_type=F32) + br_ref[...]
    erow = lax.broadcasted_iota(I32, logits.shape, 0).astype(F32)
    vals, idxs = [], []
    for _ in range(TOP_K):
        mx = jnp.max(logits, axis=0, keepdims=True)
        ix = jnp.min(jnp.where(logits == mx, erow, float(N_EXPERTS)), axis=0, keepdims=True)
        vals.append(mx)
        idxs.append(ix)
        logits = jnp.where(erow == ix, -jnp.inf, logits)
    es = [jnp.exp(v - vals[0]) for v in vals]
    tot = es[0]
    for e in es[1:]:
        tot = tot + e
    idx_ref[...] = jnp.concatenate(idxs, axis=0).astype(I32)
    wts_ref[...] = jnp.concatenate([e / tot for e in es], axis=0)


def _merge(o, zz, x, hist0, hist1, conv_w, woa, woc, wout, g_ffn, wr_t, br, tm, seq_rows, seq_len,
           hn_rows, hn_row0, hn_prev):
    t = x.shape[0]
    d = D_MODEL
    assert hn_row0 % tm == 0
    extra_specs, extra_args, aliases = [], [], {}
    if hn_prev is not None:
        extra_specs, extra_args, aliases = [pl.BlockSpec(memory_space=pl.ANY)], [hn_prev], {_MERGE_INPUTS: 1}
    full = lambda shape: pl.BlockSpec(shape, lambda i: (0,) * len(shape))
    if seq_rows is None:
        tiles_per_seq = seq_len // tm
        hist_spec = pl.BlockSpec((8, CONV_DIM), lambda i: (jnp.maximum(i * (tm // 8) - 1, 0), ZC_U // CONV_DIM))
        h0_spec = h1_spec = hist_spec
        h0_arg = h1_arg = zz
    else:
        tiles_per_seq = None
        h0_spec = h1_spec = pl.BlockSpec((tm, CONV_DIM), lambda i: (i, 0))
        h0_arg, h1_arg = hist0, hist1
    kern = functools.partial(_merge_kernel, tm=tm, seq_rows=seq_rows, tiles_per_seq=tiles_per_seq)
    vmem = (2 * tm * (ATTN_DIM * 2 + 5 * d * 4 + 4 * CONV_DIM * 4) + 2 * 2 * (2 * ATTN_DIM * d + d * d)
            + 10 * tm * d * 4 + (8 << 20))
    return pl.pallas_call(
        kern,
        grid=(t // tm,),
        in_specs=[
            pl.BlockSpec((tm, ATTN_DIM), lambda i: (i, 0)),
            pl.BlockSpec((tm, d), lambda i: (i, ZC_GA // d)),
            pl.BlockSpec((tm, d), lambda i: (i, ZC_GC // d)),
            pl.BlockSpec((tm, CONV_DIM), lambda i: (i, ZC_CB // CONV_DIM)),
            pl.BlockSpec((tm, CONV_DIM), lambda i: (i, ZC_U // CONV_DIM)),
            h0_spec, h1_spec,
            pl.BlockSpec((tm, d), lambda i: (i, 0)),
            full((3, CONV_DIM)),
            full((ATTN_DIM, d)), full((CONV_DIM, d)), full((d, d)),
            full((1, d)), full((N_EXPERTS, d)), full((N_EXPERTS, 1)),
        ] + extra_specs,
        out_specs=[
            pl.BlockSpec((tm, d), lambda i: (i, 0)),
            pl.BlockSpec((tm, d), lambda i: (hn_row0 // tm + i, 0)),
            pl.BlockSpec((TOP_K, tm), lambda i: (0, i)),
            pl.BlockSpec((TOP_K, tm), lambda i: (0, i)),
        ],
        out_shape=[
            jax.ShapeDtypeStruct((t, d), F32),
            jax.ShapeDtypeStruct((hn_rows, d), F32),
            jax.ShapeDtypeStruct((TOP_K, t), I32),
            jax.ShapeDtypeStruct((TOP_K, t), F32),
        ],
        input_output_aliases=aliases,
        compiler_params=pltpu.CompilerParams(dimension_semantics=("arbitrary",), vmem_limit_bytes=_vmem_limit(vmem)),
        name="merge_prompt" if seq_rows is None else "merge_sample",
    )(o, zz, zz, zz, zz, h0_arg, h1_arg, x, conv_w, woa, woc, wout, g_ffn.reshape(1, d), wr_t,
      br.reshape(N_EXPERTS, 1), *extra_args)


ROUTE_TILE = 32768
ROUTE_SUB = 384


def _route_kernel(idx_ref, dest_ref, cnt_ref, cnt_acc, carry, pst, tri_ref):
    p = pl.program_id(0)
    j = pl.program_id(1)
    sub = tri_ref.shape[0]
    n_sub = idx_ref.shape[1] // sub
    erow = lax.broadcasted_iota(I32, (N_EXPERTS, sub), 0)

    def membership(s):
        idx = idx_ref[:, pl.ds(pl.multiple_of(s * sub, LANES), sub)]
        masks = [(erow == idx[k:k + 1, :]).astype(F32) for k in range(TOP_K)]
        mtot = masks[0]
        for m in masks[1:]:
            mtot = mtot + m
        return masks, mtot, jnp.sum(mtot, axis=1, keepdims=True)

    @pl.when((p == 0) & (j == 0))
    def _():
        cnt_acc[...] = jnp.zeros_like(cnt_acc)

    @pl.when(p == 0)
    def _():
        def body(s, c):
            _, _, piece_cnt = membership(s)
            cnt_acc[...] += jnp.broadcast_to(piece_cnt, cnt_acc.shape)
            return c

        lax.fori_loop(0, n_sub, body, 0)

    @pl.when((p == 1) & (j == 0))
    def _():
        cnt = cnt_acc[:, 0:1]
        padded = jnp.floor((cnt + float(ROW_BLK - 1)) / float(ROW_BLK)) * float(ROW_BLK)
        r = lax.broadcasted_iota(I32, (N_EXPERTS, N_EXPERTS), 0)
        c = lax.broadcasted_iota(I32, (N_EXPERTS, N_EXPERTS), 1)
        rowv = jnp.sum(jnp.where(r == c, padded, 0.0), axis=0, keepdims=True)
        start = jnp.sum(jnp.where(c < r, rowv, 0.0), axis=1, keepdims=True)
        pst[...] = jnp.broadcast_to(start, pst.shape)
        carry[...] = jnp.zeros_like(carry)
        a = lax.broadcasted_iota(I32, (sub, sub), 0)
        b = lax.broadcasted_iota(I32, (sub, sub), 1)
        tri_ref[...] = (a < b).astype(BF16)

    @pl.when(p == 1)
    def _():
        def body(s, c):
            masks, mtot, piece_cnt = membership(s)
            excl = jnp.dot(mtot.astype(BF16), tri_ref[...], preferred_element_type=F32)
            val = pst[:, 0:1] + carry[:, 0:1] + excl
            dest = jnp.concatenate([jnp.sum(m * val, axis=0, keepdims=True) for m in masks], axis=0)
            dest_ref[:, pl.ds(pl.multiple_of(s * sub, LANES), sub)] = dest.astype(I32)
            carry[...] += jnp.broadcast_to(piece_cnt, carry.shape)
            return c

        lax.fori_loop(0, n_sub, body, 0)
        cnt_ref[...] = cnt_acc[...].astype(I32)


def _route(idx_t):
    t = idx_t.shape[1]
    assert t % ROUTE_SUB == 0
    tr = max(c for c in range(ROUTE_SUB, min(ROUTE_TILE, t) + 1, ROUTE_SUB) if t % c == 0)
    return pl.pallas_call(
        _route_kernel,
        grid=(2, t // tr),
        in_specs=[pl.BlockSpec((TOP_K, tr), lambda p, j: (0, j))],
        out_specs=[
            pl.BlockSpec((TOP_K, tr), lambda p, j: (0, j * p)),
            pl.BlockSpec((N_EXPERTS, LANES), lambda p, j: (0, 0)),
        ],
        out_shape=[jax.ShapeDtypeStruct((TOP_K, t), I32), jax.ShapeDtypeStruct((N_EXPERTS, LANES), I32)],
        scratch_shapes=[pltpu.VMEM((N_EXPERTS, LANES), F32), pltpu.VMEM((N_EXPERTS, LANES), F32),
                        pltpu.VMEM((N_EXPERTS, LANES), F32), pltpu.VMEM((ROUTE_SUB, ROUTE_SUB), BF16)],
        compiler_params=pltpu.CompilerParams(dimension_semantics=("arbitrary", "arbitrary")),
        name="route",
    )(idx_t)


DISPATCH_TILE = 1376
DMA_UNROLL = 8


def _dispatch_kernel(dest_ref, hn_ref, xs_hbm, sem):
    tt = hn_ref.shape[0]

    def issue(t, c):
        for k in range(TOP_K):
            dst = xs_hbm.at[pl.ds(dest_ref[0, TOP_K * t + k], 1)]
            pltpu.make_async_copy(hn_ref.at[pl.ds(t, 1)], dst, sem).start(priority=k % 2)
        return c

    lax.fori_loop(0, tt, issue, 0, unroll=DMA_UNROLL)

    def drain(t, c):
        for k in range(TOP_K):
            pltpu.make_async_copy(hn_ref.at[pl.ds(0, 1)], xs_hbm.at[pl.ds(0, 1)], sem).wait()
        return c

    lax.fori_loop(0, tt, drain, 0, unroll=DMA_UNROLL)


def _dispatch(dest, hn, n_slots):
    t, d = hn.shape
    tt = max(c for c in range(32, DISPATCH_TILE + 1, 32) if t % c == 0)
    return pl.pallas_call(
        _dispatch_kernel,
        grid=(t // tt,),
        in_specs=[pl.BlockSpec((1, TOP_K * tt), lambda i: (0, i), memory_space=pltpu.SMEM),
                  pl.BlockSpec((tt, d), lambda i: (i, 0))],
        out_specs=pl.BlockSpec(memory_space=pl.ANY),
        out_shape=jax.ShapeDtypeStruct((n_slots, d), hn.dtype),
        scratch_shapes=[pltpu.SemaphoreType.DMA(())],
        compiler_params=pltpu.CompilerParams(dimension_semantics=("arbitrary",)),
        name="dispatch",
    )(dest, hn)


MAX_UNIT_BLKS = 5


def _matmul_units(n_blocks):
    units, b = [], 0
    while b < n_blocks:
        size = min(MAX_UNIT_BLKS, n_blocks - b)
        units.append((b, size))
        b += size
    return units


def _expert_kernel(we_ref, ws_ref, wnb_ref, wvalid_ref, wact_ref, wtot_ref,
                   xs_hbm, wg_ref, wu_ref, wd_ref, bg_ref, bu_ref, bd_ref, ys_hbm,
                   x_sb, acc, wgu_bf, wd_bf, stage, sem_in, sem_out):
    del we_ref, wact_ref
    w = pl.program_id(0)
    f = pl.program_id(1)
    n_items = wtot_ref[0]
    last_f = N_FF_TILES - 1
    slot = w % 2
    nb = wnb_ref[w]
    row = lax.broadcasted_iota(I32, (ROW_BLK, 1), 0)

    def x_copy(item, j):
        src = xs_hbm.at[pl.ds(pl.multiple_of(ws_ref[item] + j * ROW_BLK, ROW_BLK), ROW_BLK)]
        return pltpu.make_async_copy(src, stage, sem_in)

    def x_cast(item, j, sl):
        r0 = pl.multiple_of(j * ROW_BLK, ROW_BLK)
        x_sb[sl, pl.ds(r0, ROW_BLK), :] = jnp.where(row + r0 < wvalid_ref[item], stage[...], 0.0).astype(BF16)

    def y_copy(item, j, sl):
        r0 = pl.multiple_of(j * ROW_BLK, ROW_BLK)
        dst = ys_hbm.at[pl.ds(pl.multiple_of(ws_ref[item] + r0, ROW_BLK), ROW_BLK)]
        return pltpu.make_async_copy(acc.at[sl, pl.ds(r0, ROW_BLK)], dst, sem_out.at[sl])

    def y_wait_all(item, sl):
        def body(j, c):
            y_copy(item, j, sl).wait()
            return c

        lax.fori_loop(0, wnb_ref[item], body, 0)

    @pl.when((w == 0) & (f == 0))
    def _first_rows():
        def body(j, c):
            cp = x_copy(0, j)
            cp.start()
            cp.wait()
            x_cast(0, j, 0)
            return c

        lax.fori_loop(0, wnb_ref[0], body, 0)

    nxt = jnp.minimum(w + 1, n_items - 1)
    prefetch = (w + 1 < n_items) & (f < wnb_ref[nxt])

    @pl.when(prefetch)
    def _():
        x_copy(nxt, f).start()

    @pl.when((f == 0) & (w >= 2))
    def _reclaim_acc():
        y_wait_all(jnp.maximum(w - 2, 0), slot)

    @pl.when(nb > 0)
    def _compute():
        bg = bg_ref[0]
        bu = bu_ref[0]
        bd = bd_ref[0]

        def cast_weights():
            wgu_bf[:, :FF_TILE] = wg_ref[0].astype(BF16)
            wgu_bf[:, FF_TILE:] = wu_ref[0].astype(BF16)
            wd_bf[...] = wd_ref[0].astype(BF16)

        def update(r0, rows, first):
            x = x_sb[slot, r0:r0 + rows, :]
            gu = jnp.dot(x, wgu_bf[...], preferred_element_type=F32)
            g = jnp.minimum(gu[:, :FF_TILE] + bg, SWIGLU_LIMIT)
            u = jnp.clip(gu[:, FF_TILE:] + bu, -SWIGLU_LIMIT, SWIGLU_LIMIT)
            hh = (u + 1.0) * (g * jax.nn.sigmoid(SWIGLU_ALPHA * g))
            part = jnp.dot(hh.astype(BF16), wd_bf[...], preferred_element_type=F32)
            if first:
                acc[slot, r0:r0 + rows, :] = part + bd
            else:
                acc[slot, r0:r0 + rows, :] += part

        def sweep(first):
            for n in range(1, SUPER_BLKS + 1):
                @pl.when(nb == n)
                def _(n=n):
                    cast_weights()
                    for b0, size in _matmul_units(n):
                        update(b0 * ROW_BLK, size * ROW_BLK, first)

        @pl.when(f == 0)
        def _():
            sweep(True)

        @pl.when(f > 0)
        def _():
            sweep(False)

    @pl.when((f == last_f) & (nb > 0))
    def _store():
        def body(j, c):
            y_copy(w, j, slot).start()
            return c

        lax.fori_loop(0, nb, body, 0)

    @pl.when(prefetch)
    def _():
        x_copy(nxt, f).wait()
        x_cast(nxt, f, 1 - slot)

    @pl.when((w == n_items - 1) & (f == last_f))
    def _drain():
        y_wait_all(jnp.maximum(w - 1, 0), 1 - slot)
        y_wait_all(w, slot)


def _experts(xs, tables, w_gate_up, b_gate_up, w_down, b_down):
    n_slots, d = xs.shape
    total = jnp.sum(tables[4]).astype(I32).reshape(1)
    tf = FF_TILE
    nf = N_FF_TILES

    def ff(f, act, w):
        return f * act[w] + (nf - 1) * (1 - act[w])

    grid_spec = pltpu.PrefetchScalarGridSpec(
        num_scalar_prefetch=6,
        grid=(total[0], nf),
        in_specs=[
            pl.BlockSpec(memory_space=pl.ANY),
            pl.BlockSpec((1, d, tf), lambda w, f, we, ws, wnb, wv, act, tot: (we[w], 0, ff(f, act, w))),
            pl.BlockSpec((1, d, tf), lambda w, f, we, ws, wnb, wv, act, tot: (we[w], 0, nf + ff(f, act, w))),
            pl.BlockSpec((1, tf, d), lambda w, f, we, ws, wnb, wv, act, tot: (we[w], ff(f, act, w), 0)),
            pl.BlockSpec((1, 1, tf), lambda w, f, we, ws, wnb, wv, act, tot: (we[w], 0, ff(f, act, w))),
            pl.BlockSpec((1, 1, tf), lambda w, f, we, ws, wnb, wv, act, tot: (we[w], 0, nf + ff(f, act, w))),
            pl.BlockSpec((1, 1, d), lambda w, f, we, ws, wnb, wv, act, tot: (we[w], 0, 0)),
        ],
        out_specs=pl.BlockSpec(memory_space=pl.ANY),
        scratch_shapes=[
            pltpu.VMEM((2, SUPER_ROWS, d), BF16),
            pltpu.VMEM((2, SUPER_ROWS, d), F32),
            pltpu.VMEM((d, 2 * tf), BF16), pltpu.VMEM((tf, d), BF16),
            pltpu.VMEM((ROW_BLK, d), F32),
            pltpu.SemaphoreType.DMA(()), pltpu.SemaphoreType.DMA((2,)),
        ],
    )
    vmem = (2 * SUPER_ROWS * d * 6 + 2 * 3 * d * tf * 4 + 3 * d * tf * 2 + ROW_BLK * d * 4
            + 6 * ROW_BLK * d * 4 + (6 << 20))
    return pl.pallas_call(
        _expert_kernel,
        grid_spec=grid_spec,
        out_shape=jax.ShapeDtypeStruct((n_slots, d), F32),
        compiler_params=pltpu.CompilerParams(
            dimension_semantics=("arbitrary", "arbitrary"), vmem_limit_bytes=_vmem_limit(vmem)),
        name="experts",
    )(*tables, total, xs, w_gate_up, w_gate_up, w_down, b_gate_up.reshape(N_EXPERTS, 1, 2 * D_FF),
      b_gate_up.reshape(N_EXPERTS, 1, 2 * D_FF), b_down.reshape(N_EXPERTS, 1, d))


def _work_tables(counts, n_items):
    padded = (counts + ROW_BLK - 1) // ROW_BLK * ROW_BLK
    pstart = jnp.cumsum(padded) - padded
    items_e = (padded + SUPER_ROWS - 1) // SUPER_ROWS
    item_end = jnp.cumsum(items_e)
    item_start = item_end - items_e
    w = jnp.arange(n_items, dtype=I32)
    total = item_end[-1]
    active = (w < total).astype(I32)
    wl = jnp.minimum(w, total - 1)
    e = jnp.minimum(jnp.searchsorted(item_end, wl, side='right'), N_EXPERTS - 1).astype(I32)
    local = wl - item_start[e]
    start = pstart[e] + local * SUPER_ROWS
    rows = jnp.minimum(SUPER_ROWS, padded[e] - local * SUPER_ROWS)
    valid = jnp.clip(counts[e] - local * SUPER_ROWS, 0, rows)
    nb = (rows // ROW_BLK) * active
    return (e.astype(I32), start.astype(I32), nb.astype(I32), valid.astype(I32), active)


COMBINE_TILE = 256


def _combine_kernel(dest_ref, dest_next_ref, wts_ref, h_ref, g_ref, ys_hbm, out_ref, buf, sem, *, n):
    i = pl.program_id(0)
    slot = i % 2
    tt = h_ref.shape[0]

    def issue(dref, sl):
        def body(t, c):
            for k in range(TOP_K):
                src = ys_hbm.at[pl.ds(dref[0, TOP_K * t + k], 1)]
                pltpu.make_async_copy(src, buf.at[sl, k, pl.ds(t, 1)], sem.at[sl]).start(priority=k % 2)
            return c

        lax.fori_loop(0, tt, body, 0, unroll=DMA_UNROLL)

    def issue_unrolled(dref, sl):
        for t in range(tt):
            for k in range(TOP_K):
                src = ys_hbm.at[pl.ds(dref[0, TOP_K * t + k], 1)]
                pltpu.make_async_copy(src, buf.at[sl, k, pl.ds(t, 1)], sem.at[sl]).start(priority=k % 2)

    @pl.when(i == 0)
    def _():
        issue(dest_ref, 0)

    def drain(t, c):
        for k in range(TOP_K):
            pltpu.make_async_copy(ys_hbm.at[pl.ds(0, 1)], buf.at[slot, k, pl.ds(0, 1)], sem.at[slot]).wait()
        return c

    lax.fori_loop(0, tt, drain, 0, unroll=DMA_UNROLL)

    def finish():
        wts = wts_ref[...]
        moe = wts[:, 0:1] * buf[slot, 0]
        for k in range(1, TOP_K):
            moe = moe + wts[:, k:k + 1] * buf[slot, k]
        y = h_ref[...] + moe
        ms = jnp.mean(y * y, axis=-1, keepdims=True)
        out_ref[...] = (y * lax.rsqrt(ms + NORM_EPS)) * g_ref[...]

    if n > 1:
        @pl.when(i + 1 < n)
        def _():
            issue_unrolled(dest_next_ref, 1 - slot)
            finish()

        @pl.when(i + 1 >= n)
        def _():
            finish()
    else:
        finish()


def _combine(dest, wts, h, g_final, ys):
    t, d = h.shape
    tt = min(COMBINE_TILE, t)
    assert t % tt == 0
    n = t // tt
    return pl.pallas_call(
        functools.partial(_combine_kernel, n=n),
        grid=(n,),
        in_specs=[
            pl.BlockSpec((1, TOP_K * tt), lambda i: (0, i), memory_space=pltpu.SMEM),
            pl.BlockSpec((1, TOP_K * tt), lambda i: (0, jnp.minimum(i + 1, n - 1)), memory_space=pltpu.SMEM),
            pl.BlockSpec((tt, TOP_K), lambda i: (i, 0)),
            pl.BlockSpec((tt, d), lambda i: (i, 0)),
            pl.BlockSpec((1, d), lambda i: (0, 0)),
            pl.BlockSpec(memory_space=pl.ANY),
        ],
        out_specs=pl.BlockSpec((tt, d), lambda i: (i, 0)),
        out_shape=jax.ShapeDtypeStruct((t, d), F32),
        scratch_shapes=[pltpu.VMEM((2, TOP_K, tt, d), F32), pltpu.SemaphoreType.DMA((2,))],
        compiler_params=pltpu.CompilerParams(dimension_semantics=("arbitrary",)),
        name="combine",
    )(dest, dest, wts, h, g_final.reshape(1, d), ys)


def kernel(x_prompt, x_sample, cache_attn_k, cache_attn_v, state_conv, rel_bias_table, norm_mix_g, w_in, b_gate,
           attn_sinks, conv_w, w_o_attn, w_o_conv, w_out, norm_ffn_g, w_router, b_router, w_gate_up, b_gate_up,
           w_down, b_down, norm_final_g):
    batch, seq, d = x_prompt.shape
    dbatch, dseq, _ = x_sample.shape
    assert norm_mix_g.shape[0] == 1 and d == D_MODEL
    t_p, t_s = batch * seq, dbatch * dseq
    xp = x_prompt.reshape(t_p, d)
    xs = x_sample.reshape(t_s, d)

    woa = w_o_attn[0].astype(BF16)
    woc = w_o_conv[0].astype(BF16)
    wout = w_out[0].astype(BF16)
    wr_t = w_router[0].T.astype(BF16)

    w_in_bf = w_in[0].astype(BF16)
    zz_p = _in_proj(xp, norm_mix_g[0], w_in_bf, b_gate[0], tm=min(2048, t_p))
    zz_s = _in_proj(xs, norm_mix_g[0], w_in_bf, b_gate[0], tm=t_s)

    o_p = _attn_prompt(zz_p, batch, seq, rel_bias_table, attn_sinks[0])
    k_new = zz_s[:, ZC_K:ZC_K + KV_DIM].reshape(dbatch, dseq, KV_DIM)
    v_new = zz_s[:, ZC_V:ZC_V + KV_DIM].reshape(dbatch, dseq, KV_DIM)
    k_ext = jnp.concatenate([cache_attn_k[0].reshape(dbatch, WINDOW, KV_DIM), k_new], axis=1)
    v_ext = jnp.concatenate([cache_attn_v[0].reshape(dbatch, WINDOW, KV_DIM), v_new], axis=1)
    o_s = _attn_sample(zz_s, k_ext, v_ext, dbatch, dseq, rel_bias_table, attn_sinks[0])

    hist0 = jnp.repeat(state_conv[0][:, 0, :], dseq, axis=0)
    hist1 = jnp.repeat(state_conv[0][:, 1, :], dseq, axis=0)
    merge_args = (conv_w[0], woa, woc, wout, norm_ffn_g[0], wr_t, b_router[0])
    t_all = t_p + t_s
    h_p, hn, idx_p, wts_p = _merge(o_p, zz_p, xp, None, None, *merge_args, tm=256, seq_rows=None, seq_len=seq,
                                   hn_rows=t_all, hn_row0=0, hn_prev=None)
    h_s, hn, idx_s, wts_s = _merge(o_s, zz_s, xs, hist0, hist1, *merge_args, tm=t_s, seq_rows=dseq, seq_len=dseq,
                                   hn_rows=t_all, hn_row0=t_p, hn_prev=hn)

    idx_t = jnp.concatenate([idx_p, idx_s], axis=1)
    wts = jnp.concatenate([wts_p, wts_s], axis=1).T
    n_assign = t_all * TOP_K
    n_slots = (n_assign + N_EXPERTS * (ROW_BLK - 1) + ROW_BLK - 1) // ROW_BLK * ROW_BLK
    n_items = N_EXPERTS + (n_assign - N_EXPERTS) // SUPER_ROWS

    dest, cnt = _route(idx_t)
    tables = _work_tables(cnt[:, 0], n_items)
    dest = dest.T.reshape(1, -1)
    xs_sorted = _dispatch(dest, hn, n_slots)
    ys = _experts(xs_sorted, tables, w_gate_up[0], b_gate_up[0], w_down[0], b_down[0])
    y_p = _combine(dest[:, :TOP_K * t_p], wts[:t_p], h_p, norm_final_g, ys)
    y_s = _combine(dest[:, TOP_K * t_p:], wts[t_p:], h_s, norm_final_g, ys)

    zz_p3 = zz_p.reshape(batch, seq, Z_DIM)

    def kv_tail(c0):
        return zz_p3[:, seq - WINDOW:, c0:c0 + KV_DIM].reshape(batch, WINDOW, N_KV_HEADS, HEAD_DIM)[None]

    u_s = zz_s[:, ZC_U:ZC_U + CONV_DIM].reshape(dbatch, dseq, CONV_DIM)
    conv_hist = state_conv.shape[2]
    new_conv_p = zz_p3[:, seq - conv_hist:, ZC_U:ZC_U + CONV_DIM][None]
    new_conv_s = jnp.concatenate([state_conv[0], u_s], axis=1)[:, -conv_hist:][None]
    new_k_s = k_ext[:, -WINDOW:].reshape(dbatch, WINDOW, N_KV_HEADS, HEAD_DIM)[None]
    new_v_s = v_ext[:, -WINDOW:].reshape(dbatch, WINDOW, N_KV_HEADS, HEAD_DIM)[None]
    return (y_p.reshape(batch, seq, d), y_s.reshape(dbatch, dseq, d),
            kv_tail(ZC_K), kv_tail(ZC_V), new_conv_p, new_k_s, new_v_s, new_conv_s)
```

```python
import functools
import math

import numpy as np
import jax
import jax.numpy as jnp
from jax import lax
from jax.experimental import pallas as pl
from jax.experimental.pallas import tpu as pltpu

F32 = jnp.float32
BF16 = jnp.bfloat16
I32 = jnp.int32

D_MODEL = 2048
CHUNK = 64
HEAD_DIM = 64
N_Q_HEADS = 16
N_KV_HEADS = 4
GROUP = N_Q_HEADS // N_KV_HEADS
ATTN_DIM = N_Q_HEADS * HEAD_DIM
KV_DIM = N_KV_HEADS * HEAD_DIM
WINDOW = 128
CONV_DIM = D_MODEL // 2
N_BUCKETS = 32
MAX_DISTANCE = 128
N_EXPERTS = 32
TOP_K = 4
D_FF = D_MODEL
SWIGLU_LIMIT = 7.0
SWIGLU_ALPHA = 1.702
NORM_EPS = 1e-5
IN_DIM = ATTN_DIM + 2 * KV_DIM + 3 * CONV_DIM + 2 * D_MODEL

V7X_VMEM_BYTES = 64 * 1024 * 1024
LANES = 128

ZC_GA = 0
ZC_GC = D_MODEL
ZC_CB = 2 * D_MODEL
ZC_U = ZC_CB + CONV_DIM
ZC_Q = ZC_U + CONV_DIM
ZC_K = ZC_Q + ATTN_DIM
ZC_V = ZC_K + KV_DIM
Z_DIM = ZC_V + KV_DIM
PROJ_TN = 512

ATTN_Q_ROWS = 128

ROW_BLK = 256
SUPER_BLKS = 5
SUPER_ROWS = SUPER_BLKS * ROW_BLK
FF_TILE = 256
N_FF_TILES = D_FF // FF_TILE
assert SUPER_BLKS <= N_FF_TILES


def _vmem_limit(nbytes):
    return int(min(nbytes, V7X_VMEM_BYTES - 6 * 1024 * 1024))


_MODE_PLAIN, _MODE_STASH0, _MODE_STASH1, _MODE_MUL0, _MODE_MUL1, _MODE_GATE = range(6)


def _proj_schedule():
    t = PROJ_TN
    cc0 = (ATTN_DIM + 2 * KV_DIM + CONV_DIM) // t
    ch0 = (ATTN_DIM + 2 * KV_DIM + 2 * CONV_DIM) // t
    cb0 = (ATTN_DIM + 2 * KV_DIM) // t
    ga0 = (ATTN_DIM + 2 * KV_DIM + 3 * CONV_DIM) // t
    steps = [
        (cc0, ZC_U // t, 0, _MODE_STASH0),
        (cc0 + 1, ZC_U // t, 0, _MODE_STASH1),
        (ch0, ZC_U // t, 0, _MODE_MUL0),
        (ch0 + 1, ZC_U // t + 1, 0, _MODE_MUL1),
        (0, ZC_Q // t, 0, _MODE_PLAIN),
        (1, ZC_Q // t + 1, 0, _MODE_PLAIN),
        (2, ZC_K // t, 0, _MODE_PLAIN),
        (cb0, ZC_CB // t, 0, _MODE_PLAIN),
        (cb0 + 1, ZC_CB // t + 1, 0, _MODE_PLAIN),
    ]
    for j in range(2 * D_MODEL // t):
        steps.append((ga0 + j, j, j, _MODE_GATE))
    return np.asarray(steps, dtype=np.int32)


def _in_proj_kernel(wt_ref, ot_ref, bt_ref, md_ref, x_hbm, g_ref, w_ref, b_ref, o_ref, x_buf, xn_ref, stash_ref, sem,
                    *, n_tiles):
    del wt_ref, ot_ref, bt_ref
    m = pl.program_id(0)
    n = pl.program_id(1)
    tm = x_buf.shape[0]

    def x_copy(tile):
        return pltpu.make_async_copy(x_hbm.at[pl.ds(pl.multiple_of(tile * tm, 8), tm)], x_buf, sem)

    def product():
        return jnp.dot(xn_ref[...], w_ref[...], preferred_element_type=F32)

    md = md_ref[n]

    @pl.when(md == _MODE_PLAIN)
    def _():
        o_ref[...] = product()

    @pl.when(md == _MODE_STASH0)
    def _():
        @pl.when(m == 0)
        def _():
            x_copy(0).start()

        x_copy(m).wait()
        x = x_buf[...]
        ms = jnp.mean(x * x, axis=-1, keepdims=True)
        xn_ref[...] = ((x * lax.rsqrt(ms + NORM_EPS)) * g_ref[...]).astype(BF16)
        stash_ref[0] = product()

    @pl.when(md == _MODE_STASH1)
    def _():
        if n_tiles > 1:
            @pl.when(m + 1 < n_tiles)
            def _():
                x_copy(m + 1).start()

        stash_ref[1] = product()

    @pl.when(md == _MODE_MUL0)
    def _():
        o_ref[...] = stash_ref[0] * product()

    @pl.when(md == _MODE_MUL1)
    def _():
        o_ref[...] = stash_ref[1] * product()

    @pl.when(md == _MODE_GATE)
    def _():
        o_ref[...] = jax.nn.sigmoid(product() + b_ref[...])


def _in_proj(x, g_mix, w_in, b_gate, tm):
    t, d = x.shape
    sched = _proj_schedule()
    n_steps = sched.shape[0]
    n_tiles = t // tm
    grid_spec = pltpu.PrefetchScalarGridSpec(
        num_scalar_prefetch=4,
        grid=(n_tiles, n_steps),
        in_specs=[
            pl.BlockSpec(memory_space=pl.ANY),
            pl.BlockSpec((1, d), lambda m, n, wt, ot, bt, md: (0, 0)),
            pl.BlockSpec((d, PROJ_TN), lambda m, n, wt, ot, bt, md: (0, wt[n])),
            pl.BlockSpec((1, PROJ_TN), lambda m, n, wt, ot, bt, md: (0, bt[n])),
        ],
        out_specs=pl.BlockSpec((tm, PROJ_TN), lambda m, n, wt, ot, bt, md: (m, ot[n])),
        scratch_shapes=[pltpu.VMEM((tm, d), F32), pltpu.VMEM((tm, d), BF16), pltpu.VMEM((2, tm, PROJ_TN), F32),
                        pltpu.SemaphoreType.DMA(())],
    )
    vmem = tm * d * 4 + tm * d * 2 + 2 * d * PROJ_TN * 2 + 6 * tm * PROJ_TN * 4 + (8 << 20)
    return pl.pallas_call(
        functools.partial(_in_proj_kernel, n_tiles=n_tiles),
        grid_spec=grid_spec,
        out_shape=jax.ShapeDtypeStruct((t, Z_DIM), F32),
        compiler_params=pltpu.CompilerParams(
            dimension_semantics=("arbitrary", "arbitrary"), vmem_limit_bytes=_vmem_limit(vmem)),
        name="in_proj",
    )(jnp.asarray(sched[:, 0]), jnp.asarray(sched[:, 1]), jnp.asarray(sched[:, 2]), jnp.asarray(sched[:, 3]),
      x, g_mix.reshape(1, d), w_in, b_gate.reshape(1, 2 * D_MODEL))


def _t5_buckets(n_q, n_k):
    half = N_BUCKETS // 2
    max_exact = half // 2
    rel = (np.arange(n_k, dtype=np.int32)[None, :] - WINDOW) - np.arange(n_q, dtype=np.int32)[:, None]
    ret = np.where(rel > 0, half, 0)
    n = np.abs(rel)
    nf = np.maximum(n, 1).astype(np.float32)
    large = max_exact + (np.log(nf / np.float32(max_exact)) / np.float32(math.log(MAX_DISTANCE / max_exact))
                         * np.float32(half - max_exact)).astype(np.int32)
    large = np.minimum(large, half - 1)
    return (ret + np.where(n < max_exact, n, large)).astype(np.int32)


def _build_bias(bias_ref, bucket_ref, table_ref, n_q):
    bucket = bucket_ref[...]
    for kvh in range(N_KV_HEADS):
        for g in range(GROUP):
            h = kvh * GROUP + g
            mat = jnp.zeros(bucket.shape, F32)
            for b in range(N_BUCKETS):
                mat = jnp.where(bucket == b, table_ref[b, h], mat)
            bias_ref[kvh, g * n_q:(g + 1) * n_q, :] = mat


def _attend(q, kwin, vwin, bias_ref, sinks_ref, n_q, invalid):
    outs = []
    grp = lax.broadcasted_iota(I32, (GROUP * n_q, 1), 0) // n_q
    for kvh in range(N_KV_HEADS):
        c0 = kvh * GROUP * HEAD_DIM
        qh = jnp.concatenate([q[:, c0 + g * HEAD_DIM:c0 + (g + 1) * HEAD_DIM] for g in range(GROUP)], axis=0)
        qh = (qh * (HEAD_DIM ** -0.5)).astype(BF16)
        kh = kwin[:, kvh * HEAD_DIM:(kvh + 1) * HEAD_DIM].astype(BF16)
        vh = vwin[:, kvh * HEAD_DIM:(kvh + 1) * HEAD_DIM].astype(BF16)
        s = lax.dot_general(qh, kh, (((1,), (1,)), ((), ())), preferred_element_type=F32) + bias_ref[kvh]
        if invalid is not None:
            s = jnp.where(invalid, -1e30, s)
        sink = jnp.zeros((GROUP * n_q, 1), F32)
        for g in range(GROUP):
            sink = jnp.where(grp == g, sinks_ref[kvh * GROUP + g], sink)
        m = jnp.maximum(jnp.max(s, axis=-1, keepdims=True), sink)
        p = jnp.exp(s - m)
        denom = jnp.sum(p, axis=-1, keepdims=True) + jnp.exp(sink - m)
        o = jnp.dot(p.astype(BF16), vh, preferred_element_type=F32) / denom
        outs.append(o)
    return outs


def _store_heads(o_ref, r0, n_q, outs):
    for kvh, o in enumerate(outs):
        for g in range(GROUP):
            c = (kvh * GROUP + g) * HEAD_DIM
            o_ref[r0:r0 + n_q, c:c + HEAD_DIM] = o[g * n_q:(g + 1) * n_q].astype(o_ref.dtype)


def _attn_prompt_kernel(table_ref, sinks_ref, bucket_ref, q_ref, kp_ref, kc_ref, vp_ref, vc_ref, o_ref, bias_ref):
    first = (pl.program_id(0) == 0) & (pl.program_id(1) == 0)

    @pl.when(first)
    def _():
        _build_bias(bias_ref, bucket_ref, table_ref, CHUNK)

    kwin = jnp.concatenate([kp_ref[...], kc_ref[...]], axis=0)
    vwin = jnp.concatenate([vp_ref[...], vc_ref[...]], axis=0)
    n_k = WINDOW + CHUNK
    seq_start = pl.program_id(1) == 0
    col = lax.broadcasted_iota(I32, (1, n_k), 1)
    for c in range(q_ref.shape[0] // CHUNK):
        r0 = c * CHUNK
        invalid = seq_start & (col + r0 < WINDOW)
        outs = _attend(q_ref[r0:r0 + CHUNK, :], kwin[r0:r0 + n_k], vwin[r0:r0 + n_k], bias_ref, sinks_ref,
                       CHUNK, invalid)
        _store_heads(o_ref, r0, CHUNK, outs)


def _attn_prompt(zz, batch, seq, table, sinks):
    qb = ATTN_Q_ROWS
    assert seq % qb == 0 and qb % WINDOW == 0
    nb = seq // qb
    wpq = qb // WINDOW
    n_k = WINDOW + CHUNK
    bucket = jnp.asarray(_t5_buckets(CHUNK, n_k))
    smem = pl.BlockSpec(memory_space=pltpu.SMEM)
    kblk, vblk = ZC_K // KV_DIM, ZC_V // KV_DIM
    return pl.pallas_call(
        _attn_prompt_kernel,
        grid=(batch, nb),
        in_specs=[
            smem, smem,
            pl.BlockSpec((CHUNK, n_k), lambda b, j: (0, 0)),
            pl.BlockSpec((qb, ATTN_DIM), lambda b, j: (b * nb + j, ZC_Q // ATTN_DIM)),
            pl.BlockSpec((WINDOW, KV_DIM), lambda b, j: ((b * nb + j) * wpq - jnp.minimum(j, 1), kblk)),
            pl.BlockSpec((qb, KV_DIM), lambda b, j: (b * nb + j, kblk)),
            pl.BlockSpec((WINDOW, KV_DIM), lambda b, j: ((b * nb + j) * wpq - jnp.minimum(j, 1), vblk)),
            pl.BlockSpec((qb, KV_DIM), lambda b, j: (b * nb + j, vblk)),
        ],
        out_specs=pl.BlockSpec((qb, ATTN_DIM), lambda b, j: (b * nb + j, 0)),
        out_shape=jax.ShapeDtypeStruct((batch * seq, ATTN_DIM), BF16),
        scratch_shapes=[pltpu.VMEM((N_KV_HEADS, GROUP * CHUNK, n_k), F32)],
        compiler_params=pltpu.CompilerParams(dimension_semantics=("arbitrary", "arbitrary")),
        name="attn_prompt",
    )(table, sinks, bucket, zz, zz, zz, zz, zz)


def _attn_sample_kernel(table_ref, sinks_ref, bucket_ref, q_ref, k_ref, v_ref, o_ref, bias_ref, *, n_q):
    @pl.when(pl.program_id(0) == 0)
    def _():
        _build_bias(bias_ref, bucket_ref, table_ref, n_q)

    outs = _attend(q_ref[...], k_ref[0], v_ref[0], bias_ref, sinks_ref, n_q, None)
    _store_heads(o_ref, 0, n_q, outs)


def _attn_sample(zz, k_ext, v_ext, batch, n_q, table, sinks):
    n_k = WINDOW + n_q
    bucket = jnp.asarray(_t5_buckets(n_q, n_k))
    smem = pl.BlockSpec(memory_space=pltpu.SMEM)
    return pl.pallas_call(
        functools.partial(_attn_sample_kernel, n_q=n_q),
        grid=(batch,),
        in_specs=[
            smem, smem,
            pl.BlockSpec((n_q, n_k), lambda b: (0, 0)),
            pl.BlockSpec((n_q, ATTN_DIM), lambda b: (b, ZC_Q // ATTN_DIM)),
            pl.BlockSpec((1, n_k, KV_DIM), lambda b: (b, 0, 0)),
            pl.BlockSpec((1, n_k, KV_DIM), lambda b: (b, 0, 0)),
        ],
        out_specs=pl.BlockSpec((n_q, ATTN_DIM), lambda b: (b, 0)),
        out_shape=jax.ShapeDtypeStruct((batch * n_q, ATTN_DIM), BF16),
        scratch_shapes=[pltpu.VMEM((N_KV_HEADS, GROUP * n_q, n_k), F32)],
        compiler_params=pltpu.CompilerParams(dimension_semantics=("arbitrary",)),
        name="attn_sample",
    )(table, sinks, bucket, zz, k_ext, v_ext)


_MERGE_INPUTS = 15


def _merge_kernel(*refs, tm, seq_rows, tiles_per_seq):
    (o_ref, ga_ref, gc_ref, cb_ref, u_ref, h0_ref, h1_ref, x_ref, cw_ref, woa_ref, woc_ref, wout_ref,
     gf_ref, wr_ref, br_ref) = refs[:_MERGE_INPUTS]
    h_ref, hn_ref, idx_ref, wts_ref = refs[-4:]
    u = u_ref[...]
    row = lax.broadcasted_iota(I32, (tm, 1), 0)
    if seq_rows is None:
        at_start = (pl.program_id(0) % tiles_per_seq) == 0
        hist0 = jnp.where(at_start, 0.0, h0_ref[6:7, :])
        hist1 = jnp.where(at_start, 0.0, h1_ref[7:8, :])
        pos = row
    else:
        hist0 = h0_ref[...]
        hist1 = h1_ref[...]
        pos = row % seq_rows
    r1 = pltpu.roll(u, 1, 0)
    r2 = pltpu.roll(u, 2, 0)
    u1 = jnp.where(pos == 0, hist1, r1)
    u2 = jnp.where(pos == 0, hist0, jnp.where(pos == 1, hist1, r2))
    cw = cw_ref[...]
    y = cw[0:1, :] * u2
    y = y + cw[1:2, :] * u1
    y = y + cw[2:3, :] * u
    c = (cb_ref[...] * y).astype(BF16)
    t = ga_ref[...] * jnp.dot(o_ref[...], woa_ref[...], preferred_element_type=F32)
    t = t + gc_ref[...] * jnp.dot(c, woc_ref[...], preferred_element_type=F32)
    h = x_ref[...] + jnp.dot(t.astype(BF16), wout_ref[...], preferred_element_type=F32)
    h_ref[...] = h
    ms = jnp.mean(h * h, axis=-1, keepdims=True)
    hn = (h * lax.rsqrt(ms + NORM_EPS)) * gf_ref[...]
    hn_ref[...] = hn
    logits = lax.dot_general(wr_ref[...], hn.astype(BF16), (((1,), (1,)), ((), ())),
                             preferred_element_type=F32) + br_ref[...]
    erow = lax.broadcasted_iota(I32, logits.shape, 0).astype(F32)
    vals, idxs = [], []
    for _ in range(TOP_K):
        mx = jnp.max(logits, axis=0, keepdims=True)
        ix = jnp.min(jnp.where(logits == mx, erow, float(N_EXPERTS)), axis=0, keepdims=True)
        vals.append(mx)
        idxs.append(ix)
        logits = jnp.where(erow == ix, -jnp.inf, logits)
    es = [jnp.exp(v - vals[0]) for v in vals]
    tot = es[0]
    for e in es[1:]:
        tot = tot + e
    idx_ref[...] = jnp.concatenate(idxs, axis=0).astype(I32)
    wts_ref[...] = jnp.concatenate([e / tot for e in es], axis=0)


def _merge(o, zz, x, hist0, hist1, conv_w, woa, woc, wout, g_ffn, wr_t, br, tm, seq_rows, seq_len,
           hn_rows, hn_row0, hn_prev):
    t = x.shape[0]
    d = D_MODEL
    assert hn_row0 % tm == 0
    extra_specs, extra_args, aliases = [], [], {}
    if hn_prev is not None:
        extra_specs, extra_args, aliases = [pl.BlockSpec(memory_space=pl.ANY)], [hn_prev], {_MERGE_INPUTS: 1}
    full = lambda shape: pl.BlockSpec(shape, lambda i: (0,) * len(shape))
    if seq_rows is None:
        tiles_per_seq = seq_len // tm
        hist_spec = pl.BlockSpec((8, CONV_DIM), lambda i: (jnp.maximum(i * (tm // 8) - 1, 0), ZC_U // CONV_DIM))
        h0_spec = h1_spec = hist_spec
        h0_arg = h1_arg = zz
    else:
        tiles_per_seq = None
        h0_spec = h1_spec = pl.BlockSpec((tm, CONV_DIM), lambda i: (i, 0))
        h0_arg, h1_arg = hist0, hist1
    kern = functools.partial(_merge_kernel, tm=tm, seq_rows=seq_rows, tiles_per_seq=tiles_per_seq)
    vmem = (2 * tm * (ATTN_DIM * 2 + 5 * d * 4 + 4 * CONV_DIM * 4) + 2 * 2 * (2 * ATTN_DIM * d + d * d)
            + 10 * tm * d * 4 + (8 << 20))
    return pl.pallas_call(
        kern,
        grid=(t // tm,),
        in_specs=[
            pl.BlockSpec((tm, ATTN_DIM), lambda i: (i, 0)),
            pl.BlockSpec((tm, d), lambda i: (i, ZC_GA // d)),
            pl.BlockSpec((tm, d), lambda i: (i, ZC_GC // d)),
            pl.BlockSpec((tm, CONV_DIM), lambda i: (i, ZC_CB // CONV_DIM)),
            pl.BlockSpec((tm, CONV_DIM), lambda i: (i, ZC_U // CONV_DIM)),
            h0_spec, h1_spec,
            pl.BlockSpec((tm, d), lambda i: (i, 0)),
            full((3, CONV_DIM)),
            full((ATTN_DIM, d)), full((CONV_DIM, d)), full((d, d)),
            full((1, d)), full((N_EXPERTS, d)), full((N_EXPERTS, 1)),
        ] + extra_specs,
        out_specs=[
            pl.BlockSpec((tm, d), lambda i: (i, 0)),
            pl.BlockSpec((tm, d), lambda i: (hn_row0 // tm + i, 0)),
            pl.BlockSpec((TOP_K, tm), lambda i: (0, i)),
            pl.BlockSpec((TOP_K, tm), lambda i: (0, i)),
        ],
        out_shape=[
            jax.ShapeDtypeStruct((t, d), F32),
            jax.ShapeDtypeStruct((hn_rows, d), F32),
            jax.ShapeDtypeStruct((TOP_K, t), I32),
            jax.ShapeDtypeStruct((TOP_K, t), F32),
        ],
        input_output_aliases=aliases,
        compiler_params=pltpu.CompilerParams(dimension_semantics=("arbitrary",), vmem_limit_bytes=_vmem_limit(vmem)),
        name="merge_prompt" if seq_rows is None else "merge_sample",
    )(o, zz, zz, zz, zz, h0_arg, h1_arg, x, conv_w, woa, woc, wout, g_ffn.reshape(1, d), wr_t,
      br.reshape(N_EXPERTS, 1), *extra_args)


ROUTE_TILE = 32768
ROUTE_SUB = 384


def _route_kernel(idx_ref, dest_ref, cnt_ref, cnt_acc, carry, pst, tri_ref):
    p = pl.program_id(0)
    j = pl.program_id(1)
    sub = tri_ref.shape[0]
    n_sub = idx_ref.shape[1] // sub
    erow = lax.broadcasted_iota(I32, (N_EXPERTS, sub), 0)

    def membership(s):
        idx = idx_ref[:, pl.ds(pl.multiple_of(s * sub, LANES), sub)]
        masks = [(erow == idx[k:k + 1, :]).astype(F32) for k in range(TOP_K)]
        mtot = masks[0]
        for m in masks[1:]:
            mtot = mtot + m
        return masks, mtot, jnp.sum(mtot, axis=1, keepdims=True)

    @pl.when((p == 0) & (j == 0))
    def _():
        cnt_acc[...] = jnp.zeros_like(cnt_acc)

    @pl.when(p == 0)
    def _():
        def body(s, c):
            _, _, piece_cnt = membership(s)
            cnt_acc[...] += jnp.broadcast_to(piece_cnt, cnt_acc.shape)
            return c

        lax.fori_loop(0, n_sub, body, 0)

    @pl.when((p == 1) & (j == 0))
    def _():
        cnt = cnt_acc[:, 0:1]
        padded = jnp.floor((cnt + float(ROW_BLK - 1)) / float(ROW_BLK)) * float(ROW_BLK)
        r = lax.broadcasted_iota(I32, (N_EXPERTS, N_EXPERTS), 0)
        c = lax.broadcasted_iota(I32, (N_EXPERTS, N_EXPERTS), 1)
        rowv = jnp.sum(jnp.where(r == c, padded, 0.0), axis=0, keepdims=True)
        start = jnp.sum(jnp.where(c < r, rowv, 0.0), axis=1, keepdims=True)
        pst[...] = jnp.broadcast_to(start, pst.shape)
        carry[...] = jnp.zeros_like(carry)
        a = lax.broadcasted_iota(I32, (sub, sub), 0)
        b = lax.broadcasted_iota(I32, (sub, sub), 1)
        tri_ref[...] = (a < b).astype(BF16)

    @pl.when(p == 1)
    def _():
        def body(s, c):
            masks, mtot, piece_cnt = membership(s)
            excl = jnp.dot(mtot.astype(BF16), tri_ref[...], preferred_element_type=F32)
            val = pst[:, 0:1] + carry[:, 0:1] + excl
            dest = jnp.concatenate([jnp.sum(m * val, axis=0, keepdims=True) for m in masks], axis=0)
            dest_ref[:, pl.ds(pl.multiple_of(s * sub, LANES), sub)] = dest.astype(I32)
            carry[...] += jnp.broadcast_to(piece_cnt, carry.shape)
            return c

        lax.fori_loop(0, n_sub, body, 0)
        cnt_ref[...] = cnt_acc[...].astype(I32)


def _route(idx_t):
    t = idx_t.shape[1]
    assert t % ROUTE_SUB == 0
    tr = max(c for c in range(ROUTE_SUB, min(ROUTE_TILE, t) + 1, ROUTE_SUB) if t % c == 0)
    return pl.pallas_call(
        _route_kernel,
        grid=(2, t // tr),
        in_specs=[pl.BlockSpec((TOP_K, tr), lambda p, j: (0, j))],
        out_specs=[
            pl.BlockSpec((TOP_K, tr), lambda p, j: (0, j * p)),
            pl.BlockSpec((N_EXPERTS, LANES), lambda p, j: (0, 0)),
        ],
        out_shape=[jax.ShapeDtypeStruct((TOP_K, t), I32), jax.ShapeDtypeStruct((N_EXPERTS, LANES), I32)],
        scratch_shapes=[pltpu.VMEM((N_EXPERTS, LANES), F32), pltpu.VMEM((N_EXPERTS, LANES), F32),
                        pltpu.VMEM((N_EXPERTS, LANES), F32), pltpu.VMEM((ROUTE_SUB, ROUTE_SUB), BF16)],
        compiler_params=pltpu.CompilerParams(dimension_semantics=("arbitrary", "arbitrary")),
        name="route",
    )(idx_t)


DISPATCH_TILE = 1376
DMA_UNROLL = 8


def _dispatch_kernel(dest_ref, hn_ref, xs_hbm, sem):
    tt = hn_ref.shape[0]

    def issue(t, c):
        for k in range(TOP_K):
            dst = xs_hbm.at[pl.ds(dest_ref[0, TOP_K * t + k], 1)]
            pltpu.make_async_copy(hn_ref.at[pl.ds(t, 1)], dst, sem).start(priority=k % 2)
        return c

    lax.fori_loop(0, tt, issue, 0, unroll=DMA_UNROLL)

    def drain(t, c):
        for k in range(TOP_K):
            pltpu.make_async_copy(hn_ref.at[pl.ds(0, 1)], xs_hbm.at[pl.ds(0, 1)], sem).wait()
        return c

    lax.fori_loop(0, tt, drain, 0, unroll=DMA_UNROLL)


def _dispatch(dest, hn, n_slots):
    t, d = hn.shape
    tt = max(c for c in range(32, DISPATCH_TILE + 1, 32) if t % c == 0)
    return pl.pallas_call(
        _dispatch_kernel,
        grid=(t // tt,),
        in_specs=[pl.BlockSpec((1, TOP_K * tt), lambda i: (0, i), memory_space=pltpu.SMEM),
                  pl.BlockSpec((tt, d), lambda i: (i, 0))],
        out_specs=pl.BlockSpec(memory_space=pl.ANY),
        out_shape=jax.ShapeDtypeStruct((n_slots, d), hn.dtype),
        scratch_shapes=[pltpu.SemaphoreType.DMA(())],
        compiler_params=pltpu.CompilerParams(dimension_semantics=("arbitrary",)),
        name="dispatch",
    )(dest, hn)


MAX_UNIT_BLKS = 5


def _matmul_units(n_blocks):
    units, b = [], 0
    while b < n_blocks:
        size = min(MAX_UNIT_BLKS, n_blocks - b)
        units.append((b, size))
        b += size
    return units


def _expert_kernel(we_ref, ws_ref, wnb_ref, wvalid_ref, wact_ref, wtot_ref,
                   xs_hbm, wg_ref, wu_ref, wd_ref, bias_ref, ys_hbm,
                   x_sb, acc, wgu_bf, wd_bf, stage, sem_in, sem_out):
    del we_ref, wact_ref
    w = pl.program_id(0)
    f = pl.program_id(1)
    n_items = wtot_ref[0]
    last_f = N_FF_TILES - 1
    slot = w % 2
    nb = wnb_ref[w]
    row = lax.broadcasted_iota(I32, (ROW_BLK, 1), 0)

    def x_copy(item, j):
        src = xs_hbm.at[pl.ds(pl.multiple_of(ws_ref[item] + j * ROW_BLK, ROW_BLK), ROW_BLK)]
        return pltpu.make_async_copy(src, stage, sem_in)

    def x_cast(item, j, sl):
        r0 = pl.multiple_of(j * ROW_BLK, ROW_BLK)
        x_sb[sl, pl.ds(r0, ROW_BLK), :] = jnp.where(row + r0 < wvalid_ref[item], stage[...], 0.0).astype(BF16)

    def y_copy(item, j, sl):
        r0 = pl.multiple_of(j * ROW_BLK, ROW_BLK)
        dst = ys_hbm.at[pl.ds(pl.multiple_of(ws_ref[item] + r0, ROW_BLK), ROW_BLK)]
        return pltpu.make_async_copy(acc.at[sl, pl.ds(r0, ROW_BLK)], dst, sem_out.at[sl])

    def y_wait_all(item, sl):
        def body(j, c):
            y_copy(item, j, sl).wait()
            return c

        lax.fori_loop(0, wnb_ref[item], body, 0)

    @pl.when((w == 0) & (f == 0))
    def _first_rows():
        def body(j, c):
            cp = x_copy(0, j)
            cp.start()
            cp.wait()
            x_cast(0, j, 0)
            return c

        lax.fori_loop(0, wnb_ref[0], body, 0)

    nxt = jnp.minimum(w + 1, n_items - 1)
    prefetch = (w + 1 < n_items) & (f < wnb_ref[nxt])

    @pl.when(prefetch)
    def _():
        x_copy(nxt, f).start()

    @pl.when((f == 0) & (w >= 2))
    def _reclaim_acc():
        y_wait_all(jnp.maximum(w - 2, 0), slot)

    @pl.when(nb > 0)
    def _compute():
        bias = bias_ref[0, 0]
        bg = bias[:, :FF_TILE]
        bu = bias[:, FF_TILE:2 * FF_TILE]
        bd = bias[:, 2 * FF_TILE:]

        def cast_weights():
            wgu_bf[:, :FF_TILE] = wg_ref[0].astype(BF16)
            wgu_bf[:, FF_TILE:] = wu_ref[0].astype(BF16)
            wd_bf[...] = wd_ref[0].astype(BF16)

        def update(r0, rows, first):
            x = x_sb[slot, r0:r0 + rows, :]
            gu = jnp.dot(x, wgu_bf[...], preferred_element_type=F32)
            g = jnp.minimum(gu[:, :FF_TILE] + bg, SWIGLU_LIMIT)
            u = jnp.clip(gu[:, FF_TILE:] + bu, -SWIGLU_LIMIT, SWIGLU_LIMIT)
            hh = (u + 1.0) * (g * jax.nn.sigmoid(SWIGLU_ALPHA * g))
            part = jnp.dot(hh.astype(BF16), wd_bf[...], preferred_element_type=F32)
            if first:
                acc[slot, r0:r0 + rows, :] = part + bd
            else:
                acc[slot, r0:r0 + rows, :] += part

        def sweep(first):
            for n in range(1, SUPER_BLKS + 1):
                @pl.when(nb == n)
                def _(n=n):
                    cast_weights()
                    for b0, size in _matmul_units(n):
                        update(b0 * ROW_BLK, size * ROW_BLK, first)

        @pl.when(f == 0)
        def _():
            sweep(True)

        @pl.when(f > 0)
        def _():
            sweep(False)

    @pl.when((f == last_f) & (nb > 0))
    def _store():
        def body(j, c):
            y_copy(w, j, slot).start()
            return c

        lax.fori_loop(0, nb, body, 0)

    @pl.when(prefetch)
    def _():
        x_copy(nxt, f).wait()
        x_cast(nxt, f, 1 - slot)

    @pl.when((w == n_items - 1) & (f == last_f))
    def _drain():
        y_wait_all(jnp.maximum(w - 1, 0), 1 - slot)
        y_wait_all(w, slot)


def _experts(xs, tables, w_gate_up, b_gate_up, w_down, b_down):
    n_slots, d = xs.shape
    total = jnp.sum(tables[4]).astype(I32).reshape(1)
    tf = FF_TILE
    nf = N_FF_TILES
    bgu = b_gate_up.reshape(N_EXPERTS, 2, nf, tf).transpose(0, 2, 1, 3).reshape(N_EXPERTS, nf, 2 * tf)
    bdn = jnp.broadcast_to(b_down[:, None, :], (N_EXPERTS, nf, d))
    biases = jnp.concatenate([bgu, bdn], axis=-1).reshape(N_EXPERTS, nf, 1, 2 * tf + d)

    def ff(f, act, w):
        return f * act[w] + (nf - 1) * (1 - act[w])

    grid_spec = pltpu.PrefetchScalarGridSpec(
        num_scalar_prefetch=6,
        grid=(total[0], nf),
        in_specs=[
            pl.BlockSpec(memory_space=pl.ANY),
            pl.BlockSpec((1, d, tf), lambda w, f, we, ws, wnb, wv, act, tot: (we[w], 0, ff(f, act, w))),
            pl.BlockSpec((1, d, tf), lambda w, f, we, ws, wnb, wv, act, tot: (we[w], 0, nf + ff(f, act, w))),
            pl.BlockSpec((1, tf, d), lambda w, f, we, ws, wnb, wv, act, tot: (we[w], ff(f, act, w), 0)),
            pl.BlockSpec((1, 1, 1, 2 * tf + d),
                         lambda w, f, we, ws, wnb, wv, act, tot: (we[w], ff(f, act, w), 0, 0)),
        ],
        out_specs=pl.BlockSpec(memory_space=pl.ANY),
        scratch_shapes=[
            pltpu.VMEM((2, SUPER_ROWS, d), BF16),
            pltpu.VMEM((2, SUPER_ROWS, d), F32),
            pltpu.VMEM((d, 2 * tf), BF16), pltpu.VMEM((tf, d), BF16),
            pltpu.VMEM((ROW_BLK, d), F32),
            pltpu.SemaphoreType.DMA(()), pltpu.SemaphoreType.DMA((2,)),
        ],
    )
    vmem = (2 * SUPER_ROWS * d * 6 + 2 * 3 * d * tf * 4 + 3 * d * tf * 2 + ROW_BLK * d * 4
            + 6 * ROW_BLK * d * 4 + (6 << 20))
    return pl.pallas_call(
        _expert_kernel,
        grid_spec=grid_spec,
        out_shape=jax.ShapeDtypeStruct((n_slots, d), F32),
        compiler_params=pltpu.CompilerParams(
            dimension_semantics=("arbitrary", "arbitrary"), vmem_limit_bytes=_vmem_limit(vmem)),
        name="experts",
    )(*tables, total, xs, w_gate_up, w_gate_up, w_down, biases)


def _work_tables(counts, n_items):
    padded = (counts + ROW_BLK - 1) // ROW_BLK * ROW_BLK
    pstart = jnp.cumsum(padded) - padded
    items_e = (padded + SUPER_ROWS - 1) // SUPER_ROWS
    item_end = jnp.cumsum(items_e)
    item_start = item_end - items_e
    w = jnp.arange(n_items, dtype=I32)
    total = item_end[-1]
    active = (w < total).astype(I32)
    wl = jnp.minimum(w, total - 1)
    e = jnp.minimum(jnp.searchsorted(item_end, wl, side='right'), N_EXPERTS - 1).astype(I32)
    local = wl - item_start[e]
    start = pstart[e] + local * SUPER_ROWS
    rows = jnp.minimum(SUPER_ROWS, padded[e] - local * SUPER_ROWS)
    valid = jnp.clip(counts[e] - local * SUPER_ROWS, 0, rows)
    nb = (rows // ROW_BLK) * active
    return (e.astype(I32), start.astype(I32), nb.astype(I32), valid.astype(I32), active)


COMBINE_TILE = 256


def _combine_kernel(dest_ref, dest_next_ref, wts_ref, h_ref, g_ref, ys_hbm, out_ref, buf, sem, *, n):
    i = pl.program_id(0)
    slot = i % 2
    tt = h_ref.shape[0]

    def issue(dref, sl):
        def body(t, c):
            for k in range(TOP_K):
                src = ys_hbm.at[pl.ds(dref[0, TOP_K * t + k], 1)]
                pltpu.make_async_copy(src, buf.at[sl, k, pl.ds(t, 1)], sem.at[sl]).start(priority=k % 2)
            return c

        lax.fori_loop(0, tt, body, 0, unroll=DMA_UNROLL)

    def issue_unrolled(dref, sl):
        for t in range(tt):
            for k in range(TOP_K):
                src = ys_hbm.at[pl.ds(dref[0, TOP_K * t + k], 1)]
                pltpu.make_async_copy(src, buf.at[sl, k, pl.ds(t, 1)], sem.at[sl]).start(priority=k % 2)

    @pl.when(i == 0)
    def _():
        issue(dest_ref, 0)

    def drain(t, c):
        for k in range(TOP_K):
            pltpu.make_async_copy(ys_hbm.at[pl.ds(0, 1)], buf.at[slot, k, pl.ds(0, 1)], sem.at[slot]).wait()
        return c

    lax.fori_loop(0, tt, drain, 0, unroll=DMA_UNROLL)

    def finish():
        wts = wts_ref[...]
        moe = wts[:, 0:1] * buf[slot, 0]
        for k in range(1, TOP_K):
            moe = moe + wts[:, k:k + 1] * buf[slot, k]
        y = h_ref[...] + moe
        ms = jnp.mean(y * y, axis=-1, keepdims=True)
        out_ref[...] = (y * lax.rsqrt(ms + NORM_EPS)) * g_ref[...]

    if n > 1:
        @pl.when(i + 1 < n)
        def _():
            issue_unrolled(dest_next_ref, 1 - slot)
            finish()

        @pl.when(i + 1 >= n)
        def _():
            finish()
    else:
        finish()


def _combine(dest, wts, h, g_final, ys):
    t, d = h.shape
    tt = min(COMBINE_TILE, t)
    assert t % tt == 0
    n = t // tt
    return pl.pallas_call(
        functools.partial(_combine_kernel, n=n),
        grid=(n,),
        in_specs=[
            pl.BlockSpec((1, TOP_K * tt), lambda i: (0, i), memory_space=pltpu.SMEM),
            pl.BlockSpec((1, TOP_K * tt), lambda i: (0, jnp.minimum(i + 1, n - 1)), memory_space=pltpu.SMEM),
            pl.BlockSpec((tt, TOP_K), lambda i: (i, 0)),
            pl.BlockSpec((tt, d), lambda i: (i, 0)),
            pl.BlockSpec((1, d), lambda i: (0, 0)),
            pl.BlockSpec(memory_space=pl.ANY),
        ],
        out_specs=pl.BlockSpec((tt, d), lambda i: (i, 0)),
        out_shape=jax.ShapeDtypeStruct((t, d), F32),
        scratch_shapes=[pltpu.VMEM((2, TOP_K, tt, d), F32), pltpu.SemaphoreType.DMA((2,))],
        compiler_params=pltpu.CompilerParams(dimension_semantics=("arbitrary",)),
        name="combine",
    )(dest, dest, wts, h, g_final.reshape(1, d), ys)


def kernel(x_prompt, x_sample, cache_attn_k, cache_attn_v, state_conv, rel_bias_table, norm_mix_g, w_in, b_gate,
           attn_sinks, conv_w, w_o_attn, w_o_conv, w_out, norm_ffn_g, w_router, b_router, w_gate_up, b_gate_up,
           w_down, b_down, norm_final_g):
    batch, seq, d = x_prompt.shape
    dbatch, dseq, _ = x_sample.shape
    assert norm_mix_g.shape[0] == 1 and d == D_MODEL
    t_p, t_s = batch * seq, dbatch * dseq
    xp = x_prompt.reshape(t_p, d)
    xs = x_sample.reshape(t_s, d)

    woa = w_o_attn[0].astype(BF16)
    woc = w_o_conv[0].astype(BF16)
    wout = w_out[0].astype(BF16)
    wr_t = w_router[0].T.astype(BF16)

    w_in_bf = w_in[0].astype(BF16)
    zz_p = _in_proj(xp, norm_mix_g[0], w_in_bf, b_gate[0], tm=min(2048, t_p))
    zz_s = _in_proj(xs, norm_mix_g[0], w_in_bf, b_gate[0], tm=t_s)

    o_p = _attn_prompt(zz_p, batch, seq, rel_bias_table, attn_sinks[0])
    k_new = zz_s[:, ZC_K:ZC_K + KV_DIM].reshape(dbatch, dseq, KV_DIM)
    v_new = zz_s[:, ZC_V:ZC_V + KV_DIM].reshape(dbatch, dseq, KV_DIM)
    k_ext = jnp.concatenate([cache_attn_k[0].reshape(dbatch, WINDOW, KV_DIM), k_new], axis=1)
    v_ext = jnp.concatenate([cache_attn_v[0].reshape(dbatch, WINDOW, KV_DIM), v_new], axis=1)
    o_s = _attn_sample(zz_s, k_ext, v_ext, dbatch, dseq, rel_bias_table, attn_sinks[0])

    hist0 = jnp.repeat(state_conv[0][:, 0, :], dseq, axis=0)
    hist1 = jnp.repeat(state_conv[0][:, 1, :], dseq, axis=0)
    merge_args = (conv_w[0], woa, woc, wout, norm_ffn_g[0], wr_t, b_router[0])
    t_all = t_p + t_s
    h_p, hn, idx_p, wts_p = _merge(o_p, zz_p, xp, None, None, *merge_args, tm=256, seq_rows=None, seq_len=seq,
                                   hn_rows=t_all, hn_row0=0, hn_prev=None)
    h_s, hn, idx_s, wts_s = _merge(o_s, zz_s, xs, hist0, hist1, *merge_args, tm=t_s, seq_rows=dseq, seq_len=dseq,
                                   hn_rows=t_all, hn_row0=t_p, hn_prev=hn)

    idx_t = jnp.concatenate([idx_p, idx_s], axis=1)
    wts = jnp.concatenate([wts_p, wts_s], axis=1).T
    n_assign = t_all * TOP_K
    n_slots = (n_assign + N_EXPERTS * (ROW_BLK - 1) + ROW_BLK - 1) // ROW_BLK * ROW_BLK
    n_items = N_EXPERTS + (n_assign - N_EXPERTS) // SUPER_ROWS

    dest, cnt = _route(idx_t)
    tables = _work_tables(cnt[:, 0], n_items)
    dest = dest.T.reshape(1, -1)
    xs_sorted = _dispatch(dest, hn, n_slots)
    ys = _experts(xs_sorted, tables, w_gate_up[0], b_gate_up[0], w_down[0], b_down[0])
    y_p = _combine(dest[:, :TOP_K * t_p], wts[:t_p], h_p, norm_final_g, ys)
    y_s = _combine(dest[:, TOP_K * t_p:], wts[t_p:], h_s, norm_final_g, ys)

    zz_p3 = zz_p.reshape(batch, seq, Z_DIM)

    def kv_tail(c0):
        return zz_p3[:, seq - WINDOW:, c0:c0 + KV_DIM].reshape(batch, WINDOW, N_KV_HEADS, HEAD_DIM)[None]

    u_s = zz_s[:, ZC_U:ZC_U + CONV_DIM].reshape(dbatch, dseq, CONV_DIM)
    conv_hist = state_conv.shape[2]
    new_conv_p = zz_p3[:, seq - conv_hist:, ZC_U:ZC_U + CONV_DIM][None]
    new_conv_s = jnp.concatenate([state_conv[0], u_s], axis=1)[:, -conv_hist:][None]
    new_k_s = k_ext[:, -WINDOW:].reshape(dbatch, WINDOW, N_KV_HEADS, HEAD_DIM)[None]
    new_v_s = v_ext[:, -WINDOW:].reshape(dbatch, WINDOW, N_KV_HEADS, HEAD_DIM)[None]
    return (y_p.reshape(batch, seq, d), y_s.reshape(dbatch, dseq, d),
            kv_tail(ZC_K), kv_tail(ZC_V), new_conv_p, new_k_s, new_v_s, new_conv_s)
```

```python
import functools
import math

import numpy as np
import jax
import jax.numpy as jnp
from jax import lax
from jax.experimental import pallas as pl
from jax.experimental.pallas import tpu as pltpu

F32 = jnp.float32
BF16 = jnp.bfloat16
I32 = jnp.int32

D_MODEL = 2048
CHUNK = 64
HEAD_DIM = 64
N_Q_HEADS = 16
N_KV_HEADS = 4
GROUP = N_Q_HEADS // N_KV_HEADS
ATTN_DIM = N_Q_HEADS * HEAD_DIM
KV_DIM = N_KV_HEADS * HEAD_DIM
WINDOW = 128
CONV_DIM = D_MODEL // 2
N_BUCKETS = 32
MAX_DISTANCE = 128
N_EXPERTS = 32
TOP_K = 4
D_FF = D_MODEL
SWIGLU_LIMIT = 7.0
SWIGLU_ALPHA = 1.702
NORM_EPS = 1e-5
IN_DIM = ATTN_DIM + 2 * KV_DIM + 3 * CONV_DIM + 2 * D_MODEL

V7X_VMEM_BYTES = 64 * 1024 * 1024
LANES = 128

ZC_GA = 0
ZC_GC = D_MODEL
ZC_CB = 2 * D_MODEL
ZC_U = ZC_CB + CONV_DIM
ZC_Q = ZC_U + CONV_DIM
ZC_K = ZC_Q + ATTN_DIM
ZC_V = ZC_K + KV_DIM
Z_DIM = ZC_V + KV_DIM
PROJ_TN = 512

ATTN_Q_ROWS = 512

ROW_BLK = 256
SUPER_BLKS = 5
SUPER_ROWS = SUPER_BLKS * ROW_BLK
FF_TILE = 256
N_FF_TILES = D_FF // FF_TILE
assert SUPER_BLKS <= N_FF_TILES


def _vmem_limit(nbytes):
    return int(min(nbytes, V7X_VMEM_BYTES - 6 * 1024 * 1024))


_MODE_PLAIN, _MODE_STASH0, _MODE_STASH1, _MODE_MUL0, _MODE_MUL1, _MODE_GATE = range(6)


def _proj_schedule():
    t = PROJ_TN
    cc0 = (ATTN_DIM + 2 * KV_DIM + CONV_DIM) // t
    ch0 = (ATTN_DIM + 2 * KV_DIM + 2 * CONV_DIM) // t
    cb0 = (ATTN_DIM + 2 * KV_DIM) // t
    ga0 = (ATTN_DIM + 2 * KV_DIM + 3 * CONV_DIM) // t
    steps = [
        (cc0, ZC_U // t, 0, _MODE_STASH0),
        (cc0 + 1, ZC_U // t, 0, _MODE_STASH1),
        (ch0, ZC_U // t, 0, _MODE_MUL0),
        (ch0 + 1, ZC_U // t + 1, 0, _MODE_MUL1),
        (0, ZC_Q // t, 0, _MODE_PLAIN),
        (1, ZC_Q // t + 1, 0, _MODE_PLAIN),
        (2, ZC_K // t, 0, _MODE_PLAIN),
        (cb0, ZC_CB // t, 0, _MODE_PLAIN),
        (cb0 + 1, ZC_CB // t + 1, 0, _MODE_PLAIN),
    ]
    for j in range(2 * D_MODEL // t):
        steps.append((ga0 + j, j, j, _MODE_GATE))
    return np.asarray(steps, dtype=np.int32)


def _in_proj_kernel(wt_ref, ot_ref, bt_ref, md_ref, x_hbm, g_ref, w_ref, b_ref, o_ref, x_buf, xn_ref, stash_ref, sem,
                    *, n_tiles):
    del wt_ref, ot_ref, bt_ref
    m = pl.program_id(0)
    n = pl.program_id(1)
    tm = x_buf.shape[0]

    def x_copy(tile):
        return pltpu.make_async_copy(x_hbm.at[pl.ds(pl.multiple_of(tile * tm, 8), tm)], x_buf, sem)

    def product():
        return jnp.dot(xn_ref[...], w_ref[...], preferred_element_type=F32)

    md = md_ref[n]

    @pl.when(md == _MODE_PLAIN)
    def _():
        o_ref[...] = product()

    @pl.when(md == _MODE_STASH0)
    def _():
        @pl.when(m == 0)
        def _():
            x_copy(0).start()

        x_copy(m).wait()
        x = x_buf[...]
        ms = jnp.mean(x * x, axis=-1, keepdims=True)
        xn_ref[...] = ((x * lax.rsqrt(ms + NORM_EPS)) * g_ref[...]).astype(BF16)
        stash_ref[0] = product()

    @pl.when(md == _MODE_STASH1)
    def _():
        if n_tiles > 1:
            @pl.when(m + 1 < n_tiles)
            def _():
                x_copy(m + 1).start()

        stash_ref[1] = product()

    @pl.when(md == _MODE_MUL0)
    def _():
        o_ref[...] = stash_ref[0] * product()

    @pl.when(md == _MODE_MUL1)
    def _():
        o_ref[...] = stash_ref[1] * product()

    @pl.when(md == _MODE_GATE)
    def _():
        o_ref[...] = jax.nn.sigmoid(product() + b_ref[...])


def _in_proj(x, g_mix, w_in, b_gate, tm):
    t, d = x.shape
    sched = _proj_schedule()
    n_steps = sched.shape[0]
    n_tiles = t // tm
    grid_spec = pltpu.PrefetchScalarGridSpec(
        num_scalar_prefetch=4,
        grid=(n_tiles, n_steps),
        in_specs=[
            pl.BlockSpec(memory_space=pl.ANY),
            pl.BlockSpec((1, d), lambda m, n, wt, ot, bt, md: (0, 0)),
            pl.BlockSpec((d, PROJ_TN), lambda m, n, wt, ot, bt, md: (0, wt[n])),
            pl.BlockSpec((1, PROJ_TN), lambda m, n, wt, ot, bt, md: (0, bt[n])),
        ],
        out_specs=pl.BlockSpec((tm, PROJ_TN), lambda m, n, wt, ot, bt, md: (m, ot[n])),
        scratch_shapes=[pltpu.VMEM((tm, d), F32), pltpu.VMEM((tm, d), BF16), pltpu.VMEM((2, tm, PROJ_TN), F32),
                        pltpu.SemaphoreType.DMA(())],
    )
    vmem = tm * d * 4 + tm * d * 2 + 2 * d * PROJ_TN * 2 + 6 * tm * PROJ_TN * 4 + (8 << 20)
    return pl.pallas_call(
        functools.partial(_in_proj_kernel, n_tiles=n_tiles),
        grid_spec=grid_spec,
        out_shape=jax.ShapeDtypeStruct((t, Z_DIM), F32),
        compiler_params=pltpu.CompilerParams(
            dimension_semantics=("arbitrary", "arbitrary"), vmem_limit_bytes=_vmem_limit(vmem)),
        name="in_proj",
    )(jnp.asarray(sched[:, 0]), jnp.asarray(sched[:, 1]), jnp.asarray(sched[:, 2]), jnp.asarray(sched[:, 3]),
      x, g_mix.reshape(1, d), w_in, b_gate.reshape(1, 2 * D_MODEL))


def _t5_buckets(n_q, n_k):
    half = N_BUCKETS // 2
    max_exact = half // 2
    rel = (np.arange(n_k, dtype=np.int32)[None, :] - WINDOW) - np.arange(n_q, dtype=np.int32)[:, None]
    ret = np.where(rel > 0, half, 0)
    n = np.abs(rel)
    nf = np.maximum(n, 1).astype(np.float32)
    large = max_exact + (np.log(nf / np.float32(max_exact)) / np.float32(math.log(MAX_DISTANCE / max_exact))
                         * np.float32(half - max_exact)).astype(np.int32)
    large = np.minimum(large, half - 1)
    return (ret + np.where(n < max_exact, n, large)).astype(np.int32)


def _build_bias(bias_ref, bucket_ref, table_ref, n_q):
    bucket = bucket_ref[...]
    for kvh in range(N_KV_HEADS):
        for g in range(GROUP):
            h = kvh * GROUP + g
            mat = jnp.zeros(bucket.shape, F32)
            for b in range(N_BUCKETS):
                mat = jnp.where(bucket == b, table_ref[b, h], mat)
            bias_ref[kvh, g * n_q:(g + 1) * n_q, :] = mat


def _attend(q, kwin, vwin, bias_ref, sinks_ref, n_q, invalid):
    outs = []
    grp = lax.broadcasted_iota(I32, (GROUP * n_q, 1), 0) // n_q
    for kvh in range(N_KV_HEADS):
        c0 = kvh * GROUP * HEAD_DIM
        qh = jnp.concatenate([q[:, c0 + g * HEAD_DIM:c0 + (g + 1) * HEAD_DIM] for g in range(GROUP)], axis=0)
        qh = (qh * (HEAD_DIM ** -0.5)).astype(BF16)
        kh = kwin[:, kvh * HEAD_DIM:(kvh + 1) * HEAD_DIM].astype(BF16)
        vh = vwin[:, kvh * HEAD_DIM:(kvh + 1) * HEAD_DIM].astype(BF16)
        s = lax.dot_general(qh, kh, (((1,), (1,)), ((), ())), preferred_element_type=F32) + bias_ref[kvh]
        if invalid is not None:
            s = jnp.where(invalid, -1e30, s)
        sink = jnp.zeros((GROUP * n_q, 1), F32)
        for g in range(GROUP):
            sink = jnp.where(grp == g, sinks_ref[kvh * GROUP + g], sink)
        m = jnp.maximum(jnp.max(s, axis=-1, keepdims=True), sink)
        p = jnp.exp(s - m)
        denom = jnp.sum(p, axis=-1, keepdims=True) + jnp.exp(sink - m)
        o = jnp.dot(p.astype(BF16), vh, preferred_element_type=F32) / denom
        outs.append(o)
    return outs


def _store_heads(o_ref, r0, n_q, outs):
    for kvh, o in enumerate(outs):
        for g in range(GROUP):
            c = (kvh * GROUP + g) * HEAD_DIM
            o_ref[r0:r0 + n_q, c:c + HEAD_DIM] = o[g * n_q:(g + 1) * n_q].astype(o_ref.dtype)


def _attn_prompt_kernel(table_ref, sinks_ref, bucket_ref, q_ref, kp_ref, kc_ref, vp_ref, vc_ref, o_ref, bias_ref,
                        kwin_ref, vwin_ref):
    first = (pl.program_id(0) == 0) & (pl.program_id(1) == 0)

    @pl.when(first)
    def _():
        _build_bias(bias_ref, bucket_ref, table_ref, CHUNK)

    kwin_ref[:WINDOW, :] = kp_ref[...]
    kwin_ref[WINDOW:, :] = kc_ref[...]
    vwin_ref[:WINDOW, :] = vp_ref[...]
    vwin_ref[WINDOW:, :] = vc_ref[...]
    n_k = WINDOW + CHUNK
    seq_start = pl.program_id(1) == 0
    col = lax.broadcasted_iota(I32, (1, n_k), 1)

    def half(h, carry):
        base = pl.multiple_of(h * WINDOW, WINDOW)
        kw = kwin_ref[pl.ds(base, 2 * WINDOW), :]
        vw = vwin_ref[pl.ds(base, 2 * WINDOW), :]
        q_view = q_ref.at[pl.ds(base, WINDOW)]
        o_view = o_ref.at[pl.ds(base, WINDOW)]
        for c in range(WINDOW // CHUNK):
            r0 = c * CHUNK
            invalid = seq_start & (h == 0) & (col + r0 < WINDOW)
            outs = _attend(q_view[r0:r0 + CHUNK, :], kw[r0:r0 + n_k], vw[r0:r0 + n_k], bias_ref, sinks_ref,
                           CHUNK, invalid)
            _store_heads(o_view, r0, CHUNK, outs)
        return carry

    lax.fori_loop(0, q_ref.shape[0] // WINDOW, half, 0)


def _attn_prompt(zz, batch, seq, table, sinks):
    qb = ATTN_Q_ROWS
    assert seq % qb == 0 and qb % WINDOW == 0
    nb = seq // qb
    wpq = qb // WINDOW
    n_k = WINDOW + CHUNK
    bucket = jnp.asarray(_t5_buckets(CHUNK, n_k))
    smem = pl.BlockSpec(memory_space=pltpu.SMEM)
    kblk, vblk = ZC_K // KV_DIM, ZC_V // KV_DIM
    return pl.pallas_call(
        _attn_prompt_kernel,
        grid=(batch, nb),
        in_specs=[
            smem, smem,
            pl.BlockSpec((CHUNK, n_k), lambda b, j: (0, 0)),
            pl.BlockSpec((qb, ATTN_DIM), lambda b, j: (b * nb + j, ZC_Q // ATTN_DIM)),
            pl.BlockSpec((WINDOW, KV_DIM), lambda b, j: ((b * nb + j) * wpq - jnp.minimum(j, 1), kblk)),
            pl.BlockSpec((qb, KV_DIM), lambda b, j: (b * nb + j, kblk)),
            pl.BlockSpec((WINDOW, KV_DIM), lambda b, j: ((b * nb + j) * wpq - jnp.minimum(j, 1), vblk)),
            pl.BlockSpec((qb, KV_DIM), lambda b, j: (b * nb + j, vblk)),
        ],
        out_specs=pl.BlockSpec((qb, ATTN_DIM), lambda b, j: (b * nb + j, 0)),
        out_shape=jax.ShapeDtypeStruct((batch * seq, ATTN_DIM), BF16),
        scratch_shapes=[pltpu.VMEM((N_KV_HEADS, GROUP * CHUNK, n_k), F32),
                        pltpu.VMEM((WINDOW + qb, KV_DIM), F32), pltpu.VMEM((WINDOW + qb, KV_DIM), F32)],
        compiler_params=pltpu.CompilerParams(dimension_semantics=("arbitrary", "arbitrary")),
        name="attn_prompt",
    )(table, sinks, bucket, zz, zz, zz, zz, zz)


def _attn_sample_kernel(table_ref, sinks_ref, bucket_ref, q_ref, k_ref, v_ref, o_ref, bias_ref, *, n_q):
    @pl.when(pl.program_id(0) == 0)
    def _():
        _build_bias(bias_ref, bucket_ref, table_ref, n_q)

    outs = _attend(q_ref[...], k_ref[0], v_ref[0], bias_ref, sinks_ref, n_q, None)
    _store_heads(o_ref, 0, n_q, outs)


def _attn_sample(zz, k_ext, v_ext, batch, n_q, table, sinks):
    n_k = WINDOW + n_q
    bucket = jnp.asarray(_t5_buckets(n_q, n_k))
    smem = pl.BlockSpec(memory_space=pltpu.SMEM)
    return pl.pallas_call(
        functools.partial(_attn_sample_kernel, n_q=n_q),
        grid=(batch,),
        in_specs=[
            smem, smem,
            pl.BlockSpec((n_q, n_k), lambda b: (0, 0)),
            pl.BlockSpec((n_q, ATTN_DIM), lambda b: (b, ZC_Q // ATTN_DIM)),
            pl.BlockSpec((1, n_k, KV_DIM), lambda b: (b, 0, 0)),
            pl.BlockSpec((1, n_k, KV_DIM), lambda b: (b, 0, 0)),
        ],
        out_specs=pl.BlockSpec((n_q, ATTN_DIM), lambda b: (b, 0)),
        out_shape=jax.ShapeDtypeStruct((batch * n_q, ATTN_DIM), BF16),
        scratch_shapes=[pltpu.VMEM((N_KV_HEADS, GROUP * n_q, n_k), F32)],
        compiler_params=pltpu.CompilerParams(dimension_semantics=("arbitrary",)),
        name="attn_sample",
    )(table, sinks, bucket, zz, k_ext, v_ext)


_MERGE_INPUTS = 15


def _merge_kernel(*refs, tm, seq_rows, tiles_per_seq):
    (o_ref, ga_ref, gc_ref, cb_ref, u_ref, h0_ref, h1_ref, x_ref, cw_ref, woa_ref, woc_ref, wout_ref,
     gf_ref, wr_ref, br_ref) = refs[:_MERGE_INPUTS]
    h_ref, hn_ref, idx_ref, wts_ref = refs[-4:]
    u = u_ref[...]
    row = lax.broadcasted_iota(I32, (tm, 1), 0)
    if seq_rows is None:
        at_start = (pl.program_id(0) % tiles_per_seq) == 0
        hist0 = jnp.where(at_start, 0.0, h0_ref[6:7, :])
        hist1 = jnp.where(at_start, 0.0, h1_ref[7:8, :])
        pos = row
    else:
        hist0 = h0_ref[...]
        hist1 = h1_ref[...]
        pos = row % seq_rows
    r1 = pltpu.roll(u, 1, 0)
    r2 = pltpu.roll(u, 2, 0)
    u1 = jnp.where(pos == 0, hist1, r1)
    u2 = jnp.where(pos == 0, hist0, jnp.where(pos == 1, hist1, r2))
    cw = cw_ref[...]
    y = cw[0:1, :] * u2
    y = y + cw[1:2, :] * u1
    y = y + cw[2:3, :] * u
    c = (cb_ref[...] * y).astype(BF16)
    t = ga_ref[...] * jnp.dot(o_ref[...], woa_ref[...], preferred_element_type=F32)
    t = t + gc_ref[...] * jnp.dot(c, woc_ref[...], preferred_element_type=F32)
    h = x_ref[...] + jnp.dot(t.astype(BF16), wout_ref[...], preferred_element_type=F32)
    h_ref[...] = h
    ms = jnp.mean(h * h, axis=-1, keepdims=True)
    hn = (h * lax.rsqrt(ms + NORM_EPS)) * gf_ref[...]
    hn_ref[...] = hn
    logits = lax.dot_general(wr_ref[...], hn.astype(BF16), (((1,), (1,)), ((), ())),
                             preferred_element_type=F32) + br_ref[...]
    erow = lax.broadcasted_iota(I32, logits.shape, 0).astype(F32)
    vals, idxs = [], []
    for _ in range(TOP_K):
        mx = jnp.max(logits, axis=0, keepdims=True)
        ix = jnp.min(jnp.where(logits == mx, erow, float(N_EXPERTS)), axis=0, keepdims=True)
        vals.append(mx)
        idxs.append(ix)
        logits = jnp.where(erow == ix, -jnp.inf, logits)
    es = [jnp.exp(v - vals[0]) for v in vals]
    tot = es[0]
    for e in es[1:]:
        tot = tot + e
    idx_ref[...] = jnp.concatenate(idxs, axis=0).astype(I32)
    wts_ref[...] = jnp.concatenate([e / tot for e in es], axis=0)


def _merge(o, zz, x, hist0, hist1, conv_w, woa, woc, wout, g_ffn, wr_t, br, tm, seq_rows, seq_len,
           hn_rows, hn_row0, hn_prev):
    t = x.shape[0]
    d = D_MODEL
    assert hn_row0 % tm == 0
    extra_specs, extra_args, aliases = [], [], {}
    if hn_prev is not None:
        extra_specs, extra_args, aliases = [pl.BlockSpec(memory_space=pl.ANY)], [hn_prev], {_MERGE_INPUTS: 1}
    full = lambda shape: pl.BlockSpec(shape, lambda i: (0,) * len(shape))
    if seq_rows is None:
        tiles_per_seq = seq_len // tm
        hist_spec = pl.BlockSpec((8, CONV_DIM), lambda i: (jnp.maximum(i * (tm // 8) - 1, 0), ZC_U // CONV_DIM))
        h0_spec = h1_spec = hist_spec
        h0_arg = h1_arg = zz
    else:
        tiles_per_seq = None
        h0_spec = h1_spec = pl.BlockSpec((tm, CONV_DIM), lambda i: (i, 0))
        h0_arg, h1_arg = hist0, hist1
    kern = functools.partial(_merge_kernel, tm=tm, seq_rows=seq_rows, tiles_per_seq=tiles_per_seq)
    vmem = (2 * tm * (ATTN_DIM * 2 + 5 * d * 4 + 4 * CONV_DIM * 4) + 2 * 2 * (2 * ATTN_DIM * d + d * d)
            + 10 * tm * d * 4 + (8 << 20))
    return pl.pallas_call(
        kern,
        grid=(t // tm,),
        in_specs=[
            pl.BlockSpec((tm, ATTN_DIM), lambda i: (i, 0)),
            pl.BlockSpec((tm, d), lambda i: (i, ZC_GA // d)),
            pl.BlockSpec((tm, d), lambda i: (i, ZC_GC // d)),
            pl.BlockSpec((tm, CONV_DIM), lambda i: (i, ZC_CB // CONV_DIM)),
            pl.BlockSpec((tm, CONV_DIM), lambda i: (i, ZC_U // CONV_DIM)),
            h0_spec, h1_spec,
            pl.BlockSpec((tm, d), lambda i: (i, 0)),
            full((3, CONV_DIM)),
            full((ATTN_DIM, d)), full((CONV_DIM, d)), full((d, d)),
            full((1, d)), full((N_EXPERTS, d)), full((N_EXPERTS, 1)),
        ] + extra_specs,
        out_specs=[
            pl.BlockSpec((tm, d), lambda i: (i, 0)),
            pl.BlockSpec((tm, d), lambda i: (hn_row0 // tm + i, 0)),
            pl.BlockSpec((TOP_K, tm), lambda i: (0, i)),
            pl.BlockSpec((TOP_K, tm), lambda i: (0, i)),
        ],
        out_shape=[
            jax.ShapeDtypeStruct((t, d), F32),
            jax.ShapeDtypeStruct((hn_rows, d), F32),
            jax.ShapeDtypeStruct((TOP_K, t), I32),
            jax.ShapeDtypeStruct((TOP_K, t), F32),
        ],
        input_output_aliases=aliases,
        compiler_params=pltpu.CompilerParams(dimension_semantics=("arbitrary",), vmem_limit_bytes=_vmem_limit(vmem)),
        name="merge_prompt" if seq_rows is None else "merge_sample",
    )(o, zz, zz, zz, zz, h0_arg, h1_arg, x, conv_w, woa, woc, wout, g_ffn.reshape(1, d), wr_t,
      br.reshape(N_EXPERTS, 1), *extra_args)


ROUTE_TILE = 32768
ROUTE_SUB = 384


def _route_kernel(idx_ref, dest_ref, cnt_ref, cnt_acc, carry, pst, tri_ref):
    p = pl.program_id(0)
    j = pl.program_id(1)
    sub = tri_ref.shape[0]
    n_sub = idx_ref.shape[1] // sub
    erow = lax.broadcasted_iota(I32, (N_EXPERTS, sub), 0)

    def membership(s):
        idx = idx_ref[:, pl.ds(pl.multiple_of(s * sub, LANES), sub)]
        masks = [(erow == idx[k:k + 1, :]).astype(F32) for k in range(TOP_K)]
        mtot = masks[0]
        for m in masks[1:]:
            mtot = mtot + m
        return masks, mtot, jnp.sum(mtot, axis=1, keepdims=True)

    @pl.when((p == 0) & (j == 0))
    def _():
        cnt_acc[...] = jnp.zeros_like(cnt_acc)

    @pl.when(p == 0)
    def _():
        def body(s, c):
            _, _, piece_cnt = membership(s)
            cnt_acc[...] += jnp.broadcast_to(piece_cnt, cnt_acc.shape)
            return c

        lax.fori_loop(0, n_sub, body, 0)

    @pl.when((p == 1) & (j == 0))
    def _():
        cnt = cnt_acc[:, 0:1]
        padded = jnp.floor((cnt + float(ROW_BLK - 1)) / float(ROW_BLK)) * float(ROW_BLK)
        r = lax.broadcasted_iota(I32, (N_EXPERTS, N_EXPERTS), 0)
        c = lax.broadcasted_iota(I32, (N_EXPERTS, N_EXPERTS), 1)
        rowv = jnp.sum(jnp.where(r == c, padded, 0.0), axis=0, keepdims=True)
        start = jnp.sum(jnp.where(c < r, rowv, 0.0), axis=1, keepdims=True)
        pst[...] = jnp.broadcast_to(start, pst.shape)
        carry[...] = jnp.zeros_like(carry)
        a = lax.broadcasted_iota(I32, (sub, sub), 0)
        b = lax.broadcasted_iota(I32, (sub, sub), 1)
        tri_ref[...] = (a < b).astype(BF16)

    @pl.when(p == 1)
    def _():
        def body(s, c):
            masks, mtot, piece_cnt = membership(s)
            excl = jnp.dot(mtot.astype(BF16), tri_ref[...], preferred_element_type=F32)
            val = pst[:, 0:1] + carry[:, 0:1] + excl
            dest = jnp.concatenate([jnp.sum(m * val, axis=0, keepdims=True) for m in masks], axis=0)
            dest_ref[:, pl.ds(pl.multiple_of(s * sub, LANES), sub)] = dest.astype(I32)
            carry[...] += jnp.broadcast_to(piece_cnt, carry.shape)
            return c

        lax.fori_loop(0, n_sub, body, 0)
        cnt_ref[...] = cnt_acc[...].astype(I32)


def _route(idx_t):
    t = idx_t.shape[1]
    assert t % ROUTE_SUB == 0
    tr = max(c for c in range(ROUTE_SUB, min(ROUTE_TILE, t) + 1, ROUTE_SUB) if t % c == 0)
    return pl.pallas_call(
        _route_kernel,
        grid=(2, t // tr),
        in_specs=[pl.BlockSpec((TOP_K, tr), lambda p, j: (0, j))],
        out_specs=[
            pl.BlockSpec((TOP_K, tr), lambda p, j: (0, j * p)),
            pl.BlockSpec((N_EXPERTS, LANES), lambda p, j: (0, 0)),
        ],
        out_shape=[jax.ShapeDtypeStruct((TOP_K, t), I32), jax.ShapeDtypeStruct((N_EXPERTS, LANES), I32)],
        scratch_shapes=[pltpu.VMEM((N_EXPERTS, LANES), F32), pltpu.VMEM((N_EXPERTS, LANES), F32),
                        pltpu.VMEM((N_EXPERTS, LANES), F32), pltpu.VMEM((ROUTE_SUB, ROUTE_SUB), BF16)],
        compiler_params=pltpu.CompilerParams(dimension_semantics=("arbitrary", "arbitrary")),
        name="route",
    )(idx_t)


DISPATCH_TILE = 1376
DMA_UNROLL = 8


def _dispatch_kernel(dest_ref, hn_ref, xs_hbm, sem):
    tt = hn_ref.shape[0]

    def issue(t, c):
        for k in range(TOP_K):
            dst = xs_hbm.at[pl.ds(dest_ref[0, TOP_K * t + k], 1)]
            pltpu.make_async_copy(hn_ref.at[pl.ds(t, 1)], dst, sem).start(priority=k % 2)
        return c

    lax.fori_loop(0, tt, issue, 0, unroll=DMA_UNROLL)

    def drain(t, c):
        for k in range(TOP_K):
            pltpu.make_async_copy(hn_ref.at[pl.ds(0, 1)], xs_hbm.at[pl.ds(0, 1)], sem).wait()
        return c

    lax.fori_loop(0, tt, drain, 0, unroll=DMA_UNROLL)


def _dispatch(dest, hn, n_slots):
    t, d = hn.shape
    tt = max(c for c in range(32, DISPATCH_TILE + 1, 32) if t % c == 0)
    return pl.pallas_call(
        _dispatch_kernel,
        grid=(t // tt,),
        in_specs=[pl.BlockSpec((1, TOP_K * tt), lambda i: (0, i), memory_space=pltpu.SMEM),
                  pl.BlockSpec((tt, d), lambda i: (i, 0))],
        out_specs=pl.BlockSpec(memory_space=pl.ANY),
        out_shape=jax.ShapeDtypeStruct((n_slots, d), hn.dtype),
        scratch_shapes=[pltpu.SemaphoreType.DMA(())],
        compiler_params=pltpu.CompilerParams(dimension_semantics=("arbitrary",)),
        name="dispatch",
    )(dest, hn)


MAX_UNIT_BLKS = 5


def _matmul_units(n_blocks):
    units, b = [], 0
    while b < n_blocks:
        size = min(MAX_UNIT_BLKS, n_blocks - b)
        units.append((b, size))
        b += size
    return units


def _expert_kernel(we_ref, ws_ref, wnb_ref, wvalid_ref, wact_ref, wtot_ref,
                   xs_hbm, wg_ref, wu_ref, wd_ref, bias_ref, ys_hbm,
                   x_sb, acc, wgu_bf, wd_bf, stage, sem_in, sem_out):
    del we_ref, wact_ref
    w = pl.program_id(0)
    f = pl.program_id(1)
    n_items = wtot_ref[0]
    last_f = N_FF_TILES - 1
    slot = w % 2
    nb = wnb_ref[w]
    row = lax.broadcasted_iota(I32, (ROW_BLK, 1), 0)

    def x_copy(item, j):
        src = xs_hbm.at[pl.ds(pl.multiple_of(ws_ref[item] + j * ROW_BLK, ROW_BLK), ROW_BLK)]
        return pltpu.make_async_copy(src, stage, sem_in)

    def x_cast(item, j, sl):
        r0 = pl.multiple_of(j * ROW_BLK, ROW_BLK)
        x_sb[sl, pl.ds(r0, ROW_BLK), :] = jnp.where(row + r0 < wvalid_ref[item], stage[...], 0.0).astype(BF16)

    def y_copy(item, j, sl):
        r0 = pl.multiple_of(j * ROW_BLK, ROW_BLK)
        dst = ys_hbm.at[pl.ds(pl.multiple_of(ws_ref[item] + r0, ROW_BLK), ROW_BLK)]
        return pltpu.make_async_copy(acc.at[sl, pl.ds(r0, ROW_BLK)], dst, sem_out.at[sl])

    def y_wait_all(item, sl):
        def body(j, c):
            y_copy(item, j, sl).wait()
            return c

        lax.fori_loop(0, wnb_ref[item], body, 0)

    @pl.when((w == 0) & (f == 0))
    def _first_rows():
        def body(j, c):
            cp = x_copy(0, j)
            cp.start()
            cp.wait()
            x_cast(0, j, 0)
            return c

        lax.fori_loop(0, wnb_ref[0], body, 0)

    nxt = jnp.minimum(w + 1, n_items - 1)
    prefetch = (w + 1 < n_items) & (f < wnb_ref[nxt])

    @pl.when(prefetch)
    def _():
        x_copy(nxt, f).start()

    @pl.when((f == 0) & (w >= 2))
    def _reclaim_acc():
        y_wait_all(jnp.maximum(w - 2, 0), slot)

    @pl.when(nb > 0)
    def _compute():
        bias = bias_ref[0, 0]
        bg = bias[:, :FF_TILE]
        bu = bias[:, FF_TILE:2 * FF_TILE]
        bd = bias[:, 2 * FF_TILE:]

        def cast_weights():
            wgu_bf[:, :FF_TILE] = wg_ref[0].astype(BF16)
            wgu_bf[:, FF_TILE:] = wu_ref[0].astype(BF16)
            wd_bf[...] = wd_ref[0].astype(BF16)

        def update(r0, rows, first):
            x = x_sb[slot, r0:r0 + rows, :]
            gu = jnp.dot(x, wgu_bf[...], preferred_element_type=F32)
            g = jnp.minimum(gu[:, :FF_TILE] + bg, SWIGLU_LIMIT)
            u = jnp.clip(gu[:, FF_TILE:] + bu, -SWIGLU_LIMIT, SWIGLU_LIMIT)
            hh = (u + 1.0) * (g * jax.nn.sigmoid(SWIGLU_ALPHA * g))
            part = jnp.dot(hh.astype(BF16), wd_bf[...], preferred_element_type=F32)
            if first:
                acc[slot, r0:r0 + rows, :] = part + bd
            else:
                acc[slot, r0:r0 + rows, :] += part

        def sweep(first):
            for n in range(1, SUPER_BLKS + 1):
                @pl.when(nb == n)
                def _(n=n):
                    cast_weights()
                    for b0, size in _matmul_units(n):
                        update(b0 * ROW_BLK, size * ROW_BLK, first)

        @pl.when(f == 0)
        def _():
            sweep(True)

        @pl.when(f > 0)
        def _():
            sweep(False)

    @pl.when((f == last_f) & (nb > 0))
    def _store():
        def body(j, c):
            y_copy(w, j, slot).start()
            return c

        lax.fori_loop(0, nb, body, 0)

    @pl.when(prefetch)
    def _():
        x_copy(nxt, f).wait()
        x_cast(nxt, f, 1 - slot)

    @pl.when((w == n_items - 1) & (f == last_f))
    def _drain():
        y_wait_all(jnp.maximum(w - 1, 0), 1 - slot)
        y_wait_all(w, slot)


def _experts(xs, tables, w_gate_up, b_gate_up, w_down, b_down):
    n_slots, d = xs.shape
    total = jnp.sum(tables[4]).astype(I32).reshape(1)
    tf = FF_TILE
    nf = N_FF_TILES
    bgu = b_gate_up.reshape(N_EXPERTS, 2, nf, tf).transpose(0, 2, 1, 3).reshape(N_EXPERTS, nf, 2 * tf)
    bdn = jnp.broadcast_to(b_down[:, None, :], (N_EXPERTS, nf, d))
    biases = jnp.concatenate([bgu, bdn], axis=-1).reshape(N_EXPERTS, nf, 1, 2 * tf + d)

    def ff(f, act, w):
        return f * act[w] + (nf - 1) * (1 - act[w])

    grid_spec = pltpu.PrefetchScalarGridSpec(
        num_scalar_prefetch=6,
        grid=(total[0], nf),
        in_specs=[
            pl.BlockSpec(memory_space=pl.ANY),
            pl.BlockSpec((1, d, tf), lambda w, f, we, ws, wnb, wv, act, tot: (we[w], 0, ff(f, act, w))),
            pl.BlockSpec((1, d, tf), lambda w, f, we, ws, wnb, wv, act, tot: (we[w], 0, nf + ff(f, act, w))),
            pl.BlockSpec((1, tf, d), lambda w, f, we, ws, wnb, wv, act, tot: (we[w], ff(f, act, w), 0)),
            pl.BlockSpec((1, 1, 1, 2 * tf + d),
                         lambda w, f, we, ws, wnb, wv, act, tot: (we[w], ff(f, act, w), 0, 0)),
        ],
        out_specs=pl.BlockSpec(memory_space=pl.ANY),
        scratch_shapes=[
            pltpu.VMEM((2, SUPER_ROWS, d), BF16),
            pltpu.VMEM((2, SUPER_ROWS, d), F32),
            pltpu.VMEM((d, 2 * tf), BF16), pltpu.VMEM((tf, d), BF16),
            pltpu.VMEM((ROW_BLK, d), F32),
            pltpu.SemaphoreType.DMA(()), pltpu.SemaphoreType.DMA((2,)),
        ],
    )
    vmem = (2 * SUPER_ROWS * d * 6 + 2 * 3 * d * tf * 4 + 3 * d * tf * 2 + ROW_BLK * d * 4
            + 6 * ROW_BLK * d * 4 + (6 << 20))
    return pl.pallas_call(
        _expert_kernel,
        grid_spec=grid_spec,
        out_shape=jax.ShapeDtypeStruct((n_slots, d), F32),
        compiler_params=pltpu.CompilerParams(
            dimension_semantics=("arbitrary", "arbitrary"), vmem_limit_bytes=_vmem_limit(vmem)),
        name="experts",
    )(*tables, total, xs, w_gate_up, w_gate_up, w_down, biases)


def _work_tables(counts, n_items):
    padded = (counts + ROW_BLK - 1) // ROW_BLK * ROW_BLK
    pstart = jnp.cumsum(padded) - padded
    items_e = (padded + SUPER_ROWS - 1) // SUPER_ROWS
    item_end = jnp.cumsum(items_e)
    item_start = item_end - items_e
    w = jnp.arange(n_items, dtype=I32)
    total = item_end[-1]
    active = (w < total).astype(I32)
    wl = jnp.minimum(w, total - 1)
    e = jnp.minimum(jnp.searchsorted(item_end, wl, side='right'), N_EXPERTS - 1).astype(I32)
    local = wl - item_start[e]
    start = pstart[e] + local * SUPER_ROWS
    rows = jnp.minimum(SUPER_ROWS, padded[e] - local * SUPER_ROWS)
    valid = jnp.clip(counts[e] - local * SUPER_ROWS, 0, rows)
    nb = (rows // ROW_BLK) * active
    return (e.astype(I32), start.astype(I32), nb.astype(I32), valid.astype(I32), active)


COMBINE_TILE = 256


def _combine_kernel(dest_ref, dest_next_ref, wts_ref, h_ref, g_ref, ys_hbm, out_ref, buf, sem, *, n):
    i = pl.program_id(0)
    slot = i % 2
    tt = h_ref.shape[0]

    def issue(dref, sl):
        def body(t, c):
            for k in range(TOP_K):
                src = ys_hbm.at[pl.ds(dref[0, TOP_K * t + k], 1)]
                pltpu.make_async_copy(src, buf.at[sl, k, pl.ds(t, 1)], sem.at[sl]).start(priority=k % 2)
            return c

        lax.fori_loop(0, tt, body, 0, unroll=DMA_UNROLL)

    def issue_unrolled(dref, sl):
        for t in range(tt):
            for k in range(TOP_K):
                src = ys_hbm.at[pl.ds(dref[0, TOP_K * t + k], 1)]
                pltpu.make_async_copy(src, buf.at[sl, k, pl.ds(t, 1)], sem.at[sl]).start(priority=k % 2)

    @pl.when(i == 0)
    def _():
        issue(dest_ref, 0)

    def drain(t, c):
        for k in range(TOP_K):
            pltpu.make_async_copy(ys_hbm.at[pl.ds(0, 1)], buf.at[slot, k, pl.ds(0, 1)], sem.at[slot]).wait()
        return c

    lax.fori_loop(0, tt, drain, 0, unroll=DMA_UNROLL)

    def finish():
        wts = wts_ref[...]
        moe = wts[:, 0:1] * buf[slot, 0]
        for k in range(1, TOP_K):
            moe = moe + wts[:, k:k + 1] * buf[slot, k]
        y = h_ref[...] + moe
        ms = jnp.mean(y * y, axis=-1, keepdims=True)
        out_ref[...] = (y * lax.rsqrt(ms + NORM_EPS)) * g_ref[...]

    if n > 1:
        @pl.when(i + 1 < n)
        def _():
            issue_unrolled(dest_next_ref, 1 - slot)
            finish()

        @pl.when(i + 1 >= n)
        def _():
            finish()
    else:
        finish()


def _combine(dest, wts, h, g_final, ys):
    t, d = h.shape
    tt = min(COMBINE_TILE, t)
    assert t % tt == 0
    n = t // tt
    return pl.pallas_call(
        functools.partial(_combine_kernel, n=n),
        grid=(n,),
        in_specs=[
            pl.BlockSpec((1, TOP_K * tt), lambda i: (0, i), memory_space=pltpu.SMEM),
            pl.BlockSpec((1, TOP_K * tt), lambda i: (0, jnp.minimum(i + 1, n - 1)), memory_space=pltpu.SMEM),
            pl.BlockSpec((tt, TOP_K), lambda i: (i, 0)),
            pl.BlockSpec((tt, d), lambda i: (i, 0)),
            pl.BlockSpec((1, d), lambda i: (0, 0)),
            pl.BlockSpec(memory_space=pl.ANY),
        ],
        out_specs=pl.BlockSpec((tt, d), lambda i: (i, 0)),
        out_shape=jax.ShapeDtypeStruct((t, d), F32),
        scratch_shapes=[pltpu.VMEM((2, TOP_K, tt, d), F32), pltpu.SemaphoreType.DMA((2,))],
        compiler_params=pltpu.CompilerParams(dimension_semantics=("arbitrary",)),
        name="combine",
    )(dest, dest, wts, h, g_final.reshape(1, d), ys)


def kernel(x_prompt, x_sample, cache_attn_k, cache_attn_v, state_conv, rel_bias_table, norm_mix_g, w_in, b_gate,
           attn_sinks, conv_w, w_o_attn, w_o_conv, w_out, norm_ffn_g, w_router, b_router, w_gate_up, b_gate_up,
           w_down, b_down, norm_final_g):
    batch, seq, d = x_prompt.shape
    dbatch, dseq, _ = x_sample.shape
    assert norm_mix_g.shape[0] == 1 and d == D_MODEL
    t_p, t_s = batch * seq, dbatch * dseq
    xp = x_prompt.reshape(t_p, d)
    xs = x_sample.reshape(t_s, d)

    woa = w_o_attn[0].astype(BF16)
    woc = w_o_conv[0].astype(BF16)
    wout = w_out[0].astype(BF16)
    wr_t = w_router[0].T.astype(BF16)

    w_in_bf = w_in[0].astype(BF16)
    zz_p = _in_proj(xp, norm_mix_g[0], w_in_bf, b_gate[0], tm=min(2048, t_p))
    zz_s = _in_proj(xs, norm_mix_g[0], w_in_bf, b_gate[0], tm=t_s)

    o_p = _attn_prompt(zz_p, batch, seq, rel_bias_table, attn_sinks[0])
    k_new = zz_s[:, ZC_K:ZC_K + KV_DIM].reshape(dbatch, dseq, KV_DIM)
    v_new = zz_s[:, ZC_V:ZC_V + KV_DIM].reshape(dbatch, dseq, KV_DIM)
    k_ext = jnp.concatenate([cache_attn_k[0].reshape(dbatch, WINDOW, KV_DIM), k_new], axis=1)
    v_ext = jnp.concatenate([cache_attn_v[0].reshape(dbatch, WINDOW, KV_DIM), v_new], axis=1)
    o_s = _attn_sample(zz_s, k_ext, v_ext, dbatch, dseq, rel_bias_table, attn_sinks[0])

    hist0 = jnp.repeat(state_conv[0][:, 0, :], dseq, axis=0)
    hist1 = jnp.repeat(state_conv[0][:, 1, :], dseq, axis=0)
    merge_args = (conv_w[0], woa, woc, wout, norm_ffn_g[0], wr_t, b_router[0])
    t_all = t_p + t_s
    h_p, hn, idx_p, wts_p = _merge(o_p, zz_p, xp, None, None, *merge_args, tm=256, seq_rows=None, seq_len=seq,
                                   hn_rows=t_all, hn_row0=0, hn_prev=None)
    h_s, hn, idx_s, wts_s = _merge(o_s, zz_s, xs, hist0, hist1, *merge_args, tm=t_s, seq_rows=dseq, seq_len=dseq,
                                   hn_rows=t_all, hn_row0=t_p, hn_prev=hn)

    idx_t = jnp.concatenate([idx_p, idx_s], axis=1)
    wts = jnp.concatenate([wts_p, wts_s], axis=1).T
    n_assign = t_all * TOP_K
    n_slots = (n_assign + N_EXPERTS * (ROW_BLK - 1) + ROW_BLK - 1) // ROW_BLK * ROW_BLK
    n_items = N_EXPERTS + (n_assign - N_EXPERTS) // SUPER_ROWS

    dest, cnt = _route(idx_t)
    tables = _work_tables(cnt[:, 0], n_items)
    dest = dest.T.reshape(1, -1)
    xs_sorted = _dispatch(dest, hn, n_slots)
    ys = _experts(xs_sorted, tables, w_gate_up[0], b_gate_up[0], w_down[0], b_down[0])
    y_p = _combine(dest[:, :TOP_K * t_p], wts[:t_p], h_p, norm_final_g, ys)
    y_s = _combine(dest[:, TOP_K * t_p:], wts[t_p:], h_s, norm_final_g, ys)

    zz_p3 = zz_p.reshape(batch, seq, Z_DIM)

    def kv_tail(c0):
        return zz_p3[:, seq - WINDOW:, c0:c0 + KV_DIM].reshape(batch, WINDOW, N_KV_HEADS, HEAD_DIM)[None]

    u_s = zz_s[:, ZC_U:ZC_U + CONV_DIM].reshape(dbatch, dseq, CONV_DIM)
    conv_hist = state_conv.shape[2]
    new_conv_p = zz_p3[:, seq - conv_hist:, ZC_U:ZC_U + CONV_DIM][None]
    new_conv_s = jnp.concatenate([state_conv[0], u_s], axis=1)[:, -conv_hist:][None]
    new_k_s = k_ext[:, -WINDOW:].reshape(dbatch, WINDOW, N_KV_HEADS, HEAD_DIM)[None]
    new_v_s = v_ext[:, -WINDOW:].reshape(dbatch, WINDOW, N_KV_HEADS, HEAD_DIM)[None]
    return (y_p.reshape(batch, seq, d), y_s.reshape(dbatch, dseq, d),
            kv_tail(ZC_K), kv_tail(ZC_V), new_conv_p, new_k_s, new_v_s, new_conv_s)
```
